```python
import jax
import jax.numpy as jnp
from jax import lax
import numpy as np

D_MODEL = 1024
BATCH = 2
SEQ = 8192
DEPTH = 2

HEAD_DIM = 64
N_GROUPS_A = 8
N_HEADS_B = 8
N_HEADS_C = 16
WIDTH_A = N_GROUPS_A * HEAD_DIM
WIDTH_B = N_HEADS_B * HEAD_DIM
WIDTH_C = N_HEADS_C * HEAD_DIM
SGU_CHUNK = 128
MOBA_BLOCK = 256
MOBA_TOPK = 3
IDX_HEADS = 8
IDX_DIM = 64
DSA_TOPK = 256
ROPE_THETA = 500000.0
ROPE_DIM = HEAD_DIM // 4
D_FF = ((8 * D_MODEL + 3 * 256 - 1) // (3 * 256)) * 256
PLE_DIM = 256
Q_BLOCK = 128
EPS = 1e-6
NEG = -1e30
N_EVEN = (DEPTH + 1) // 2
N_ODD = DEPTH // 2
IN_EVEN = 2 * WIDTH_A + 3 * WIDTH_B
IN_ODD = 3 * WIDTH_C + IDX_HEADS * IDX_DIM + IDX_DIM + IDX_HEADS

kernel_name = 'hybrid_sgu_moba_dsa_trunk'


def rmsnorm(x, g):
    xf = x.astype(jnp.float32)
    y = xf * lax.rsqrt(jnp.mean(xf * xf, axis=-1, keepdims=True) + EPS)
    return (y * g.astype(jnp.float32)).astype(x.dtype)


def layernorm(x, g, b):
    xf = x.astype(jnp.float32)
    mu = jnp.mean(xf, axis=-1, keepdims=True)
    xc = xf - mu
    var = jnp.mean(xc * xc, axis=-1, keepdims=True)
    y = xc * lax.rsqrt(var + EPS) * g.astype(jnp.float32) + b.astype(jnp.float32)
    return y.astype(x.dtype)


def rope_tables(positions):
    inv_freq = ROPE_THETA ** (-jnp.arange(0, ROPE_DIM, 2, dtype=jnp.float32) / ROPE_DIM)
    ang = positions.astype(jnp.float32)[..., None] * inv_freq
    return jnp.cos(ang), jnp.sin(ang)


def partial_rope(x, cos, sin):
    half = ROPE_DIM // 2
    c = cos[:, :, None, :]
    s = sin[:, :, None, :]
    xf = x.astype(jnp.float32)
    x1 = xf[..., :half]
    x2 = xf[..., half:ROPE_DIM]
    out = jnp.concatenate([x1 * c - x2 * s, x2 * c + x1 * s, xf[..., ROPE_DIM:]], axis=-1)
    return out.astype(x.dtype)


def spatial_gating(u, v, ln_g, ln_b, w_s, b_s):
    B, S, _ = u.shape
    nc = S // SGU_CHUNK
    vg = v.reshape(B, S, N_GROUPS_A, HEAD_DIM)
    vg = layernorm(vg, ln_g.reshape(N_GROUPS_A, HEAD_DIM), ln_b.reshape(N_GROUPS_A, HEAD_DIM))
    vg = vg.reshape(B, nc, SGU_CHUNK, N_GROUPS_A, HEAD_DIM)
    causal = jnp.tril(jnp.ones((SGU_CHUNK, SGU_CHUNK), dtype=bool))
    w = jnp.where(causal[None], w_s, 0.0).astype(vg.dtype)
    mixed = jnp.einsum('gts,bcsgd->bctgd', w, vg) + b_s.T[None, None, :, :, None].astype(vg.dtype)
    return u * mixed.reshape(B, S, WIDTH_A)


def moba_attention(q, k, v):
    B, S, H, Dh = q.shape
    nb = -(-S // MOBA_BLOCK)
    pad = nb * MOBA_BLOCK - S
    kp = jnp.pad(k, ((0, 0), (0, pad), (0, 0), (0, 0)))
    vp = jnp.pad(v, ((0, 0), (0, pad), (0, 0), (0, 0)))
    kbh = kp.reshape(B, nb, MOBA_BLOCK, H, Dh).transpose(0, 3, 1, 2, 4)
    vbh = vp.reshape(B, nb, MOBA_BLOCK, H, Dh).transpose(0, 3, 1, 2, 4)
    k_mean = jnp.mean(kbh.astype(jnp.float32), axis=3)
    q_blk = jnp.arange(S) // MOBA_BLOCK
    gate = jnp.einsum('bshd,bhnd->bhsn', q.astype(jnp.float32), k_mean)
    past = jnp.arange(nb)[None, :] < q_blk[:, None]
    gate = jnp.where(past[None, None], gate, NEG)
    kk = min(MOBA_TOPK, nb)
    _, sel = lax.top_k(gate, kk)
    sel_valid = sel < q_blk[None, None, :, None]
    scale = Dh ** -0.5
    b_i = jnp.arange(B)[:, None, None, None]
    h_i = jnp.arange(H)[None, :, None, None]

    def one_block(ci):
        start = ci * Q_BLOCK
        qpos = start + jnp.arange(Q_BLOCK)
        qc = lax.dynamic_slice_in_dim(q, start, Q_BLOCK, axis=1)
        selc = lax.dynamic_slice_in_dim(sel, start, Q_BLOCK, axis=2)
        validc = lax.dynamic_slice_in_dim(sel_valid, start, Q_BLOCK, axis=2)
        kg = kbh[b_i, h_i, selc]
        vg = vbh[b_i, h_i, selc]
        s_sel = jnp.einsum('bqhd,bhqkmd->bhqkm', qc, kg, preferred_element_type=jnp.float32) * scale
        s_sel = jnp.where(validc[..., None], s_sel, NEG)
        blk = start // MOBA_BLOCK
        k_own = lax.dynamic_slice_in_dim(kp, blk * MOBA_BLOCK, MOBA_BLOCK, axis=1)
        v_own = lax.dynamic_slice_in_dim(vp, blk * MOBA_BLOCK, MOBA_BLOCK, axis=1)
        kpos = blk * MOBA_BLOCK + jnp.arange(MOBA_BLOCK)
        s_own = jnp.einsum('bqhd,bmhd->bhqm', qc, k_own, preferred_element_type=jnp.float32) * scale
        s_own = jnp.where((kpos[None, :] <= qpos[:, None])[None, None], s_own, NEG)
        s_all = jnp.concatenate([s_sel.reshape(B, H, Q_BLOCK, kk * MOBA_BLOCK), s_own], axis=-1)
        pr = jax.nn.softmax(s_all, axis=-1)
        p_sel = pr[..., :kk * MOBA_BLOCK].reshape(B, H, Q_BLOCK, kk, MOBA_BLOCK).astype(vg.dtype)
        p_own = pr[..., kk * MOBA_BLOCK:].astype(v_own.dtype)
        o = (jnp.einsum('bhqkm,bhqkmd->bqhd', p_sel, vg, preferred_element_type=jnp.float32)
             + jnp.einsum('bhqm,bmhd->bqhd', p_own, v_own, preferred_element_type=jnp.float32))
        return o.astype(q.dtype)

    out = lax.map(one_block, jnp.arange(S // Q_BLOCK))
    return jnp.moveaxis(out, 0, 1).reshape(B, S, H, Dh)


def dsa_attention(q, k, v, q_idx, k_idx, w_idx):
    B, S, H, Dh = q.shape
    n_sel = min(DSA_TOPK, S // 4)
    scale = Dh ** -0.5
    b_i = jnp.arange(B)[:, None, None]
    kpos = jnp.arange(S)

    def one_block(ci):
        start = ci * Q_BLOCK
        qpos = start + jnp.arange(Q_BLOCK)
        qi = lax.dynamic_slice_in_dim(q_idx, start, Q_BLOCK, axis=1)
        wi = lax.dynamic_slice_in_dim(w_idx, start, Q_BLOCK, axis=1).astype(jnp.float32)
        logits = jnp.einsum('bqid,bsd->bqis', qi, k_idx, preferred_element_type=jnp.float32)
        score = jnp.einsum('bqis,bqi->bqs', jax.nn.relu(logits), wi)
        score = jnp.where(kpos[None, None, :] <= qpos[None, :, None], score, NEG)
        _, idx = lax.top_k(score, n_sel)
        valid = idx <= qpos[None, :, None]
        kg = k[b_i, idx]
        vg = v[b_i, idx]
        qc = lax.dynamic_slice_in_dim(q, start, Q_BLOCK, axis=1)
        s = jnp.einsum('bqhd,bqnhd->bhqn', qc, kg, preferred_element_type=jnp.float32) * scale
        s = jnp.where(valid[:, None], s, NEG)
        pr = jax.nn.softmax(s, axis=-1)
        o = jnp.einsum('bhqn,bqnhd->bqhd', pr.astype(vg.dtype), vg, preferred_element_type=jnp.float32)
        return o.astype(q.dtype)

    out = lax.map(one_block, jnp.arange(S // Q_BLOCK))
    return jnp.moveaxis(out, 0, 1).reshape(B, S, H, Dh)


def setup_inputs(seed: int = 0) -> dict:
    key = jax.random.key(seed)
    ks = jax.random.split(key, 24)
    f32 = jnp.float32

    def dense(k, shape, fan_in):
        return jax.random.normal(k, shape, f32) * (fan_in ** -0.5)

    def gain(k, shape):
        return 1.0 + 0.02 * jax.random.normal(k, shape, f32)

    def small(k, shape):
        return 0.02 * jax.random.normal(k, shape, f32)

    x = jax.random.normal(ks[0], (BATCH, SEQ, D_MODEL), f32)
    p = jax.random.normal(ks[1], (DEPTH, BATCH, SEQ, PLE_DIM), f32)
    offset = jax.random.randint(ks[2], (BATCH, 1), 0, 4096, dtype=jnp.int32)
    positions = offset + jnp.arange(SEQ, dtype=jnp.int32)[None, :]
    return {
        'x': x,
        'p': p,
        'positions': positions,
        'g_mix': gain(ks[3], (DEPTH, D_MODEL)),
        'w_in_even': dense(ks[4], (N_EVEN, D_MODEL, IN_EVEN), D_MODEL),
        'a_ln_g': gain(ks[5], (N_EVEN, WIDTH_A)),
        'a_ln_b': small(ks[6], (N_EVEN, WIDTH_A)),
        'a_w_s': dense(ks[7], (N_EVEN, N_GROUPS_A, SGU_CHUNK, SGU_CHUNK), SGU_CHUNK),
        'a_b_s': 1.0 + 0.1 * jax.random.normal(ks[8], (N_EVEN, N_GROUPS_A, SGU_CHUNK), f32),
        'w_out_even': dense(ks[9], (N_EVEN, WIDTH_A + WIDTH_B, D_MODEL), WIDTH_A + WIDTH_B),
        'w_in_odd': dense(ks[10], (N_ODD, D_MODEL, IN_ODD), D_MODEL),
        'c_kidx_ln_g': gain(ks[11], (N_ODD, IDX_DIM)),
        'c_kidx_ln_b': small(ks[12], (N_ODD, IDX_DIM)),
        'w_out_odd': dense(ks[13], (N_ODD, WIDTH_C, D_MODEL), WIDTH_C),
        'g_ffn': gain(ks[14], (DEPTH, D_MODEL)),
        'w_ffn_gate': dense(ks[15], (DEPTH, D_MODEL, D_FF), D_MODEL),
        'w_ffn_up': dense(ks[16], (DEPTH, D_MODEL, D_FF), D_MODEL),
        'w_ffn_down': dense(ks[17], (DEPTH, D_FF, D_MODEL), D_FF),
        'g_ple': gain(ks[18], (DEPTH, D_MODEL)),
        'w_ple_proj': dense(ks[19], (DEPTH, PLE_DIM, D_MODEL), PLE_DIM),
        'w_ple_gate': dense(ks[20], (DEPTH, D_MODEL, D_MODEL), D_MODEL),
        'g_final': gain(ks[21], (D_MODEL,)),
    }


def reference(x, p, positions, g_mix, w_in_even, a_ln_g, a_ln_b, a_w_s, a_b_s, w_out_even,
              w_in_odd, c_kidx_ln_g, c_kidx_ln_b, w_out_odd, g_ffn, w_ffn_gate, w_ffn_up,
              w_ffn_down, g_ple, w_ple_proj, w_ple_gate, g_final):
    B, S, _ = x.shape
    cos, sin = rope_tables(positions)
    h = x
    for i in range(DEPTH):
        j = i // 2
        xn = rmsnorm(h, g_mix[i])
        if i % 2 == 0:
            z = xn @ w_in_even[j]
            u_a = jax.nn.gelu(z[..., :WIDTH_A], approximate=False)
            v_a = jax.nn.gelu(z[..., WIDTH_A:2 * WIDTH_A], approximate=False)
            o = 2 * WIDTH_A
            q_b = z[..., o:o + WIDTH_B].reshape(B, S, N_HEADS_B, HEAD_DIM)
            k_b = z[..., o + WIDTH_B:o + 2 * WIDTH_B].reshape(B, S, N_HEADS_B, HEAD_DIM)
            v_b = z[..., o + 2 * WIDTH_B:o + 3 * WIDTH_B].reshape(B, S, N_HEADS_B, HEAD_DIM)
            y_a = spatial_gating(u_a, v_a, a_ln_g[j], a_ln_b[j], a_w_s[j], a_b_s[j])
            y_b = moba_attention(partial_rope(q_b, cos, sin), partial_rope(k_b, cos, sin), v_b)
            mix = jnp.concatenate([y_a, y_b.reshape(B, S, WIDTH_B)], axis=-1) @ w_out_even[j]
        else:
            z = xn @ w_in_odd[j]
            q_c = z[..., :WIDTH_C].reshape(B, S, N_HEADS_C, HEAD_DIM)
            k_c = z[..., WIDTH_C:2 * WIDTH_C].reshape(B, S, N_HEADS_C, HEAD_DIM)
            v_c = z[..., 2 * WIDTH_C:3 * WIDTH_C].reshape(B, S, N_HEADS_C, HEAD_DIM)
            o = 3 * WIDTH_C
            q_i = z[..., o:o + IDX_HEADS * IDX_DIM].reshape(B, S, IDX_HEADS, IDX_DIM)
            o = o + IDX_HEADS * IDX_DIM
            k_i = layernorm(z[..., o:o + IDX_DIM], c_kidx_ln_g[j], c_kidx_ln_b[j])
            w_i = z[..., o + IDX_DIM:o + IDX_DIM + IDX_HEADS] * ((IDX_HEADS ** -0.5) * (IDX_DIM ** -0.5))
            q_i = partial_rope(q_i, cos, sin)
            k_i = partial_rope(k_i[:, :, None, :], cos, sin)[:, :, 0, :]
            y_c = dsa_attention(partial_rope(q_c, cos, sin), partial_rope(k_c, cos, sin), v_c,
                                q_i, k_i, w_i)
            mix = y_c.reshape(B, S, WIDTH_C) @ w_out_odd[j]
        h = h + mix
        xn = rmsnorm(h, g_ffn[i])
        h = h + (jax.nn.silu(xn @ w_ffn_gate[i]) * (xn @ w_ffn_up[i])) @ w_ffn_down[i]
        gate = jax.nn.sigmoid(rmsnorm(h, g_ple[i]) @ w_ple_gate[i])
        h = h + (p[i] @ w_ple_proj[i]) * gate
    return rmsnorm(h, g_final)
```

```python
import functools

import jax
import jax.numpy as jnp
from jax import lax
from jax.experimental import pallas as pl
from jax.experimental.pallas import tpu as pltpu

HEAD_DIM = 64
ROPE_DIM = HEAD_DIM // 4
ROPE_HALF = ROPE_DIM // 2
ROPE_THETA = 500000.0
N_GROUPS_A = 8
WIDTH_A = N_GROUPS_A * HEAD_DIM
N_HEADS_B = 8
WIDTH_B = N_HEADS_B * HEAD_DIM
N_HEADS_C = 16
WIDTH_C = N_HEADS_C * HEAD_DIM
SGU_CHUNK = 128
MOBA_BLOCK = 256
MOBA_TOPK = 3
IDX_HEADS = 8
IDX_DIM = 64
DSA_TOPK = 256
EPS = 1e-6
NEG = -1e30
LOG2E = 1.4426950408889634

LANES = 128
ROW_TILE = 512
DSA_Q_TILE = 128
DSA_K_CHUNK = 512
BISECT_STEPS = 24
VMEM_LIMIT = 56 * 1024 * 1024

F32 = jnp.float32
BF16 = jnp.bfloat16


def _contract_last(a, b):
    return lax.dot_general(a, b, (((1,), (1,)), ((), ())), preferred_element_type=F32)


def _rms(x, g):
    return x * lax.rsqrt(jnp.mean(x * x, axis=-1, keepdims=True) + EPS) * g


def _rope(z, c, sa, sb):
    outs = []
    for j in range(z.shape[1] // LANES):
        zz = z[:, j * LANES:(j + 1) * LANES]
        outs.append(zz * c + pltpu.roll(zz, LANES - ROPE_HALF, 1) * sa + pltpu.roll(zz, ROPE_HALF, 1) * sb)
    return outs[0] if len(outs) == 1 else jnp.concatenate(outs, axis=1)


def _gelu(x):
    return 0.5 * x * (1.0 + lax.erf(x * (2.0 ** -0.5)))


def _const_spec(shape):
    nd = len(shape)
    return pl.BlockSpec(shape, lambda *_: (0,) * nd, pipeline_mode=pl.Buffered(1))


def _params(n_axes):
    return pltpu.CompilerParams(dimension_semantics=("arbitrary",) * n_axes, vmem_limit_bytes=VMEM_LIMIT)


def _front0_kernel(h_ref, g_ref, w_ref, c_ref, sa_ref, sb_ref, lng_ref, lnb_ref, ws_ref, bs_ref,
                   ya_ref, q_ref, k_ref, v_ref, km_ref):
    tm = h_ref.shape[0]
    xn = _rms(h_ref[...], g_ref[...]).astype(BF16)

    def proj(lo, hi):
        return jnp.dot(xn, w_ref[:, lo:hi], preferred_element_type=F32)

    u = _gelu(proj(0, WIDTH_A))
    v = _gelu(proj(WIDTH_A, 2 * WIDTH_A))
    lane = lax.broadcasted_iota(jnp.int32, (1, LANES), 1)
    low = lane < HEAD_DIM
    row = lax.broadcasted_iota(jnp.int32, (SGU_CHUNK, SGU_CHUNK), 0)
    col = lax.broadcasted_iota(jnp.int32, (SGU_CHUNK, SGU_CHUNK), 1)
    tril = col <= row
    for j in range(WIDTH_A // LANES):
        sl = slice(j * LANES, (j + 1) * LANES)
        vv = v[:, sl]

        def half_mean(t):
            s_lo = jnp.sum(jnp.where(low, t, 0.0), axis=-1, keepdims=True)
            s_hi = jnp.sum(jnp.where(low, 0.0, t), axis=-1, keepdims=True)
            return jnp.where(low, s_lo, s_hi) * (1.0 / HEAD_DIM)

        xc = vv - half_mean(vv)
        var = half_mean(xc * xc)
        vn = (xc * lax.rsqrt(var + EPS) * lng_ref[:, sl] + lnb_ref[:, sl]).astype(BF16)
        w_lo = jnp.where(tril, ws_ref[2 * j], 0.0).astype(BF16)
        w_hi = jnp.where(tril, ws_ref[2 * j + 1], 0.0).astype(BF16)
        for c in range(tm // SGU_CHUNK):
            rs = slice(c * SGU_CHUNK, (c + 1) * SGU_CHUNK)
            vc = vn[rs]
            mixed = jnp.where(low, jnp.dot(w_lo, vc, preferred_element_type=F32),
                              jnp.dot(w_hi, vc, preferred_element_type=F32)) + bs_ref[:, sl]
            ya_ref[rs, sl] = (u[rs, sl] * mixed).astype(BF16)

    o = 2 * WIDTH_A
    c, sa, sb = c_ref[...], sa_ref[...], sb_ref[...]
    q = _rope(proj(o, o + WIDTH_B), c, sa, sb)
    q_ref[...] = (q * (HEAD_DIM ** -0.5 * LOG2E)).astype(BF16)
    k = _rope(proj(o + WIDTH_B, o + 2 * WIDTH_B), c, sa, sb)
    k_ref[...] = k.astype(BF16)
    for blk in range(tm // MOBA_BLOCK):
        km_ref[0, blk:blk + 1, :] = jnp.mean(k[blk * MOBA_BLOCK:(blk + 1) * MOBA_BLOCK], axis=0, keepdims=True)
    v_ref[...] = proj(o + 2 * WIDTH_B, o + 3 * WIDTH_B).astype(BF16)


def _front0(h, g, w, c, sa, sb, lng, lnb, ws, bs):
    t, d = h.shape
    tm = ROW_TILE
    row = lambda n: pl.BlockSpec((tm, n), lambda i: (i, 0))
    return pl.pallas_call(
        _front0_kernel,
        grid=(t // tm,),
        in_specs=[row(d), _const_spec((1, d)), _const_spec(w.shape), row(LANES), row(LANES), row(LANES),
                  _const_spec(lng.shape), _const_spec(lnb.shape), _const_spec(ws.shape), _const_spec(bs.shape)],
        out_specs=[row(WIDTH_A), row(WIDTH_B), row(WIDTH_B), row(WIDTH_B),
                   pl.BlockSpec((1, tm // MOBA_BLOCK, WIDTH_B), lambda i: (i, 0, 0))],
        out_shape=[jax.ShapeDtypeStruct((t, WIDTH_A), BF16), jax.ShapeDtypeStruct((t, WIDTH_B), BF16),
                   jax.ShapeDtypeStruct((t, WIDTH_B), BF16), jax.ShapeDtypeStruct((t, WIDTH_B), BF16),
                   jax.ShapeDtypeStruct((t // tm, tm // MOBA_BLOCK, WIDTH_B), F32)],
        compiler_params=_params(1),
        name="front0",
    )(h, g, w, c, sa, sb, lng, lnb, ws, bs)


def _softmax_step(s, m, l, acc, vc):
    m_new = jnp.maximum(m, jnp.max(s, axis=-1, keepdims=True))
    alpha = jnp.exp2(m - m_new)
    p = jnp.exp2(s - m_new)
    l = alpha * l + jnp.sum(p, axis=-1, keepdims=True)
    acc = alpha * acc + jnp.dot(p.astype(BF16), vc, preferred_element_type=F32)
    return m_new, l, acc


def _moba_kernel(q_ref, k_ref, v_ref, km_ref, oh_ref, o_ref):
    i = pl.program_id(2)
    tq = q_ref.shape[0]
    nb = km_ref.shape[0]
    lane = lax.broadcasted_iota(jnp.int32, (1, LANES), 1)
    lane_f = lane.astype(F32)
    q2 = q_ref[...]
    km = jnp.concatenate([km_ref[...].astype(BF16), jnp.zeros((LANES - nb, LANES), BF16)], axis=0)
    causal = (lax.broadcasted_iota(jnp.int32, (tq, tq), 1) <= lax.broadcasted_iota(jnp.int32, (tq, tq), 0))
    past = lane < i
    own = pl.multiple_of(i * tq, tq)
    k_own = k_ref[pl.ds(own, tq), :]
    v_own = v_ref[pl.ds(own, tq), :]
    outs = []
    for hh in range(2):
        head = (lane >= hh * HEAD_DIM) & (lane < (hh + 1) * HEAD_DIM)
        qm = jnp.where(head, q2, jnp.zeros_like(q2))
        g = jnp.where(past, _contract_last(qm, km), NEG)
        sel = jnp.zeros(g.shape, jnp.bool_)
        for _ in range(min(MOBA_TOPK, nb)):
            top = jnp.max(g, axis=-1, keepdims=True)
            idx = jnp.min(jnp.where(g == top, lane_f, 1e9), axis=-1, keepdims=True)
            pick = lane_f == idx
            sel = sel | pick
            g = jnp.where(pick, -jnp.inf, g)
        selb = jnp.where(sel & past, 0.0, NEG).astype(BF16)
        qcat = jnp.concatenate([qm, selb], axis=1)

        s = jnp.where(causal, _contract_last(qm, k_own), NEG)
        m = jnp.max(s, axis=-1, keepdims=True)
        p = jnp.exp2(s - m)
        l = jnp.sum(p, axis=-1, keepdims=True)
        acc = jnp.dot(p.astype(BF16), v_own, preferred_element_type=F32)

        def body(j, carry):
            st = pl.multiple_of(j * tq, tq)
            kc = jnp.concatenate([k_ref[pl.ds(st, tq), :], oh_ref[pl.ds(st, tq), :]], axis=1)
            return _softmax_step(_contract_last(qcat, kc), *carry, v_ref[pl.ds(st, tq), :])

        m, l, acc = lax.fori_loop(0, i, body, (m, l, acc))
        outs.append(acc / l)
    o_ref[...] = jnp.where(lane < HEAD_DIM, outs[0], outs[1]).astype(BF16)


def _moba(q, k, v, km, onehot, batch, seq):
    t = q.shape[0]
    tq = MOBA_BLOCK
    nb = seq // MOBA_BLOCK
    nq = seq // tq
    qspec = pl.BlockSpec((tq, LANES), lambda b, p, i: (b * nq + i, p))
    kspec = pl.BlockSpec((seq, LANES), lambda b, p, i: (b, p))
    return pl.pallas_call(
        _moba_kernel,
        grid=(batch, WIDTH_B // LANES, nq),
        in_specs=[qspec, kspec, kspec, pl.BlockSpec((nb, LANES), lambda b, p, i: (b, p)),
                  _const_spec(onehot.shape)],
        out_specs=qspec,
        out_shape=jax.ShapeDtypeStruct((t, WIDTH_B), BF16),
        compiler_params=_params(3),
        name="moba",
    )(q, k, v, km, onehot)


def _post_kernel(*refs, n_mix, final):
    h_ref = refs[0]
    y_refs = refs[1:1 + n_mix]
    wo_refs = refs[1 + n_mix:1 + 2 * n_mix]
    (p_ref, gf_ref, wg_ref, wu_ref, wd_ref, gp_ref, wpg_ref, wpp_ref, gl_ref, o_ref, a_ref) = refs[1 + 2 * n_mix:]
    h = h_ref[...]
    for y_ref, wo_ref in zip(y_refs, wo_refs):
        h = h + jnp.dot(y_ref[...], wo_ref[...], preferred_element_type=F32)
    xn = _rms(h, gf_ref[...]).astype(BF16)
    d_ff = wg_ref.shape[1]
    step = 2 * LANES
    for lo in range(0, d_ff, step):
        gate = jnp.dot(xn, wg_ref[:, lo:lo + step], preferred_element_type=F32)
        up = jnp.dot(xn, wu_ref[:, lo:lo + step], preferred_element_type=F32)
        a_ref[:, lo:lo + step] = (gate * jax.nn.sigmoid(gate) * up).astype(BF16)
    h = h + jnp.dot(a_ref[...], wd_ref[...], preferred_element_type=F32)
    pg = jax.nn.sigmoid(jnp.dot(_rms(h, gp_ref[...]).astype(BF16), wpg_ref[...], preferred_element_type=F32))
    h = h + jnp.dot(p_ref[...].astype(BF16), wpp_ref[...], preferred_element_type=F32) * pg
    o_ref[...] = _rms(h, gl_ref[...]) if final else h


def _post(h, ys, wos, p, gf, wg, wu, wd, gp, wpg, wpp, gl, final):
    t, d = h.shape
    tm = ROW_TILE
    row = lambda n: pl.BlockSpec((tm, n), lambda i: (i, 0))
    consts = [gf, wg, wu, wd, gp, wpg, wpp, gl]
    return pl.pallas_call(
        functools.partial(_post_kernel, n_mix=len(ys), final=final),
        grid=(t // tm,),
        in_specs=[row(d)] + [row(y.shape[1]) for y in ys] + [_const_spec(w.shape) for w in wos]
                 + [row(p.shape[1])] + [_const_spec(c.shape) for c in consts],
        out_specs=row(d),
        out_shape=jax.ShapeDtypeStruct((t, d), F32),
        scratch_shapes=[pltpu.VMEM((tm, wg.shape[1]), BF16)],
        compiler_params=_params(1),
        name="post_final" if final else "post",
    )(h, *ys, *wos, p, *consts)


def _front1_kernel(h_ref, g_ref, w_ref, c_ref, sa_ref, sb_ref, lng_ref, lnb_ref,
                   q_ref, k_ref, v_ref, qi_ref, ki_ref, wi_ref):
    xn = _rms(h_ref[...], g_ref[...]).astype(BF16)

    def proj(lo, hi):
        return jnp.dot(xn, w_ref[:, lo:hi], preferred_element_type=F32)

    c, sa, sb = c_ref[...], sa_ref[...], sb_ref[...]
    q_ref[...] = (_rope(proj(0, WIDTH_C), c, sa, sb) * (HEAD_DIM ** -0.5 * LOG2E)).astype(BF16)
    k_ref[...] = _rope(proj(WIDTH_C, 2 * WIDTH_C), c, sa, sb).astype(BF16)
    v_ref[...] = proj(2 * WIDTH_C, 3 * WIDTH_C).astype(BF16)
    o = 3 * WIDTH_C
    qi_ref[...] = _rope(proj(o, o + IDX_HEADS * IDX_DIM), c, sa, sb).astype(BF16)
    o += IDX_HEADS * IDX_DIM
    ki = proj(o, o + LANES)
    xc = ki - jnp.mean(ki, axis=-1, keepdims=True)
    var = jnp.mean(xc * xc, axis=-1, keepdims=True)
    ki = xc * lax.rsqrt(var + EPS) * lng_ref[...] + lnb_ref[...]
    ki_ref[...] = _rope(ki, c, sa, sb).astype(BF16)
    wi_ref[...] = proj(o + LANES, o + 2 * LANES) * ((IDX_HEADS ** -0.5) * (IDX_DIM ** -0.5))


def _front1(h, g, w, c, sa, sb, lng, lnb):
    t, d = h.shape
    tm = ROW_TILE
    row = lambda n: pl.BlockSpec((tm, n), lambda i: (i, 0))
    widths = [WIDTH_C, WIDTH_C, WIDTH_C, IDX_HEADS * IDX_DIM, LANES, LANES]
    dtypes = [BF16, BF16, BF16, BF16, BF16, F32]
    return pl.pallas_call(
        _front1_kernel,
        grid=(t // tm,),
        in_specs=[row(d), _const_spec((1, d)), _const_spec(w.shape), row(LANES), row(LANES), row(LANES),
                  _const_spec(lng.shape), _const_spec(lnb.shape)],
        out_specs=[row(n) for n in widths],
        out_shape=[jax.ShapeDtypeStruct((t, n), dt) for n, dt in zip(widths, dtypes)],
        compiler_params=_params(1),
        name="front1",
    )(h, g, w, c, sa, sb, lng, lnb)


def _dsa_index_kernel(qi_ref, wi_ref, ki_ref, b_ref, sc_ref, *, n_sel):
    i = pl.program_id(1)
    tq = qi_ref.shape[0]
    kc_n = DSA_K_CHUNK
    n_chunks_total = sc_ref.shape[0]
    n_chunks = (i * tq + tq + kc_n - 1) // kc_n
    lane = lax.broadcasted_iota(jnp.int32, (1, LANES), 1)
    qpos = i * tq + lax.broadcasted_iota(jnp.int32, (tq, 1), 0)
    kofs = lax.broadcasted_iota(jnp.int32, (1, kc_n), 1)
    wv = wi_ref[...]
    qms = []
    for h in range(IDX_HEADS):
        q2 = qi_ref[:, (h // 2) * LANES:(h // 2 + 1) * LANES]
        head = (lane >= (h % 2) * IDX_DIM) & (lane < (h % 2 + 1) * IDX_DIM)
        qms.append(jnp.where(head, q2, jnp.zeros_like(q2)))

    def causal(c):
        return (c * kc_n + kofs) <= qpos

    def score_chunk(c, carry):
        kc = ki_ref[pl.ds(pl.multiple_of(c * kc_n, kc_n), kc_n), :]
        acc = jnp.zeros((tq, kc_n), F32)
        for h in range(IDX_HEADS):
            acc = acc + jnp.maximum(_contract_last(qms[h], kc), 0.0) * wv[:, h:h + 1]
        sc_ref[c] = jnp.where(causal(c), acc, NEG)
        return carry

    lax.fori_loop(0, n_chunks, score_chunk, 0)

    def lane_fold(x, op):
        out = x[:, :LANES]
        for a in range(1, kc_n // LANES):
            out = op(out, x[:, a * LANES:(a + 1) * LANES])
        return out

    def reduce_chunks(fn, op, init):
        def body(c, acc):
            return op(acc, lane_fold(fn(sc_ref[c]), op))
        return lax.fori_loop(0, n_chunks, body, jnp.full((tq, LANES), init, F32))

    def count_ge(thr):
        acc = reduce_chunks(lambda x: jnp.where(x >= thr, 1.0, 0.0), jnp.add, 0.0)
        return jnp.sum(acc, axis=-1, keepdims=True)

    k_f = float(n_sel)
    big = 3e38
    row_max = jnp.max(reduce_chunks(lambda x: x, jnp.maximum, -big), axis=-1, keepdims=True)
    row_min = jnp.min(reduce_chunks(lambda x: jnp.where(x > 0.5 * NEG, x, big), jnp.minimum, big),
                      axis=-1, keepdims=True)
    lo = jnp.where(qpos + 1 >= n_sel, row_min, NEG)
    hi = row_max + jnp.maximum(jnp.maximum(row_max - row_min, jnp.abs(row_max) * 1e-3), 1e-30)

    def bisect(_, carry):
        lo, hi = carry
        mid = 0.5 * (lo + hi)
        ok = count_ge(mid) >= k_f
        return jnp.where(ok, mid, lo), jnp.where(ok, hi, mid)

    lo, hi = lax.fori_loop(0, BISECT_STEPS, bisect, (lo, hi))

    def snap_cond(carry):
        return carry[3] > 0.5

    def snap(carry):
        hi, thr, done, _ = carry
        cand = jnp.max(reduce_chunks(lambda x: jnp.where(x < hi, x, -big), jnp.maximum, -big),
                       axis=-1, keepdims=True)
        ok = count_ge(cand) >= k_f
        thr = jnp.where(done > 0.5, thr, cand)
        done = jnp.where(ok, 1.0, done)
        hi = jnp.where(done > 0.5, hi, cand)
        return hi, thr, done, jnp.max(1.0 - done)

    zeros = jnp.zeros((tq, 1), F32)
    _, thr, _, _ = lax.while_loop(snap_cond, snap, (hi, zeros, zeros, jnp.float32(1.0)))

    n_gt = reduce_chunks(lambda x: jnp.where(x > thr, 1.0, 0.0), jnp.add, 0.0)
    need = k_f - jnp.sum(n_gt, axis=-1, keepdims=True)
    n_ge = count_ge(thr)
    tied = jnp.max(jnp.where((n_ge > k_f) & (thr > 0.5 * NEG), 1.0, 0.0))

    @pl.when(tied < 0.5)
    def _():
        def body(c, carry):
            x = sc_ref[c]
            b_ref[0, 0, c] = jnp.where((x >= thr) & causal(c), 0.0, NEG)
            return carry
        lax.fori_loop(0, n_chunks, body, 0)

    @pl.when(tied > 0.5)
    def _():
        r = lax.broadcasted_iota(jnp.int32, (LANES, 2 * LANES), 0)
        cc = lax.broadcasted_iota(jnp.int32, (LANES, 2 * LANES), 1)
        prefix_and_total = jnp.where((cc >= LANES) | (r <= cc), 1.0, 0.0).astype(BF16)

        def body(c, seen):
            x = sc_ref[c]
            eq = x == thr
            cols = []
            for a in range(kc_n // LANES):
                sl = slice(a * LANES, (a + 1) * LANES)
                pt = jnp.dot(jnp.where(eq[:, sl], 1.0, 0.0).astype(BF16), prefix_and_total,
                             preferred_element_type=F32)
                keep = eq[:, sl] & (seen + pt[:, :LANES] <= need)
                cols.append(jnp.where(((x[:, sl] > thr) | keep) & causal(c)[:, sl], 0.0, NEG))
                seen = seen + pt[:, LANES:]
            b_ref[0, 0, c] = jnp.concatenate(cols, axis=1)
            return seen
        lax.fori_loop(0, n_chunks, body, jnp.zeros((tq, LANES), F32))

    def fill(c, carry):
        b_ref[0, 0, c] = jnp.full((tq, kc_n), NEG, F32)
        return carry

    lax.fori_loop(n_chunks, n_chunks_total, fill, 0)


def _dsa_index(qi, wi, ki, batch, seq):
    tq = DSA_Q_TILE
    nq = seq // tq
    nch = seq // DSA_K_CHUNK
    n_sel = min(DSA_TOPK, seq // 4)
    return pl.pallas_call(
        functools.partial(_dsa_index_kernel, n_sel=n_sel),
        grid=(batch, nq),
        in_specs=[pl.BlockSpec((tq, qi.shape[1]), lambda b, i: (b * nq + i, 0)),
                  pl.BlockSpec((tq, LANES), lambda b, i: (b * nq + i, 0)),
                  pl.BlockSpec((seq, LANES), lambda b, i: (b, 0))],
        out_specs=pl.BlockSpec((1, 1, nch, tq, DSA_K_CHUNK), lambda b, i: (b, i, 0, 0, 0)),
        out_shape=jax.ShapeDtypeStruct((batch, nq, nch, tq, DSA_K_CHUNK), F32),
        scratch_shapes=[pltpu.VMEM((nch, tq, DSA_K_CHUNK), F32)],
        compiler_params=_params(2),
        name="dsa_index",
    )(qi, wi, ki)


def _dsa_attn_kernel(q_ref, k_ref, v_ref, b_ref, o_ref):
    i = pl.program_id(1)
    tq = q_ref.shape[0]
    kc_n = DSA_K_CHUNK
    n_chunks = (i * tq + tq + kc_n - 1) // kc_n
    lane = lax.broadcasted_iota(jnp.int32, (1, LANES), 1)
    for p in range(q_ref.shape[1] // LANES):
        sl = slice(p * LANES, (p + 1) * LANES)
        q2 = q_ref[:, sl]
        qm = [jnp.where((lane >= hh * HEAD_DIM) & (lane < (hh + 1) * HEAD_DIM), q2, jnp.zeros_like(q2))
              for hh in range(2)]

        def body(c, carry, sl=sl, qm=qm):
            st = pl.multiple_of(c * kc_n, kc_n)
            kc = k_ref[pl.ds(st, kc_n), sl]
            vc = v_ref[pl.ds(st, kc_n), sl]
            bias = b_ref[0, 0, c]
            out = []
            for hh in range(2):
                out.extend(_softmax_step(_contract_last(qm[hh], kc) + bias, *carry[3 * hh:3 * hh + 3], vc))
            return tuple(out)

        init = (jnp.full((tq, 1), -jnp.inf, F32), jnp.zeros((tq, 1), F32), jnp.zeros((tq, LANES), F32)) * 2
        m0, l0, a0, m1, l1, a1 = lax.fori_loop(0, n_chunks, body, init)
        o_ref[:, sl] = jnp.where(lane < HEAD_DIM, a0 / l0, a1 / l1).astype(BF16)


def _dsa_attn(q, k, v, bias, batch, seq):
    t, width = q.shape
    tq = DSA_Q_TILE
    nq = seq // tq
    qspec = pl.BlockSpec((tq, width), lambda b, i: (b * nq + i, 0))
    kspec = pl.BlockSpec((seq, width), lambda b, i: (b, 0), pipeline_mode=pl.Buffered(1))
    return pl.pallas_call(
        _dsa_attn_kernel,
        grid=(batch, nq),
        in_specs=[qspec, kspec, kspec,
                  pl.BlockSpec((1, 1) + bias.shape[2:], lambda b, i: (b, i, 0, 0, 0))],
        out_specs=qspec,
        out_shape=jax.ShapeDtypeStruct((t, width), BF16),
        compiler_params=_params(2),
        name="dsa_attn",
    )(q, k, v, bias)


def _rope_tables(positions):
    inv_freq = ROPE_THETA ** (-jnp.arange(0, ROPE_DIM, 2, dtype=F32) / ROPE_DIM)
    ang = positions.astype(F32).reshape(-1, 1) * inv_freq
    cos, sin = jnp.cos(ang), jnp.sin(ang)
    t = cos.shape[0]
    rest = HEAD_DIM - ROPE_DIM
    c = jnp.concatenate([cos, cos, jnp.ones((t, rest), F32)], axis=1)
    sa = jnp.concatenate([-sin, jnp.zeros((t, HEAD_DIM - ROPE_HALF), F32)], axis=1)
    sb = jnp.concatenate([jnp.zeros((t, ROPE_HALF), F32), sin, jnp.zeros((t, rest), F32)], axis=1)
    return tuple(jnp.tile(x, (1, LANES // HEAD_DIM)) for x in (c, sa, sb))


def kernel(x, p, positions, g_mix, w_in_even, a_ln_g, a_ln_b, a_w_s, a_b_s, w_out_even, w_in_odd, c_kidx_ln_g, c_kidx_ln_b, w_out_odd, g_ffn, w_ffn_gate, w_ffn_up, w_ffn_down, g_ple, w_ple_proj, w_ple_gate, g_final):
    batch, seq, d = x.shape
    depth = p.shape[0]
    t = batch * seq
    assert seq % ROW_TILE == 0 and seq % DSA_K_CHUNK == 0 and min(DSA_TOPK, seq // 4) <= DSA_K_CHUNK
    c, sa, sb = _rope_tables(positions)
    h = x.reshape(t, d)
    p2 = p.reshape(depth, t, p.shape[-1])
    row = lambda a: a.reshape(1, -1)
    onehot = (jnp.arange(seq)[:, None] // MOBA_BLOCK == jnp.arange(LANES)[None, :]).astype(BF16)
    for i in range(depth):
        j = i // 2
        if i % 2 == 0:
            bs = jnp.repeat(a_b_s[j].T, HEAD_DIM, axis=1)
            ya, q, k, v, km = _front0(h, row(g_mix[i]), w_in_even[j].astype(BF16), c, sa, sb,
                                      row(a_ln_g[j]), row(a_ln_b[j]), a_w_s[j], bs)
            yb = _moba(q, k, v, km.reshape(t // MOBA_BLOCK, WIDTH_B), onehot, batch, seq)
            wo = w_out_even[j].astype(BF16)
            ys, wos = [ya, yb], [wo[:WIDTH_A], wo[WIDTH_A:]]
        else:
            w = w_in_odd[j]
            o = 3 * WIDTH_C + IDX_HEADS * IDX_DIM
            w1 = jnp.concatenate([w[:, :o], w[:, o:o + IDX_DIM], w[:, o:o + IDX_DIM], w[:, o + IDX_DIM:],
                                  jnp.zeros((d, LANES - IDX_HEADS), w.dtype)], axis=1).astype(BF16)
            q, k, v, qi, ki, wi = _front1(h, row(g_mix[i]), w1, c, sa, sb,
                                          row(jnp.tile(c_kidx_ln_g[j], 2)), row(jnp.tile(c_kidx_ln_b[j], 2)))
            bias = _dsa_index(qi, wi, ki, batch, seq)
            ys, wos = [_dsa_attn(q, k, v, bias, batch, seq)], [w_out_odd[j].astype(BF16)]
        h = _post(h, ys, wos, p2[i], row(g_ffn[i]), w_ffn_gate[i].astype(BF16), w_ffn_up[i].astype(BF16),
                  w_ffn_down[i].astype(BF16), row(g_ple[i]), w_ple_gate[i].astype(BF16),
                  w_ple_proj[i].astype(BF16), row(g_final), final=(i == depth - 1))
    return h.reshape(batch, seq, d)
```

```python
import functools

import jax
import jax.numpy as jnp
from jax import lax
from jax.experimental import pallas as pl
from jax.experimental.pallas import tpu as pltpu

HEAD_DIM = 64
ROPE_DIM = HEAD_DIM // 4
ROPE_HALF = ROPE_DIM // 2
ROPE_THETA = 500000.0
N_GROUPS_A = 8
WIDTH_A = N_GROUPS_A * HEAD_DIM
N_HEADS_B = 8
WIDTH_B = N_HEADS_B * HEAD_DIM
N_HEADS_C = 16
WIDTH_C = N_HEADS_C * HEAD_DIM
SGU_CHUNK = 128
MOBA_BLOCK = 256
MOBA_TOPK = 3
IDX_HEADS = 8
IDX_DIM = 64
DSA_TOPK = 256
EPS = 1e-6
NEG = -1e30
LOG2E = 1.4426950408889634

LANES = 128
SUBLANES = 8
ROW_TILE = 512
Q_TILE = 256
DSA_K_CHUNK = 512
SCORE_SLOTS = 4
BISECT_STEPS = 24
VMEM_LIMIT = 56 * 1024 * 1024

F32 = jnp.float32
BF16 = jnp.bfloat16


def _dot(a, b):
    return jnp.dot(a, b, preferred_element_type=F32)


def _rms(x, g):
    return x * lax.rsqrt(jnp.mean(x * x, axis=-1, keepdims=True) + EPS) * g


def _rope(z, c, sa, sb):
    outs = []
    for j in range(z.shape[1] // LANES):
        zz = z[:, j * LANES:(j + 1) * LANES]
        outs.append(zz * c + pltpu.roll(zz, LANES - ROPE_HALF, 1) * sa + pltpu.roll(zz, ROPE_HALF, 1) * sb)
    return outs[0] if len(outs) == 1 else jnp.concatenate(outs, axis=1)


def _gelu(x):
    return 0.5 * x * (1.0 + lax.erf(x * (2.0 ** -0.5)))


def _fold_rows(x, op):
    parts = [x[a:a + SUBLANES] for a in range(0, x.shape[0], SUBLANES)]
    while len(parts) > 1:
        parts = [op(parts[a], parts[a + 1]) if a + 1 < len(parts) else parts[a] for a in range(0, len(parts), 2)]
    return parts[0]


def _head_rows(x, hh):
    r = lax.broadcasted_iota(jnp.int32, (LANES, 1), 0)
    return jnp.where((r >= hh * HEAD_DIM) & (r < (hh + 1) * HEAD_DIM), x, jnp.zeros_like(x))


def _const_spec(shape):
    nd = len(shape)
    return pl.BlockSpec(shape, lambda *_: (0,) * nd, pipeline_mode=pl.Buffered(1))


def _params(n_axes, flags=None):
    return pltpu.CompilerParams(dimension_semantics=("arbitrary",) * n_axes, vmem_limit_bytes=VMEM_LIMIT, flags=flags)


def _attend_heads(n_heads, score_fn, vt_fn, s_ref, cm_ref, m_ref, l_ref, acc_ref):
    n_slots = s_ref.shape[0]

    def stage(h):
        s = score_fn(h)
        s_ref[h % n_slots] = s
        cm_ref[h % n_slots] = _fold_rows(s, jnp.maximum)

    for h in range(min(n_slots - 1, n_heads)):
        stage(h)
    for h in range(n_heads):
        if h + n_slots - 1 < n_heads:
            stage(h + n_slots - 1)
        slot = h % n_slots
        m_old = m_ref[h]
        m_new = jnp.maximum(m_old, jnp.max(cm_ref[slot], axis=0, keepdims=True))
        alpha = jnp.exp2(m_old - m_new)
        p = jnp.exp2(s_ref[slot] - m_new)
        m_ref[h] = m_new
        l_ref[h] = alpha * l_ref[h] + jnp.sum(_fold_rows(p, jnp.add), axis=0, keepdims=True)
        acc_ref[h] = alpha * acc_ref[h] + _dot(vt_fn(h), p.astype(BF16))


def _softmax_init(m_ref, l_ref, acc_ref):
    m_ref[...] = jnp.full(m_ref.shape, -jnp.inf, F32)
    l_ref[...] = jnp.zeros(l_ref.shape, F32)
    acc_ref[...] = jnp.zeros(acc_ref.shape, F32)


def _write_heads(o_ref, l_ref, acc_ref):
    for p in range(acc_ref.shape[0] // 2):
        pair = jnp.concatenate([acc_ref[2 * p] / l_ref[2 * p], acc_ref[2 * p + 1] / l_ref[2 * p + 1]], axis=0)
        o_ref[:, p * LANES:(p + 1) * LANES] = pair.T.astype(BF16)


def _front0_kernel(h_ref, g_ref, w_ref, c_ref, sa_ref, sb_ref, lng_ref, lnb_ref, ws_ref, bs_ref,
                   ya_ref, qt_ref, k_ref, vt_ref, km_ref):
    tm = h_ref.shape[0]
    xn = _rms(h_ref[...], g_ref[...]).astype(BF16)

    def proj(lo, hi):
        return _dot(xn, w_ref[:, lo:hi])

    u = _gelu(proj(0, WIDTH_A))
    v = _gelu(proj(WIDTH_A, 2 * WIDTH_A))
    lane = lax.broadcasted_iota(jnp.int32, (1, LANES), 1)
    low = lane < HEAD_DIM
    row = lax.broadcasted_iota(jnp.int32, (SGU_CHUNK, SGU_CHUNK), 0)
    col = lax.broadcasted_iota(jnp.int32, (SGU_CHUNK, SGU_CHUNK), 1)
    tril = col <= row
    for j in range(WIDTH_A // LANES):
        sl = slice(j * LANES, (j + 1) * LANES)
        vv = v[:, sl]

        def half_mean(t):
            s_lo = jnp.sum(jnp.where(low, t, 0.0), axis=-1, keepdims=True)
            s_hi = jnp.sum(jnp.where(low, 0.0, t), axis=-1, keepdims=True)
            return jnp.where(low, s_lo, s_hi) * (1.0 / HEAD_DIM)

        xc = vv - half_mean(vv)
        var = half_mean(xc * xc)
        vn = (xc * lax.rsqrt(var + EPS) * lng_ref[:, sl] + lnb_ref[:, sl]).astype(BF16)
        w_lo = jnp.where(tril, ws_ref[2 * j], 0.0).astype(BF16)
        w_hi = jnp.where(tril, ws_ref[2 * j + 1], 0.0).astype(BF16)
        for c in range(tm // SGU_CHUNK):
            rs = slice(c * SGU_CHUNK, (c + 1) * SGU_CHUNK)
            vc = vn[rs]
            mixed = jnp.where(low, _dot(w_lo, vc), _dot(w_hi, vc)) + bs_ref[:, sl]
            ya_ref[rs, sl] = (u[rs, sl] * mixed).astype(BF16)

    o = 2 * WIDTH_A
    c, sa, sb = c_ref[...], sa_ref[...], sb_ref[...]
    q = _rope(proj(o, o + WIDTH_B), c, sa, sb) * (HEAD_DIM ** -0.5 * LOG2E)
    qt_ref[0] = q.T.astype(BF16)
    k = _rope(proj(o + WIDTH_B, o + 2 * WIDTH_B), c, sa, sb)
    k_ref[...] = k.astype(BF16)
    v_b = proj(o + 2 * WIDTH_B, o + 3 * WIDTH_B)
    for blk in range(tm // MOBA_BLOCK):
        rs = slice(blk * MOBA_BLOCK, (blk + 1) * MOBA_BLOCK)
        km_ref[0, blk:blk + 1, :] = jnp.mean(k[rs], axis=0, keepdims=True)
        vt_ref[0, blk] = v_b[rs].T.astype(BF16)


def _front0(h, g, w, c, sa, sb, lng, lnb, ws, bs, batch, seq):
    t, d = h.shape
    tm = ROW_TILE
    nt = seq // tm
    nblk = tm // MOBA_BLOCK
    row = lambda n: pl.BlockSpec((tm, n), lambda i: (i, 0))
    return pl.pallas_call(
        _front0_kernel,
        grid=(t // tm,),
        in_specs=[row(d), _const_spec((1, d)), _const_spec(w.shape), row(LANES), row(LANES), row(LANES),
                  _const_spec(lng.shape), _const_spec(lnb.shape), _const_spec(ws.shape), _const_spec(bs.shape)],
        out_specs=[row(WIDTH_A),
                   pl.BlockSpec((1, WIDTH_B, tm), lambda i: (i // nt, 0, i % nt)),
                   row(WIDTH_B),
                   pl.BlockSpec((1, nblk, WIDTH_B, MOBA_BLOCK), lambda i: (i // nt, i % nt, 0, 0)),
                   pl.BlockSpec((1, nblk, WIDTH_B), lambda i: (i, 0, 0))],
        out_shape=[jax.ShapeDtypeStruct((t, WIDTH_A), BF16),
                   jax.ShapeDtypeStruct((batch, WIDTH_B, seq), BF16),
                   jax.ShapeDtypeStruct((t, WIDTH_B), BF16),
                   jax.ShapeDtypeStruct((batch, seq // MOBA_BLOCK, WIDTH_B, MOBA_BLOCK), BF16),
                   jax.ShapeDtypeStruct((t // tm, nblk, WIDTH_B), F32)],
        compiler_params=_params(1),
        name="front0",
    )(h, g, w, c, sa, sb, lng, lnb, ws, bs)


def _moba_kernel(qt_ref, k_ref, vt_ref, km_ref, oh_ref, o_ref, qc_ref, s_ref, cm_ref, m_ref, l_ref, acc_ref):
    i = pl.program_id(1)
    tq = qt_ref.shape[2]
    nb = km_ref.shape[0]
    n_heads = acc_ref.shape[0]
    blk_f = lax.broadcasted_iota(jnp.int32, (nb, 1), 0).astype(F32)
    past = lax.broadcasted_iota(jnp.int32, (nb, 1), 0) < i
    causal = (lax.broadcasted_iota(jnp.int32, (tq, tq), 0) <= lax.broadcasted_iota(jnp.int32, (tq, tq), 1))
    own = pl.multiple_of(i * tq, tq)
    _softmax_init(m_ref, l_ref, acc_ref)
    for h in range(n_heads):
        sl = slice((h // 2) * LANES, (h // 2 + 1) * LANES)
        qm = _head_rows(qt_ref[0, sl, :], h % 2)
        g = jnp.where(past, _dot(km_ref[:, sl].astype(BF16), qm), NEG)
        sel = jnp.zeros(g.shape, jnp.bool_)
        for _ in range(min(MOBA_TOPK, nb)):
            top = jnp.max(g, axis=0, keepdims=True)
            idx = jnp.min(jnp.where(g == top, blk_f, 1e9), axis=0, keepdims=True)
            pick = blk_f == idx
            sel = sel | pick
            g = jnp.where(pick, -jnp.inf, g)
        selb = jnp.where(sel & past, 0.0, NEG).astype(BF16)
        qc_ref[h] = jnp.concatenate([qm, selb, jnp.zeros((LANES - nb, tq), BF16)], axis=0)

    def pair_lanes(h):
        return slice((h // 2) * LANES, (h // 2 + 1) * LANES)

    def values(j):
        return lambda h: vt_ref[0, j, h * HEAD_DIM:(h + 1) * HEAD_DIM, :]

    def own_scores(h):
        return jnp.where(causal, _dot(k_ref[pl.ds(own, tq), pair_lanes(h)], qc_ref[h, :LANES, :]), NEG)

    refs = (s_ref, cm_ref, m_ref, l_ref, acc_ref)
    _attend_heads(n_heads, own_scores, values(i), *refs)

    def body(j, carry):
        st = pl.multiple_of(j * tq, tq)

        def past_scores(h):
            kc = jnp.concatenate([k_ref[pl.ds(st, tq), pair_lanes(h)], oh_ref[pl.ds(st, tq), :]], axis=1)
            return _dot(kc, qc_ref[h])

        _attend_heads(n_heads, past_scores, values(j), *refs)
        return carry

    lax.fori_loop(0, i, body, 0)
    _write_heads(o_ref, l_ref, acc_ref)


def _moba(qt, k, vt, km, onehot, batch, seq):
    t, width = k.shape
    tq = Q_TILE
    assert tq == MOBA_BLOCK
    nb = seq // MOBA_BLOCK
    nq = seq // tq
    n_heads = width // HEAD_DIM
    one = pl.Buffered(1)
    return pl.pallas_call(
        _moba_kernel,
        grid=(batch, nq),
        in_specs=[pl.BlockSpec((1, width, tq), lambda b, i: (b, 0, i)),
                  pl.BlockSpec((seq, width), lambda b, i: (b, 0), pipeline_mode=one),
                  pl.BlockSpec((1, nb, width, MOBA_BLOCK), lambda b, i: (b, 0, 0, 0), pipeline_mode=one),
                  pl.BlockSpec((nb, width), lambda b, i: (b, 0)),
                  _const_spec(onehot.shape)],
        out_specs=pl.BlockSpec((tq, width), lambda b, i: (b * nq + i, 0)),
        out_shape=jax.ShapeDtypeStruct((t, width), BF16),
        scratch_shapes=[pltpu.VMEM((n_heads, 2 * LANES, tq), BF16), pltpu.VMEM((SCORE_SLOTS, MOBA_BLOCK, tq), F32),
                        pltpu.VMEM((SCORE_SLOTS, SUBLANES, tq), F32),
                        pltpu.VMEM((n_heads, 1, tq), F32), pltpu.VMEM((n_heads, 1, tq), F32),
                        pltpu.VMEM((n_heads, HEAD_DIM, tq), F32)],
        compiler_params=_params(2),
        name="moba",
    )(qt, k, vt, km, onehot)


def _post_kernel(*refs, n_mix, final):
    h_ref = refs[0]
    y_refs = refs[1:1 + n_mix]
    wo_refs = refs[1 + n_mix:1 + 2 * n_mix]
    (p_ref, gf_ref, wg_ref, wu_ref, wd_ref, gp_ref, wpg_ref, wpp_ref, gl_ref, o_ref, a_ref) = refs[1 + 2 * n_mix:]
    h = h_ref[...]
    for y_ref, wo_ref in zip(y_refs, wo_refs):
        h = h + _dot(y_ref[...], wo_ref[...])
    xn = _rms(h, gf_ref[...]).astype(BF16)
    d_ff = wg_ref.shape[1]
    step = 2 * LANES
    for lo in range(0, d_ff, step):
        gate = _dot(xn, wg_ref[:, lo:lo + step])
        up = _dot(xn, wu_ref[:, lo:lo + step])
        a_ref[:, lo:lo + step] = (gate * jax.nn.sigmoid(gate) * up).astype(BF16)
    h = h + _dot(a_ref[...], wd_ref[...])
    pg = jax.nn.sigmoid(_dot(_rms(h, gp_ref[...]).astype(BF16), wpg_ref[...]))
    h = h + _dot(p_ref[...].astype(BF16), wpp_ref[...]) * pg
    o_ref[...] = _rms(h, gl_ref[...]) if final else h


def _post(h, ys, wos, p, gf, wg, wu, wd, gp, wpg, wpp, gl, final):
    t, d = h.shape
    tm = ROW_TILE
    row = lambda n: pl.BlockSpec((tm, n), lambda i: (i, 0))
    consts = [gf, wg, wu, wd, gp, wpg, wpp, gl]
    return pl.pallas_call(
        functools.partial(_post_kernel, n_mix=len(ys), final=final),
        grid=(t // tm,),
        in_specs=[row(d)] + [row(y.shape[1]) for y in ys] + [_const_spec(w.shape) for w in wos]
                 + [row(p.shape[1])] + [_const_spec(c.shape) for c in consts],
        out_specs=row(d),
        out_shape=jax.ShapeDtypeStruct((t, d), F32),
        scratch_shapes=[pltpu.VMEM((tm, wg.shape[1]), BF16)],
        compiler_params=_params(1),
        name="post_final" if final else "post",
    )(h, *ys, *wos, p, *consts)


def _front1_kernel(h_ref, g_ref, w_ref, c_ref, sa_ref, sb_ref, lng_ref, lnb_ref,
                   qt_ref, k_ref, vt_ref, qit_ref, ki_ref, wit_ref):
    xn = _rms(h_ref[...], g_ref[...]).astype(BF16)

    def proj(lo, hi):
        return _dot(xn, w_ref[:, lo:hi])

    c, sa, sb = c_ref[...], sa_ref[...], sb_ref[...]
    qt_ref[0] = (_rope(proj(0, WIDTH_C), c, sa, sb) * (HEAD_DIM ** -0.5 * LOG2E)).T.astype(BF16)
    k_ref[...] = _rope(proj(WIDTH_C, 2 * WIDTH_C), c, sa, sb).astype(BF16)
    vt_ref[0, 0] = proj(2 * WIDTH_C, 3 * WIDTH_C).T.astype(BF16)
    o = 3 * WIDTH_C
    qit_ref[0] = _rope(proj(o, o + IDX_HEADS * IDX_DIM), c, sa, sb).T.astype(BF16)
    o += IDX_HEADS * IDX_DIM
    ki = proj(o, o + LANES)
    xc = ki - jnp.mean(ki, axis=-1, keepdims=True)
    var = jnp.mean(xc * xc, axis=-1, keepdims=True)
    ki = xc * lax.rsqrt(var + EPS) * lng_ref[...] + lnb_ref[...]
    ki_ref[...] = _rope(ki, c, sa, sb).astype(BF16)
    wi = proj(o + LANES, o + 2 * LANES) * ((IDX_HEADS ** -0.5) * (IDX_DIM ** -0.5))
    wit_ref[0] = wi.T[:IDX_HEADS]


def _front1(h, g, w, c, sa, sb, lng, lnb, batch, seq):
    t, d = h.shape
    tm = ROW_TILE
    assert tm == DSA_K_CHUNK
    nt = seq // tm
    row = lambda n: pl.BlockSpec((tm, n), lambda i: (i, 0))
    tr = lambda n: pl.BlockSpec((1, n, tm), lambda i: (i // nt, 0, i % nt))
    return pl.pallas_call(
        _front1_kernel,
        grid=(t // tm,),
        in_specs=[row(d), _const_spec((1, d)), _const_spec(w.shape), row(LANES), row(LANES), row(LANES),
                  _const_spec(lng.shape), _const_spec(lnb.shape)],
        out_specs=[tr(WIDTH_C), row(WIDTH_C),
                   pl.BlockSpec((1, 1, WIDTH_C, tm), lambda i: (i // nt, i % nt, 0, 0)),
                   tr(IDX_HEADS * IDX_DIM), row(LANES), tr(IDX_HEADS)],
        out_shape=[jax.ShapeDtypeStruct((batch, WIDTH_C, seq), BF16),
                   jax.ShapeDtypeStruct((t, WIDTH_C), BF16),
                   jax.ShapeDtypeStruct((batch, nt, WIDTH_C, tm), BF16),
                   jax.ShapeDtypeStruct((batch, IDX_HEADS * IDX_DIM, seq), BF16),
                   jax.ShapeDtypeStruct((t, LANES), BF16),
                   jax.ShapeDtypeStruct((batch, IDX_HEADS, seq), F32)],
        compiler_params=_params(1),
        name="front1",
    )(h, g, w, c, sa, sb, lng, lnb)


def _dsa_index_kernel(qit_ref, wit_ref, ki_ref, b_ref, sc_ref, *, n_sel):
    i = pl.program_id(1)
    tq = qit_ref.shape[2]
    kc_n = DSA_K_CHUNK
    n_chunks_total = sc_ref.shape[0]
    n_chunks = (i * tq + tq + kc_n - 1) // kc_n
    qpos = i * tq + lax.broadcasted_iota(jnp.int32, (1, tq), 1)
    kofs = lax.broadcasted_iota(jnp.int32, (kc_n, 1), 0)
    wv = wit_ref[0]

    def causal(c):
        return (c * kc_n + kofs) <= qpos

    def score_chunk(c, carry):
        kc = ki_ref[pl.ds(pl.multiple_of(c * kc_n, kc_n), kc_n), :]
        acc = jnp.zeros((kc_n, tq), F32)
        for h in range(IDX_HEADS):
            qm = _head_rows(qit_ref[0, (h // 2) * LANES:(h // 2 + 1) * LANES, :], h % 2)
            acc = acc + jnp.maximum(_dot(kc, qm), 0.0) * wv[h:h + 1, :]
        sc_ref[c] = jnp.where(causal(c), acc, NEG)
        return carry

    lax.fori_loop(0, n_chunks, score_chunk, 0)

    def reduce_chunks(fn, op, init):
        def body(c, acc):
            return op(acc, _fold_rows(fn(sc_ref[c]), op))
        return lax.fori_loop(0, n_chunks, body, jnp.full((SUBLANES, tq), init, F32))

    def count_ge(thr):
        acc = reduce_chunks(lambda x: jnp.where(x >= thr, 1.0, 0.0), jnp.add, 0.0)
        return jnp.sum(acc, axis=0, keepdims=True)

    k_f = float(n_sel)
    big = 3e38
    col_max = jnp.max(reduce_chunks(lambda x: x, jnp.maximum, -big), axis=0, keepdims=True)
    col_min = jnp.min(reduce_chunks(lambda x: jnp.where(x > 0.5 * NEG, x, big), jnp.minimum, big),
                      axis=0, keepdims=True)
    lo = jnp.where(qpos + 1 >= n_sel, col_min, NEG)
    hi = col_max + jnp.maximum(jnp.maximum(col_max - col_min, jnp.abs(col_max) * 1e-3), 1e-30)

    def bisect(_, carry):
        lo, hi = carry
        mid = 0.5 * (lo + hi)
        ok = count_ge(mid) >= k_f
        return jnp.where(ok, mid, lo), jnp.where(ok, hi, mid)

    lo, hi = lax.fori_loop(0, BISECT_STEPS, bisect, (lo, hi))

    def snap_cond(carry):
        return carry[3] > 0.5

    def snap(carry):
        hi, thr, done, _ = carry
        cand = jnp.max(reduce_chunks(lambda x: jnp.where(x < hi, x, -big), jnp.maximum, -big),
                       axis=0, keepdims=True)
        ok = count_ge(cand) >= k_f
        thr = jnp.where(done > 0.5, thr, cand)
        done = jnp.where(ok, 1.0, done)
        hi = jnp.where(done > 0.5, hi, cand)
        return hi, thr, done, jnp.max(1.0 - done)

    zeros = jnp.zeros((1, tq), F32)
    _, thr, _, _ = lax.while_loop(snap_cond, snap, (hi, zeros, zeros, jnp.float32(1.0)))

    n_gt = reduce_chunks(lambda x: jnp.where(x > thr, 1.0, 0.0), jnp.add, 0.0)
    need = k_f - jnp.sum(n_gt, axis=0, keepdims=True)
    n_ge = count_ge(thr)
    tied = jnp.max(jnp.where((n_ge > k_f) & (thr > 0.5 * NEG), 1.0, 0.0))

    @pl.when(tied < 0.5)
    def _():
        def body(c, carry):
            x = sc_ref[c]
            b_ref[0, 0, c] = jnp.where((x >= thr) & causal(c), 0.0, NEG)
            return carry
        lax.fori_loop(0, n_chunks, body, 0)

    @pl.when(tied > 0.5)
    def _():
        r = lax.broadcasted_iota(jnp.int32, (kc_n, kc_n), 0)
        cc = lax.broadcasted_iota(jnp.int32, (kc_n, kc_n), 1)
        prefix = jnp.where(cc <= r, 1.0, 0.0).astype(BF16)

        def body(c, seen):
            x = sc_ref[c]
            eq = x == thr
            cnt = seen + _dot(prefix, jnp.where(eq, 1.0, 0.0).astype(BF16))
            b_ref[0, 0, c] = jnp.where(((x > thr) | (eq & (cnt <= need))) & causal(c), 0.0, NEG)
            return cnt[kc_n - 1:kc_n, :]
        lax.fori_loop(0, n_chunks, body, jnp.zeros((1, tq), F32))

    def fill(c, carry):
        b_ref[0, 0, c] = jnp.full((kc_n, tq), NEG, F32)
        return carry

    lax.fori_loop(n_chunks, n_chunks_total, fill, 0)


def _dsa_index(qit, wit, ki, batch, seq):
    tq = Q_TILE
    nq = seq // tq
    nch = seq // DSA_K_CHUNK
    n_sel = min(DSA_TOPK, seq // 4)
    return pl.pallas_call(
        functools.partial(_dsa_index_kernel, n_sel=n_sel),
        grid=(batch, nq),
        in_specs=[pl.BlockSpec((1, qit.shape[1], tq), lambda b, i: (b, 0, i)),
                  pl.BlockSpec((1, IDX_HEADS, tq), lambda b, i: (b, 0, i)),
                  pl.BlockSpec((seq, LANES), lambda b, i: (b, 0))],
        out_specs=pl.BlockSpec((1, 1, nch, DSA_K_CHUNK, tq), lambda b, i: (b, i, 0, 0, 0)),
        out_shape=jax.ShapeDtypeStruct((batch, nq, nch, DSA_K_CHUNK, tq), F32),
        scratch_shapes=[pltpu.VMEM((nch, DSA_K_CHUNK, tq), F32)],
        compiler_params=_params(2),
        name="dsa_index",
    )(qit, wit, ki)


def _dsa_attn_kernel(qt_ref, k_ref, vt_ref, b_ref, o_ref, qm_ref, s_ref, cm_ref, m_ref, l_ref, acc_ref):
    i = pl.program_id(1)
    tq = qt_ref.shape[2]
    kc_n = DSA_K_CHUNK
    n_heads = acc_ref.shape[0]
    n_chunks = (i * tq + tq + kc_n - 1) // kc_n
    _softmax_init(m_ref, l_ref, acc_ref)
    for h in range(n_heads):
        qm_ref[h] = _head_rows(qt_ref[0, (h // 2) * LANES:(h // 2 + 1) * LANES, :], h % 2)

    def body(c, carry):
        st = pl.multiple_of(c * kc_n, kc_n)
        bias = b_ref[0, 0, c]

        def scores(h):
            return _dot(k_ref[pl.ds(st, kc_n), (h // 2) * LANES:(h // 2 + 1) * LANES], qm_ref[h]) + bias

        def values(h):
            return vt_ref[0, c, h * HEAD_DIM:(h + 1) * HEAD_DIM, :]

        _attend_heads(n_heads, scores, values, s_ref, cm_ref, m_ref, l_ref, acc_ref)
        return carry

    lax.fori_loop(0, n_chunks, body, 0)
    _write_heads(o_ref, l_ref, acc_ref)


def _dsa_attn(qt, k, vt, bias, batch, seq):
    t, width = k.shape
    tq = Q_TILE
    nq = seq // tq
    n_heads = width // HEAD_DIM
    one = pl.Buffered(1)
    return pl.pallas_call(
        _dsa_attn_kernel,
        grid=(batch, nq),
        in_specs=[pl.BlockSpec((1, width, tq), lambda b, i: (b, 0, i)),
                  pl.BlockSpec((seq, width), lambda b, i: (b, 0), pipeline_mode=one),
                  pl.BlockSpec((1,) + vt.shape[1:], lambda b, i: (b, 0, 0, 0), pipeline_mode=one),
                  pl.BlockSpec((1, 1) + bias.shape[2:], lambda b, i: (b, i, 0, 0, 0))],
        out_specs=pl.BlockSpec((tq, width), lambda b, i: (b * nq + i, 0)),
        out_shape=jax.ShapeDtypeStruct((t, width), BF16),
        scratch_shapes=[pltpu.VMEM((n_heads, LANES, tq), BF16), pltpu.VMEM((SCORE_SLOTS, DSA_K_CHUNK, tq), F32),
                        pltpu.VMEM((SCORE_SLOTS, SUBLANES, tq), F32),
                        pltpu.VMEM((n_heads, 1, tq), F32), pltpu.VMEM((n_heads, 1, tq), F32),
                        pltpu.VMEM((n_heads, HEAD_DIM, tq), F32)],
        compiler_params=_params(2),
        name="dsa_attn",
    )(qt, k, vt, bias)


def _rope_tables(positions):
    inv_freq = ROPE_THETA ** (-jnp.arange(0, ROPE_DIM, 2, dtype=F32) / ROPE_DIM)
    ang = positions.astype(F32).reshape(-1, 1) * inv_freq
    cos, sin = jnp.cos(ang), jnp.sin(ang)
    t = cos.shape[0]
    rest = HEAD_DIM - ROPE_DIM
    c = jnp.concatenate([cos, cos, jnp.ones((t, rest), F32)], axis=1)
    sa = jnp.concatenate([-sin, jnp.zeros((t, HEAD_DIM - ROPE_HALF), F32)], axis=1)
    sb = jnp.concatenate([jnp.zeros((t, ROPE_HALF), F32), sin, jnp.zeros((t, rest), F32)], axis=1)
    return tuple(jnp.tile(x, (1, LANES // HEAD_DIM)) for x in (c, sa, sb))


def kernel(x, p, positions, g_mix, w_in_even, a_ln_g, a_ln_b, a_w_s, a_b_s, w_out_even, w_in_odd, c_kidx_ln_g, c_kidx_ln_b, w_out_odd, g_ffn, w_ffn_gate, w_ffn_up, w_ffn_down, g_ple, w_ple_proj, w_ple_gate, g_final):
    batch, seq, d = x.shape
    depth = p.shape[0]
    t = batch * seq
    assert seq % ROW_TILE == 0 and seq % DSA_K_CHUNK == 0 and min(DSA_TOPK, seq // 4) <= DSA_K_CHUNK
    c, sa, sb = _rope_tables(positions)
    h = x.reshape(t, d)
    p2 = p.reshape(depth, t, p.shape[-1])
    row = lambda a: a.reshape(1, -1)
    onehot = (jnp.arange(seq)[:, None] // MOBA_BLOCK == jnp.arange(LANES)[None, :]).astype(BF16)
    for i in range(depth):
        j = i // 2
        if i % 2 == 0:
            bs = jnp.repeat(a_b_s[j].T, HEAD_DIM, axis=1)
            ya, qt, k, vt, km = _front0(h, row(g_mix[i]), w_in_even[j].astype(BF16), c, sa, sb,
                                        row(a_ln_g[j]), row(a_ln_b[j]), a_w_s[j], bs, batch, seq)
            yb = _moba(qt, k, vt, km.reshape(t // MOBA_BLOCK, WIDTH_B), onehot, batch, seq)
            wo = w_out_even[j].astype(BF16)
            ys, wos = [ya, yb], [wo[:WIDTH_A], wo[WIDTH_A:]]
        else:
            w = w_in_odd[j]
            o = 3 * WIDTH_C + IDX_HEADS * IDX_DIM
            w1 = jnp.concatenate([w[:, :o], w[:, o:o + IDX_DIM], w[:, o:o + IDX_DIM], w[:, o + IDX_DIM:],
                                  jnp.zeros((d, LANES - IDX_HEADS), w.dtype)], axis=1).astype(BF16)
            qt, k, vt, qit, ki, wit = _front1(h, row(g_mix[i]), w1, c, sa, sb, row(jnp.tile(c_kidx_ln_g[j], 2)),
                                              row(jnp.tile(c_kidx_ln_b[j], 2)), batch, seq)
            bias = _dsa_index(qit, wit, ki, batch, seq)
            ys, wos = [_dsa_attn(qt, k, vt, bias, batch, seq)], [w_out_odd[j].astype(BF16)]
        h = _post(h, ys, wos, p2[i], row(g_ffn[i]), w_ffn_gate[i].astype(BF16), w_ffn_up[i].astype(BF16),
                  w_ffn_down[i].astype(BF16), row(g_ple[i]), w_ple_gate[i].astype(BF16),
                  w_ple_proj[i].astype(BF16), row(g_final), final=(i == depth - 1))
    return h.reshape(batch, seq, d)
```

```python
import functools

import jax
import jax.numpy as jnp
from jax import lax
from jax.experimental import pallas as pl
from jax.experimental.pallas import tpu as pltpu

HEAD_DIM = 64
ROPE_DIM = HEAD_DIM // 4
ROPE_HALF = ROPE_DIM // 2
ROPE_THETA = 500000.0
N_GROUPS_A = 8
WIDTH_A = N_GROUPS_A * HEAD_DIM
N_HEADS_B = 8
WIDTH_B = N_HEADS_B * HEAD_DIM
N_HEADS_C = 16
WIDTH_C = N_HEADS_C * HEAD_DIM
SGU_CHUNK = 128
MOBA_BLOCK = 256
MOBA_TOPK = 3
IDX_HEADS = 8
IDX_DIM = 64
DSA_TOPK = 256
EPS = 1e-6
NEG = -1e30
LOG2E = 1.4426950408889634

LANES = 128
SUBLANES = 8
ROW_TILE = 512
Q_TILE = 256
DSA_K_CHUNK = 512
FOLD_CHAINS = 8
ONES_ROWS = 16
SCORE_SLOTS = 4
MOBA_SLOTS = 8
BISECT_CAP = 48
VMEM_LIMIT = 56 * 1024 * 1024

F32 = jnp.float32
BF16 = jnp.bfloat16


def _dot(a, b):
    return jnp.dot(a, b, preferred_element_type=F32)


def _rms(x, g):
    return x * lax.rsqrt(jnp.mean(x * x, axis=-1, keepdims=True) + EPS) * g


def _rope(z, c, sa, sb):
    outs = []
    for j in range(z.shape[1] // LANES):
        zz = z[:, j * LANES:(j + 1) * LANES]
        outs.append(zz * c + pltpu.roll(zz, LANES - ROPE_HALF, 1) * sa + pltpu.roll(zz, ROPE_HALF, 1) * sb)
    return outs[0] if len(outs) == 1 else jnp.concatenate(outs, axis=1)


def _gelu(x):
    return 0.5 * x * (1.0 + lax.erf(x * (2.0 ** -0.5)))


def _fold_rows(x, op):
    parts = [x[a:a + SUBLANES] for a in range(0, x.shape[0], SUBLANES)]
    chains = parts[:FOLD_CHAINS]
    for a, part in enumerate(parts[FOLD_CHAINS:]):
        chains[a % FOLD_CHAINS] = op(chains[a % FOLD_CHAINS], part)
    while len(chains) > 1:
        chains = [op(chains[a], chains[a + 1]) if a + 1 < len(chains) else chains[a] for a in range(0, len(chains), 2)]
    return chains[0]


def _head_rows(x, hh):
    r = lax.broadcasted_iota(jnp.int32, (LANES, 1), 0)
    return jnp.where((r >= hh * HEAD_DIM) & (r < (hh + 1) * HEAD_DIM), x, jnp.zeros_like(x))


def _const_spec(shape):
    nd = len(shape)
    return pl.BlockSpec(shape, lambda *_: (0,) * nd, pipeline_mode=pl.Buffered(1))


def _params(n_axes, flags=None):
    return pltpu.CompilerParams(dimension_semantics=("arbitrary",) * n_axes, vmem_limit_bytes=VMEM_LIMIT, flags=flags)


def _stage_scores(score_fn, heads, s_ref, cm_ref):
    n_slots = s_ref.shape[0]
    for h in heads:
        s = score_fn(h)
        s_ref[h % n_slots] = s
        cm_ref[h % n_slots] = _fold_rows(s, jnp.maximum)


def _attend_chunk(n_heads, score_fn, next_score_fn, vt_fn, s_ref, cm_ref, m_ref, acc_ref):
    n_slots = s_ref.shape[0]
    ahead = n_slots - 1
    assert n_heads % n_slots == 0
    for h in range(n_heads):
        if h + ahead < n_heads:
            _stage_scores(score_fn, [h + ahead], s_ref, cm_ref)
        else:
            _stage_scores(next_score_fn, [h + ahead - n_heads], s_ref, cm_ref)
        slot = h % n_slots
        m_old = m_ref[h]
        m_new = jnp.maximum(m_old, jnp.max(cm_ref[slot], axis=0, keepdims=True))
        p = jnp.exp2(s_ref[slot] - m_new).astype(BF16)
        m_ref[h] = m_new
        vt = vt_fn(h)
        vt = jnp.concatenate([vt, jnp.ones((acc_ref.shape[1] - vt.shape[0], vt.shape[1]), BF16)], axis=0)
        acc_ref[h] = jnp.exp2(m_old - m_new) * acc_ref[h] + _dot(vt, p)


def _softmax_init(m_ref, acc_ref):
    m_ref[...] = jnp.full(m_ref.shape, -jnp.inf, F32)
    acc_ref[...] = jnp.zeros(acc_ref.shape, F32)


def _write_heads(o_ref, acc_ref):
    def head(h):
        return acc_ref[h, :HEAD_DIM, :] / acc_ref[h, HEAD_DIM:HEAD_DIM + 1, :]

    for p in range(acc_ref.shape[0] // 2):
        o_ref[:, p * LANES:(p + 1) * LANES] = jnp.concatenate([head(2 * p), head(2 * p + 1)], axis=0).T.astype(BF16)


def _front0_kernel(h_ref, g_ref, w_ref, c_ref, sa_ref, sb_ref, lng_ref, lnb_ref, ws_ref, bs_ref,
                   ya_ref, qt_ref, k_ref, vt_ref, km_ref):
    tm = h_ref.shape[0]
    xn = _rms(h_ref[...], g_ref[...]).astype(BF16)

    def proj(lo, hi):
        return _dot(xn, w_ref[:, lo:hi])

    u = _gelu(proj(0, WIDTH_A))
    v = _gelu(proj(WIDTH_A, 2 * WIDTH_A))
    lane = lax.broadcasted_iota(jnp.int32, (1, LANES), 1)
    low = lane < HEAD_DIM
    row = lax.broadcasted_iota(jnp.int32, (SGU_CHUNK, SGU_CHUNK), 0)
    col = lax.broadcasted_iota(jnp.int32, (SGU_CHUNK, SGU_CHUNK), 1)
    tril = col <= row
    for j in range(WIDTH_A // LANES):
        sl = slice(j * LANES, (j + 1) * LANES)
        vv = v[:, sl]

        def half_mean(t):
            s_lo = jnp.sum(jnp.where(low, t, 0.0), axis=-1, keepdims=True)
            s_hi = jnp.sum(jnp.where(low, 0.0, t), axis=-1, keepdims=True)
            return jnp.where(low, s_lo, s_hi) * (1.0 / HEAD_DIM)

        xc = vv - half_mean(vv)
        var = half_mean(xc * xc)
        vn = (xc * lax.rsqrt(var + EPS) * lng_ref[:, sl] + lnb_ref[:, sl]).astype(BF16)
        w_lo = jnp.where(tril, ws_ref[2 * j], 0.0).astype(BF16)
        w_hi = jnp.where(tril, ws_ref[2 * j + 1], 0.0).astype(BF16)
        for c in range(tm // SGU_CHUNK):
            rs = slice(c * SGU_CHUNK, (c + 1) * SGU_CHUNK)
            vc = vn[rs]
            mixed = jnp.where(low, _dot(w_lo, vc), _dot(w_hi, vc)) + bs_ref[:, sl]
            ya_ref[rs, sl] = (u[rs, sl] * mixed).astype(BF16)

    o = 2 * WIDTH_A
    c, sa, sb = c_ref[...], sa_ref[...], sb_ref[...]
    q = _rope(proj(o, o + WIDTH_B), c, sa, sb) * (HEAD_DIM ** -0.5 * LOG2E)
    qt_ref[0] = q.T.astype(BF16)
    k = _rope(proj(o + WIDTH_B, o + 2 * WIDTH_B), c, sa, sb)
    k_ref[...] = k.astype(BF16)
    v_b = proj(o + 2 * WIDTH_B, o + 3 * WIDTH_B)
    for blk in range(tm // MOBA_BLOCK):
        rs = slice(blk * MOBA_BLOCK, (blk + 1) * MOBA_BLOCK)
        km_ref[0, blk:blk + 1, :] = jnp.mean(k[rs], axis=0, keepdims=True)
        vt_ref[0, blk] = v_b[rs].T.astype(BF16)


def _front0(h, g, w, c, sa, sb, lng, lnb, ws, bs, batch, seq):
    t, d = h.shape
    tm = ROW_TILE
    nt = seq // tm
    nblk = tm // MOBA_BLOCK
    row = lambda n: pl.BlockSpec((tm, n), lambda i: (i, 0))
    return pl.pallas_call(
        _front0_kernel,
        grid=(t // tm,),
        in_specs=[row(d), _const_spec((1, d)), _const_spec(w.shape), row(LANES), row(LANES), row(LANES),
                  _const_spec(lng.shape), _const_spec(lnb.shape), _const_spec(ws.shape), _const_spec(bs.shape)],
        out_specs=[row(WIDTH_A),
                   pl.BlockSpec((1, WIDTH_B, tm), lambda i: (i // nt, 0, i % nt)),
                   row(WIDTH_B),
                   pl.BlockSpec((1, nblk, WIDTH_B, MOBA_BLOCK), lambda i: (i // nt, i % nt, 0, 0)),
                   pl.BlockSpec((1, nblk, WIDTH_B), lambda i: (i, 0, 0))],
        out_shape=[jax.ShapeDtypeStruct((t, WIDTH_A), BF16),
                   jax.ShapeDtypeStruct((batch, WIDTH_B, seq), BF16),
                   jax.ShapeDtypeStruct((t, WIDTH_B), BF16),
                   jax.ShapeDtypeStruct((batch, seq // MOBA_BLOCK, WIDTH_B, MOBA_BLOCK), BF16),
                   jax.ShapeDtypeStruct((t // tm, nblk, WIDTH_B), F32)],
        compiler_params=_params(1),
        name="front0",
    )(h, g, w, c, sa, sb, lng, lnb, ws, bs)


def _moba_kernel(qt_ref, k_ref, vt_ref, km_ref, oh_ref, o_ref, qc_ref, s_ref, cm_ref, m_ref, acc_ref):
    i = pl.program_id(1)
    tq = qt_ref.shape[2]
    nb = km_ref.shape[0]
    n_heads = acc_ref.shape[0]
    blk_f = lax.broadcasted_iota(jnp.int32, (nb, 1), 0).astype(F32)
    past = lax.broadcasted_iota(jnp.int32, (nb, 1), 0) < i
    causal = (lax.broadcasted_iota(jnp.int32, (tq, tq), 0) <= lax.broadcasted_iota(jnp.int32, (tq, tq), 1))
    own = pl.multiple_of(i * tq, tq)
    _softmax_init(m_ref, acc_ref)
    for h in range(n_heads):
        sl = slice((h // 2) * LANES, (h // 2 + 1) * LANES)
        qm = _head_rows(qt_ref[0, sl, :], h % 2)
        g = jnp.where(past, _dot(km_ref[:, sl].astype(BF16), qm), NEG)
        sel = jnp.zeros(g.shape, jnp.bool_)
        for _ in range(min(MOBA_TOPK, nb)):
            top = jnp.max(g, axis=0, keepdims=True)
            idx = jnp.min(jnp.where(g == top, blk_f, 1e9), axis=0, keepdims=True)
            pick = blk_f == idx
            sel = sel | pick
            g = jnp.where(pick, -jnp.inf, g)
        selb = jnp.where(sel & past, 0.0, NEG).astype(BF16)
        qc_ref[h] = jnp.concatenate([qm, selb, jnp.zeros((LANES - nb, tq), BF16)], axis=0)

    def pair_lanes(h):
        return slice((h // 2) * LANES, (h // 2 + 1) * LANES)

    def values(j):
        return lambda h: vt_ref[0, j, h * HEAD_DIM:(h + 1) * HEAD_DIM, :]

    def own_scores(h):
        return jnp.where(causal, _dot(k_ref[pl.ds(own, tq), pair_lanes(h)], qc_ref[h, :LANES, :]), NEG)

    def past_scores(j):
        st = pl.multiple_of(j * tq, tq)

        def scores(h):
            kc = jnp.concatenate([k_ref[pl.ds(st, tq), pair_lanes(h)], oh_ref[pl.ds(st, tq), :]], axis=1)
            return _dot(kc, qc_ref[h])

        return scores

    refs = (s_ref, cm_ref, m_ref, acc_ref)
    last = jnp.maximum(i - 1, 0)
    _stage_scores(own_scores, range(s_ref.shape[0] - 1), s_ref, cm_ref)
    _attend_chunk(n_heads, own_scores, past_scores(0), values(i), *refs)

    def body(j, carry):
        _attend_chunk(n_heads, past_scores(j), past_scores(jnp.minimum(j + 1, last)), values(j), *refs)
        return carry

    lax.fori_loop(0, i, body, 0)
    _write_heads(o_ref, acc_ref)


def _moba(qt, k, vt, km, onehot, batch, seq):
    t, width = k.shape
    tq = Q_TILE
    assert tq == MOBA_BLOCK
    nb = seq // MOBA_BLOCK
    nq = seq // tq
    n_heads = width // HEAD_DIM
    one = pl.Buffered(1)
    return pl.pallas_call(
        _moba_kernel,
        grid=(batch, nq),
        in_specs=[pl.BlockSpec((1, width, tq), lambda b, i: (b, 0, i)),
                  pl.BlockSpec((seq, width), lambda b, i: (b, 0), pipeline_mode=one),
                  pl.BlockSpec((1, nb, width, MOBA_BLOCK), lambda b, i: (b, 0, 0, 0), pipeline_mode=one),
                  pl.BlockSpec((nb, width), lambda b, i: (b, 0)),
                  _const_spec(onehot.shape)],
        out_specs=pl.BlockSpec((tq, width), lambda b, i: (b * nq + i, 0)),
        out_shape=jax.ShapeDtypeStruct((t, width), BF16),
        scratch_shapes=[pltpu.VMEM((n_heads, 2 * LANES, tq), BF16), pltpu.VMEM((MOBA_SLOTS, MOBA_BLOCK, tq), F32),
                        pltpu.VMEM((MOBA_SLOTS, SUBLANES, tq), F32),
                        pltpu.VMEM((n_heads, 1, tq), F32), pltpu.VMEM((n_heads, HEAD_DIM + ONES_ROWS, tq), F32)],
        compiler_params=_params(2),
        name="moba",
    )(qt, k, vt, km, onehot)


def _post_kernel(*refs, n_mix, final):
    h_ref = refs[0]
    y_refs = refs[1:1 + n_mix]
    wo_refs = refs[1 + n_mix:1 + 2 * n_mix]
    (p_ref, gf_ref, wg_ref, wu_ref, wd_ref, gp_ref, wpg_ref, wpp_ref, gl_ref, o_ref, a_ref) = refs[1 + 2 * n_mix:]
    h = h_ref[...]
    for y_ref, wo_ref in zip(y_refs, wo_refs):
        h = h + _dot(y_ref[...], wo_ref[...])
    xn = _rms(h, gf_ref[...]).astype(BF16)
    d_ff = wg_ref.shape[1]
    step = 2 * LANES
    for lo in range(0, d_ff, step):
        gate = _dot(xn, wg_ref[:, lo:lo + step])
        up = _dot(xn, wu_ref[:, lo:lo + step])
        a_ref[:, lo:lo + step] = (gate * jax.nn.sigmoid(gate) * up).astype(BF16)
    h = h + _dot(a_ref[...], wd_ref[...])
    pg = jax.nn.sigmoid(_dot(_rms(h, gp_ref[...]).astype(BF16), wpg_ref[...]))
    h = h + _dot(p_ref[...].astype(BF16), wpp_ref[...]) * pg
    o_ref[...] = _rms(h, gl_ref[...]) if final else h


def _post(h, ys, wos, p, gf, wg, wu, wd, gp, wpg, wpp, gl, final):
    t, d = h.shape
    tm = ROW_TILE
    row = lambda n: pl.BlockSpec((tm, n), lambda i: (i, 0))
    consts = [gf, wg, wu, wd, gp, wpg, wpp, gl]
    return pl.pallas_call(
        functools.partial(_post_kernel, n_mix=len(ys), final=final),
        grid=(t // tm,),
        in_specs=[row(d)] + [row(y.shape[1]) for y in ys] + [_const_spec(w.shape) for w in wos]
                 + [row(p.shape[1])] + [_const_spec(c.shape) for c in consts],
        out_specs=row(d),
        out_shape=jax.ShapeDtypeStruct((t, d), F32),
        scratch_shapes=[pltpu.VMEM((tm, wg.shape[1]), BF16)],
        compiler_params=_params(1),
        name="post_final" if final else "post",
    )(h, *ys, *wos, p, *consts)


def _front1_kernel(h_ref, g_ref, w_ref, c_ref, sa_ref, sb_ref, lng_ref, lnb_ref,
                   qt_ref, k_ref, vt_ref, qit_ref, ki_ref, wit_ref):
    xn = _rms(h_ref[...], g_ref[...]).astype(BF16)

    def proj(lo, hi):
        return _dot(xn, w_ref[:, lo:hi])

    c, sa, sb = c_ref[...], sa_ref[...], sb_ref[...]
    qt_ref[0] = (_rope(proj(0, WIDTH_C), c, sa, sb) * (HEAD_DIM ** -0.5 * LOG2E)).T.astype(BF16)
    k_ref[...] = _rope(proj(WIDTH_C, 2 * WIDTH_C), c, sa, sb).astype(BF16)
    vt_ref[0, 0] = proj(2 * WIDTH_C, 3 * WIDTH_C).T.astype(BF16)
    o = 3 * WIDTH_C
    qit_ref[0] = _rope(proj(o, o + IDX_HEADS * IDX_DIM), c, sa, sb).T.astype(BF16)
    o += IDX_HEADS * IDX_DIM
    ki = proj(o, o + LANES)
    xc = ki - jnp.mean(ki, axis=-1, keepdims=True)
    var = jnp.mean(xc * xc, axis=-1, keepdims=True)
    ki = xc * lax.rsqrt(var + EPS) * lng_ref[...] + lnb_ref[...]
    ki_ref[...] = _rope(ki, c, sa, sb).astype(BF16)
    wi = proj(o + LANES, o + 2 * LANES) * ((IDX_HEADS ** -0.5) * (IDX_DIM ** -0.5))
    wit_ref[0] = wi.T[:IDX_HEADS]


def _front1(h, g, w, c, sa, sb, lng, lnb, batch, seq):
    t, d = h.shape
    tm = ROW_TILE
    assert tm == DSA_K_CHUNK
    nt = seq // tm
    row = lambda n: pl.BlockSpec((tm, n), lambda i: (i, 0))
    tr = lambda n: pl.BlockSpec((1, n, tm), lambda i: (i // nt, 0, i % nt))
    return pl.pallas_call(
        _front1_kernel,
        grid=(t // tm,),
        in_specs=[row(d), _const_spec((1, d)), _const_spec(w.shape), row(LANES), row(LANES), row(LANES),
                  _const_spec(lng.shape), _const_spec(lnb.shape)],
        out_specs=[tr(WIDTH_C), row(WIDTH_C),
                   pl.BlockSpec((1, 1, WIDTH_C, tm), lambda i: (i // nt, i % nt, 0, 0)),
                   tr(IDX_HEADS * IDX_DIM), row(LANES), tr(IDX_HEADS)],
        out_shape=[jax.ShapeDtypeStruct((batch, WIDTH_C, seq), BF16),
                   jax.ShapeDtypeStruct((t, WIDTH_C), BF16),
                   jax.ShapeDtypeStruct((batch, nt, WIDTH_C, tm), BF16),
                   jax.ShapeDtypeStruct((batch, IDX_HEADS * IDX_DIM, seq), BF16),
                   jax.ShapeDtypeStruct((t, LANES), BF16),
                   jax.ShapeDtypeStruct((batch, IDX_HEADS, seq), F32)],
        compiler_params=_params(1),
        name="front1",
    )(h, g, w, c, sa, sb, lng, lnb)


def _dsa_index_kernel(qit_ref, wit_ref, ki_ref, b_ref, sc_ref, *, n_sel):
    i = pl.program_id(1)
    tq = qit_ref.shape[2]
    kc_n = DSA_K_CHUNK
    n_chunks_total = sc_ref.shape[0]
    n_chunks = (i * tq + tq + kc_n - 1) // kc_n
    qpos = i * tq + lax.broadcasted_iota(jnp.int32, (1, tq), 1)
    kofs = lax.broadcasted_iota(jnp.int32, (kc_n, 1), 0)
    wv = wit_ref[0]

    def causal(c):
        return (c * kc_n + kofs) <= qpos

    big = 3e38

    def score_chunk(c, carry):
        mx, mn = carry
        kc = ki_ref[pl.ds(pl.multiple_of(c * kc_n, kc_n), kc_n), :]
        acc = jnp.zeros((kc_n, tq), F32)
        for h in range(IDX_HEADS):
            qm = _head_rows(qit_ref[0, (h // 2) * LANES:(h // 2 + 1) * LANES, :], h % 2)
            acc = acc + jnp.maximum(_dot(kc, qm), 0.0) * wv[h:h + 1, :]
        ok = causal(c)
        sc_ref[c] = jnp.where(ok, acc, NEG)
        return (jnp.maximum(mx, _fold_rows(jnp.where(ok, acc, -big), jnp.maximum)),
                jnp.minimum(mn, _fold_rows(jnp.where(ok, acc, big), jnp.minimum)))

    mx, mn = lax.fori_loop(0, n_chunks, score_chunk,
                           (jnp.full((SUBLANES, tq), -big, F32), jnp.full((SUBLANES, tq), big, F32)))
    col_max = jnp.max(mx, axis=0, keepdims=True)
    col_min = jnp.min(mn, axis=0, keepdims=True)

    def reduce_chunks(fn, op, init):
        def body(c, acc):
            return op(acc, _fold_rows(fn(sc_ref[c]), op))
        return lax.fori_loop(0, n_chunks, body, jnp.full((SUBLANES, tq), init, F32))

    def count_ge(thr):
        acc = reduce_chunks(lambda x: jnp.where(x >= thr, 1.0, 0.0), jnp.add, 0.0)
        return jnp.sum(acc, axis=0, keepdims=True)

    def pending(flags):
        return jnp.max(jnp.where(flags, 0.0, 1.0))

    k_f = float(n_sel)
    n_causal = (qpos + 1).astype(F32)
    small = n_causal < k_f
    lo = jnp.where(small, NEG, col_min)
    hi = col_max + jnp.maximum(jnp.maximum(col_max - col_min, jnp.abs(col_max) * 1e-3), 1e-30)
    c_lo = jnp.where(small, 2.0 * k_f, n_causal)
    c_hi = jnp.zeros((1, tq), F32)

    def settled(c_lo, c_hi):
        return small | (c_hi == k_f - 1.0) | (c_lo == k_f)

    def bisect_cond(carry):
        return (carry[4] < BISECT_CAP) & (carry[5] > 0.5)

    def bisect(carry):
        lo, hi, c_lo, c_hi, it, _ = carry
        mid = 0.5 * (lo + hi)
        cnt = count_ge(mid)
        ok = cnt >= k_f
        lo, c_lo = jnp.where(ok, mid, lo), jnp.where(ok, cnt, c_lo)
        hi, c_hi = jnp.where(ok, hi, mid), jnp.where(ok, c_hi, cnt)
        return lo, hi, c_lo, c_hi, it + 1, pending(settled(c_lo, c_hi))

    lo, hi, c_lo, c_hi, _, _ = lax.while_loop(
        bisect_cond, bisect, (lo, hi, c_lo, c_hi, jnp.int32(0), pending(settled(c_lo, c_hi))))

    def ends_body(c, carry):
        below, above = carry
        x = sc_ref[c]
        return (jnp.maximum(below, _fold_rows(jnp.where(x < hi, x, -big), jnp.maximum)),
                jnp.minimum(above, _fold_rows(jnp.where(x >= lo, x, big), jnp.minimum)))

    below, above = lax.fori_loop(0, n_chunks, ends_body,
                                 (jnp.full((SUBLANES, tq), -big, F32), jnp.full((SUBLANES, tq), big, F32)))
    below = jnp.max(below, axis=0, keepdims=True)
    above = jnp.min(above, axis=0, keepdims=True)
    thr = jnp.where(small, NEG, jnp.where(c_lo == k_f, above, below))
    done = jnp.where(settled(c_lo, c_hi), 1.0, 0.0)

    def snap_cond(carry):
        return carry[3] > 0.5

    def snap(carry):
        hi, thr, done, _ = carry
        cand = jnp.max(reduce_chunks(lambda x: jnp.where(x < hi, x, -big), jnp.maximum, -big),
                       axis=0, keepdims=True)
        ok = count_ge(cand) >= k_f
        thr = jnp.where(done > 0.5, thr, cand)
        done = jnp.where(ok, 1.0, done)
        hi = jnp.where(done > 0.5, hi, cand)
        return hi, thr, done, jnp.max(1.0 - done)

    _, thr, _, _ = lax.while_loop(snap_cond, snap, (hi, thr, done, jnp.max(1.0 - done)))

    n_ge = count_ge(thr)
    tied = jnp.max(jnp.where((n_ge > k_f) & jnp.logical_not(small), 1.0, 0.0))

    @pl.when(tied < 0.5)
    def _():
        def body(c, carry):
            x = sc_ref[c]
            b_ref[0, 0, c] = jnp.where((x >= thr) & causal(c), 0.0, NEG)
            return carry
        lax.fori_loop(0, n_chunks, body, 0)

    @pl.when(tied > 0.5)
    def _():
        r = lax.broadcasted_iota(jnp.int32, (kc_n, kc_n), 0)
        cc = lax.broadcasted_iota(jnp.int32, (kc_n, kc_n), 1)
        prefix = jnp.where(cc <= r, 1.0, 0.0).astype(BF16)
        n_gt = reduce_chunks(lambda x: jnp.where(x > thr, 1.0, 0.0), jnp.add, 0.0)
        need = k_f - jnp.sum(n_gt, axis=0, keepdims=True)

        def body(c, seen):
            x = sc_ref[c]
            eq = x == thr
            cnt = seen + _dot(prefix, jnp.where(eq, 1.0, 0.0).astype(BF16))
            b_ref[0, 0, c] = jnp.where(((x > thr) | (eq & (cnt <= need))) & causal(c), 0.0, NEG)
            return cnt[kc_n - 1:kc_n, :]
        lax.fori_loop(0, n_chunks, body, jnp.zeros((1, tq), F32))

    def fill(c, carry):
        b_ref[0, 0, c] = jnp.full((kc_n, tq), NEG, F32)
        return carry

    lax.fori_loop(n_chunks, n_chunks_total, fill, 0)


def _dsa_index(qit, wit, ki, batch, seq):
    tq = Q_TILE
    nq = seq // tq
    nch = seq // DSA_K_CHUNK
    n_sel = min(DSA_TOPK, seq // 4)
    return pl.pallas_call(
        functools.partial(_dsa_index_kernel, n_sel=n_sel),
        grid=(batch, nq),
        in_specs=[pl.BlockSpec((1, qit.shape[1], tq), lambda b, i: (b, 0, i)),
                  pl.BlockSpec((1, IDX_HEADS, tq), lambda b, i: (b, 0, i)),
                  pl.BlockSpec((seq, LANES), lambda b, i: (b, 0))],
        out_specs=pl.BlockSpec((1, 1, nch, DSA_K_CHUNK, tq), lambda b, i: (b, i, 0, 0, 0)),
        out_shape=jax.ShapeDtypeStruct((batch, nq, nch, DSA_K_CHUNK, tq), F32),
        scratch_shapes=[pltpu.VMEM((nch, DSA_K_CHUNK, tq), F32)],
        compiler_params=_params(2),
        name="dsa_index",
    )(qit, wit, ki)


def _dsa_attn_kernel(qt_ref, k_ref, vt_ref, b_ref, o_ref, qm_ref, s_ref, cm_ref, m_ref, acc_ref):
    i = pl.program_id(1)
    tq = qt_ref.shape[2]
    kc_n = DSA_K_CHUNK
    n_heads = acc_ref.shape[0]
    n_chunks = (i * tq + tq + kc_n - 1) // kc_n
    _softmax_init(m_ref, acc_ref)
    for h in range(n_heads):
        qm_ref[h] = _head_rows(qt_ref[0, (h // 2) * LANES:(h // 2 + 1) * LANES, :], h % 2)

    def chunk_scores(c):
        st = pl.multiple_of(c * kc_n, kc_n)

        def scores(h):
            return _dot(k_ref[pl.ds(st, kc_n), (h // 2) * LANES:(h // 2 + 1) * LANES], qm_ref[h]) + b_ref[0, 0, c]

        return scores

    _stage_scores(chunk_scores(0), range(s_ref.shape[0] - 1), s_ref, cm_ref)

    def body(c, carry):
        def values(h):
            return vt_ref[0, c, h * HEAD_DIM:(h + 1) * HEAD_DIM, :]

        _attend_chunk(n_heads, chunk_scores(c), chunk_scores(jnp.minimum(c + 1, n_chunks - 1)), values,
                      s_ref, cm_ref, m_ref, acc_ref)
        return carry

    lax.fori_loop(0, n_chunks, body, 0)
    _write_heads(o_ref, acc_ref)


def _dsa_attn(qt, k, vt, bias, batch, seq):
    t, width = k.shape
    tq = Q_TILE
    nq = seq // tq
    n_heads = width // HEAD_DIM
    one = pl.Buffered(1)
    return pl.pallas_call(
        _dsa_attn_kernel,
        grid=(batch, nq),
        in_specs=[pl.BlockSpec((1, width, tq), lambda b, i: (b, 0, i)),
                  pl.BlockSpec((seq, width), lambda b, i: (b, 0), pipeline_mode=one),
                  pl.BlockSpec((1,) + vt.shape[1:], lambda b, i: (b, 0, 0, 0), pipeline_mode=one),
                  pl.BlockSpec((1, 1) + bias.shape[2:], lambda b, i: (b, i, 0, 0, 0))],
        out_specs=pl.BlockSpec((tq, width), lambda b, i: (b * nq + i, 0)),
        out_shape=jax.ShapeDtypeStruct((t, width), BF16),
        scratch_shapes=[pltpu.VMEM((n_heads, LANES, tq), BF16), pltpu.VMEM((SCORE_SLOTS, DSA_K_CHUNK, tq), F32),
                        pltpu.VMEM((SCORE_SLOTS, SUBLANES, tq), F32),
                        pltpu.VMEM((n_heads, 1, tq), F32), pltpu.VMEM((n_heads, HEAD_DIM + ONES_ROWS, tq), F32)],
        compiler_params=_params(2),
        name="dsa_attn",
    )(qt, k, vt, bias)


def _rope_tables(positions):
    inv_freq = ROPE_THETA ** (-jnp.arange(0, ROPE_DIM, 2, dtype=F32) / ROPE_DIM)
    ang = positions.astype(F32).reshape(-1, 1) * inv_freq
    cos, sin = jnp.cos(ang), jnp.sin(ang)
    t = cos.shape[0]
    rest = HEAD_DIM - ROPE_DIM
    c = jnp.concatenate([cos, cos, jnp.ones((t, rest), F32)], axis=1)
    sa = jnp.concatenate([-sin, jnp.zeros((t, HEAD_DIM - ROPE_HALF), F32)], axis=1)
    sb = jnp.concatenate([jnp.zeros((t, ROPE_HALF), F32), sin, jnp.zeros((t, rest), F32)], axis=1)
    return tuple(jnp.tile(x, (1, LANES // HEAD_DIM)) for x in (c, sa, sb))


def kernel(x, p, positions, g_mix, w_in_even, a_ln_g, a_ln_b, a_w_s, a_b_s, w_out_even, w_in_odd, c_kidx_ln_g, c_kidx_ln_b, w_out_odd, g_ffn, w_ffn_gate, w_ffn_up, w_ffn_down, g_ple, w_ple_proj, w_ple_gate, g_final):
    batch, seq, d = x.shape
    depth = p.shape[0]
    t = batch * seq
    assert seq % ROW_TILE == 0 and seq % DSA_K_CHUNK == 0 and min(DSA_TOPK, seq // 4) <= DSA_K_CHUNK
    c, sa, sb = _rope_tables(positions)
    h = x.reshape(t, d)
    p2 = p.reshape(depth, t, p.shape[-1])
    row = lambda a: a.reshape(1, -1)
    onehot = (jnp.arange(seq)[:, None] // MOBA_BLOCK == jnp.arange(LANES)[None, :]).astype(BF16)
    for i in range(depth):
        j = i // 2
        if i % 2 == 0:
            bs = jnp.repeat(a_b_s[j].T, HEAD_DIM, axis=1)
            ya, qt, k, vt, km = _front0(h, row(g_mix[i]), w_in_even[j].astype(BF16), c, sa, sb,
                                        row(a_ln_g[j]), row(a_ln_b[j]), a_w_s[j], bs, batch, seq)
            yb = _moba(qt, k, vt, km.reshape(t // MOBA_BLOCK, WIDTH_B), onehot, batch, seq)
            wo = w_out_even[j].astype(BF16)
            ys, wos = [ya, yb], [wo[:WIDTH_A], wo[WIDTH_A:]]
        else:
            w = w_in_odd[j]
            o = 3 * WIDTH_C + IDX_HEADS * IDX_DIM
            w1 = jnp.concatenate([w[:, :o], w[:, o:o + IDX_DIM], w[:, o:o + IDX_DIM], w[:, o + IDX_DIM:],
                                  jnp.zeros((d, LANES - IDX_HEADS), w.dtype)], axis=1).astype(BF16)
            qt, k, vt, qit, ki, wit = _front1(h, row(g_mix[i]), w1, c, sa, sb, row(jnp.tile(c_kidx_ln_g[j], 2)),
                                              row(jnp.tile(c_kidx_ln_b[j], 2)), batch, seq)
            bias = _dsa_index(qit, wit, ki, batch, seq)
            ys, wos = [_dsa_attn(qt, k, vt, bias, batch, seq)], [w_out_odd[j].astype(BF16)]
        h = _post(h, ys, wos, p2[i], row(g_ffn[i]), w_ffn_gate[i].astype(BF16), w_ffn_up[i].astype(BF16),
                  w_ffn_down[i].astype(BF16), row(g_ple[i]), w_ple_gate[i].astype(BF16),
                  w_ple_proj[i].astype(BF16), row(g_final), final=(i == depth - 1))
    return h.reshape(batch, seq, d)
```

```python
import functools

import jax
import jax.numpy as jnp
from jax import lax
from jax.experimental import pallas as pl
from jax.experimental.pallas import tpu as pltpu

HEAD_DIM = 64
ROPE_DIM = HEAD_DIM // 4
ROPE_HALF = ROPE_DIM // 2
ROPE_THETA = 500000.0
N_GROUPS_A = 8
WIDTH_A = N_GROUPS_A * HEAD_DIM
N_HEADS_B = 8
WIDTH_B = N_HEADS_B * HEAD_DIM
N_HEADS_C = 16
WIDTH_C = N_HEADS_C * HEAD_DIM
SGU_CHUNK = 128
MOBA_BLOCK = 256
MOBA_TOPK = 3
IDX_HEADS = 8
IDX_DIM = 64
DSA_TOPK = 256
EPS = 1e-6
NEG = -1e30
LOG2E = 1.4426950408889634

LANES = 128
SUBLANES = 8
ROW_TILE = 512
Q_TILE = 256
DSA_K_CHUNK = 512
FOLD_CHAINS = 8
ONES_ROWS = 16
SCORE_SLOTS = 4
MOBA_SLOTS = 4
MOBA_K_CHUNK = 2 * MOBA_BLOCK
BISECT_WARMUP = 8
BISECT_PER_ROUND = 4
BISECT_ROUNDS = 12
VMEM_LIMIT = 56 * 1024 * 1024

F32 = jnp.float32
BF16 = jnp.bfloat16


def _dot(a, b):
    return jnp.dot(a, b, preferred_element_type=F32)


def _rms(x, g):
    return x * lax.rsqrt(jnp.mean(x * x, axis=-1, keepdims=True) + EPS) * g


def _rope(z, c, sa, sb):
    outs = []
    for j in range(z.shape[1] // LANES):
        zz = z[:, j * LANES:(j + 1) * LANES]
        outs.append(zz * c + pltpu.roll(zz, LANES - ROPE_HALF, 1) * sa + pltpu.roll(zz, ROPE_HALF, 1) * sb)
    return outs[0] if len(outs) == 1 else jnp.concatenate(outs, axis=1)


def _gelu(x):
    return 0.5 * x * (1.0 + lax.erf(x * (2.0 ** -0.5)))


def _fold_rows(x, op):
    parts = [x[a:a + SUBLANES] for a in range(0, x.shape[0], SUBLANES)]
    chains = parts[:FOLD_CHAINS]
    for a, part in enumerate(parts[FOLD_CHAINS:]):
        chains[a % FOLD_CHAINS] = op(chains[a % FOLD_CHAINS], part)
    while len(chains) > 1:
        chains = [op(chains[a], chains[a + 1]) if a + 1 < len(chains) else chains[a] for a in range(0, len(chains), 2)]
    return chains[0]


def _head_rows(x, hh):
    r = lax.broadcasted_iota(jnp.int32, (LANES, 1), 0)
    return jnp.where((r >= hh * HEAD_DIM) & (r < (hh + 1) * HEAD_DIM), x, jnp.zeros_like(x))


def _const_spec(shape):
    nd = len(shape)
    return pl.BlockSpec(shape, lambda *_: (0,) * nd, pipeline_mode=pl.Buffered(1))


def _params(n_axes, flags=None):
    return pltpu.CompilerParams(dimension_semantics=("arbitrary",) * n_axes, vmem_limit_bytes=VMEM_LIMIT, flags=flags)


def _stage_scores(score_fn, heads, s_ref, cm_ref):
    n_slots = s_ref.shape[0]
    for h in heads:
        s = score_fn(h)
        s_ref[h % n_slots] = s
        cm_ref[h % n_slots] = _fold_rows(s, jnp.maximum)


def _attend_chunk(n_heads, score_fn, next_score_fn, vt_fn, s_ref, cm_ref, m_ref, acc_ref):
    n_slots = s_ref.shape[0]
    ahead = n_slots - 1
    assert n_heads % n_slots == 0
    for h in range(n_heads):
        if h + ahead < n_heads:
            _stage_scores(score_fn, [h + ahead], s_ref, cm_ref)
        else:
            _stage_scores(next_score_fn, [h + ahead - n_heads], s_ref, cm_ref)
        slot = h % n_slots
        m_old = m_ref[h]
        m_new = jnp.maximum(m_old, jnp.max(cm_ref[slot], axis=0, keepdims=True))
        p = jnp.exp2(s_ref[slot] - m_new).astype(BF16)
        m_ref[h] = m_new
        vt = vt_fn(h)
        vt = jnp.concatenate([vt, jnp.ones((acc_ref.shape[1] - vt.shape[0], vt.shape[1]), BF16)], axis=0)
        acc_ref[h] = jnp.exp2(m_old - m_new) * acc_ref[h] + _dot(vt, p)


def _softmax_init(m_ref, acc_ref):
    m_ref[...] = jnp.full(m_ref.shape, -jnp.inf, F32)
    acc_ref[...] = jnp.zeros(acc_ref.shape, F32)


def _write_heads(o_ref, acc_ref):
    def head(h):
        return acc_ref[h, :HEAD_DIM, :] / acc_ref[h, HEAD_DIM:HEAD_DIM + 1, :]

    for p in range(acc_ref.shape[0] // 2):
        o_ref[:, p * LANES:(p + 1) * LANES] = jnp.concatenate([head(2 * p), head(2 * p + 1)], axis=0).T.astype(BF16)


def _front0_kernel(h_ref, g_ref, w_ref, c_ref, sa_ref, sb_ref, lng_ref, lnb_ref, ws_ref, bs_ref,
                   ya_ref, qt_ref, k_ref, vt_ref, km_ref):
    tm = h_ref.shape[0]
    xn = _rms(h_ref[...], g_ref[...]).astype(BF16)

    def proj(lo, hi):
        return _dot(xn, w_ref[:, lo:hi])

    u = _gelu(proj(0, WIDTH_A))
    v = _gelu(proj(WIDTH_A, 2 * WIDTH_A))
    lane = lax.broadcasted_iota(jnp.int32, (1, LANES), 1)
    low = lane < HEAD_DIM
    row = lax.broadcasted_iota(jnp.int32, (SGU_CHUNK, SGU_CHUNK), 0)
    col = lax.broadcasted_iota(jnp.int32, (SGU_CHUNK, SGU_CHUNK), 1)
    tril = col <= row
    for j in range(WIDTH_A // LANES):
        sl = slice(j * LANES, (j + 1) * LANES)
        vv = v[:, sl]

        def half_mean(t):
            s_lo = jnp.sum(jnp.where(low, t, 0.0), axis=-1, keepdims=True)
            s_hi = jnp.sum(jnp.where(low, 0.0, t), axis=-1, keepdims=True)
            return jnp.where(low, s_lo, s_hi) * (1.0 / HEAD_DIM)

        xc = vv - half_mean(vv)
        var = half_mean(xc * xc)
        vn = (xc * lax.rsqrt(var + EPS) * lng_ref[:, sl] + lnb_ref[:, sl]).astype(BF16)
        w_lo = jnp.where(tril, ws_ref[2 * j], 0.0).astype(BF16)
        w_hi = jnp.where(tril, ws_ref[2 * j + 1], 0.0).astype(BF16)
        for c in range(tm // SGU_CHUNK):
            rs = slice(c * SGU_CHUNK, (c + 1) * SGU_CHUNK)
            vc = vn[rs]
            mixed = jnp.where(low, _dot(w_lo, vc), _dot(w_hi, vc)) + bs_ref[:, sl]
            ya_ref[rs, sl] = (u[rs, sl] * mixed).astype(BF16)

    o = 2 * WIDTH_A
    c, sa, sb = c_ref[...], sa_ref[...], sb_ref[...]
    q = _rope(proj(o, o + WIDTH_B), c, sa, sb) * (HEAD_DIM ** -0.5 * LOG2E)
    qt_ref[0] = q.T.astype(BF16)
    k = _rope(proj(o + WIDTH_B, o + 2 * WIDTH_B), c, sa, sb)
    k_ref[...] = k.astype(BF16)
    v_b = proj(o + 2 * WIDTH_B, o + 3 * WIDTH_B)
    for blk in range(tm // MOBA_BLOCK):
        rs = slice(blk * MOBA_BLOCK, (blk + 1) * MOBA_BLOCK)
        km_ref[0, blk:blk + 1, :] = jnp.mean(k[rs], axis=0, keepdims=True)
        vt_ref[0, blk] = v_b[rs].T.astype(BF16)


def _front0(h, g, w, c, sa, sb, lng, lnb, ws, bs, batch, seq):
    t, d = h.shape
    tm = ROW_TILE
    nt = seq // tm
    nblk = tm // MOBA_BLOCK
    row = lambda n: pl.BlockSpec((tm, n), lambda i: (i, 0))
    return pl.pallas_call(
        _front0_kernel,
        grid=(t // tm,),
        in_specs=[row(d), _const_spec((1, d)), _const_spec(w.shape), row(LANES), row(LANES), row(LANES),
                  _const_spec(lng.shape), _const_spec(lnb.shape), _const_spec(ws.shape), _const_spec(bs.shape)],
        out_specs=[row(WIDTH_A),
                   pl.BlockSpec((1, WIDTH_B, tm), lambda i: (i // nt, 0, i % nt)),
                   row(WIDTH_B),
                   pl.BlockSpec((1, nblk, WIDTH_B, MOBA_BLOCK), lambda i: (i // nt, i % nt, 0, 0)),
                   pl.BlockSpec((1, nblk, WIDTH_B), lambda i: (i, 0, 0))],
        out_shape=[jax.ShapeDtypeStruct((t, WIDTH_A), BF16),
                   jax.ShapeDtypeStruct((batch, WIDTH_B, seq), BF16),
                   jax.ShapeDtypeStruct((t, WIDTH_B), BF16),
                   jax.ShapeDtypeStruct((batch, seq // MOBA_BLOCK, WIDTH_B, MOBA_BLOCK), BF16),
                   jax.ShapeDtypeStruct((t // tm, nblk, WIDTH_B), F32)],
        compiler_params=_params(1),
        name="front0",
    )(h, g, w, c, sa, sb, lng, lnb, ws, bs)


def _moba_kernel(qt_ref, k_ref, vt_ref, km_ref, oh_ref, o_ref, qc_ref, s_ref, cm_ref, m_ref, acc_ref):
    i = pl.program_id(1)
    tq = qt_ref.shape[2]
    nb = km_ref.shape[0]
    n_heads = acc_ref.shape[0]
    kc_n = s_ref.shape[1]
    per_chunk = kc_n // MOBA_BLOCK
    blk = lax.broadcasted_iota(jnp.int32, (nb, 1), 0)
    blk_f = blk.astype(F32)
    past = blk < i
    _softmax_init(m_ref, acc_ref)
    for h in range(n_heads):
        sl = slice((h // 2) * LANES, (h // 2 + 1) * LANES)
        qm = _head_rows(qt_ref[0, sl, :], h % 2)
        g = jnp.where(past, _dot(km_ref[:, sl].astype(BF16), qm), NEG)
        sel = jnp.zeros(g.shape, jnp.bool_)
        for _ in range(min(MOBA_TOPK, nb)):
            top = jnp.max(g, axis=0, keepdims=True)
            idx = jnp.min(jnp.where(g == top, blk_f, 1e9), axis=0, keepdims=True)
            pick = blk_f == idx
            sel = sel | pick
            g = jnp.where(pick, -jnp.inf, g)
        selb = jnp.where((sel & past) | (blk == i), 0.0, NEG).astype(BF16)
        qc_ref[h] = jnp.concatenate([qm, selb, jnp.zeros((LANES - nb, tq), BF16)], axis=0)

    def values(c):
        def vt(h):
            rows = slice(h * HEAD_DIM, (h + 1) * HEAD_DIM)
            return jnp.concatenate([vt_ref[0, per_chunk * c + a, rows, :] for a in range(per_chunk)], axis=1)
        return vt

    def chunk_scores(c, mask=None):
        st = pl.multiple_of(c * kc_n, kc_n)

        def scores(h):
            kc = jnp.concatenate([k_ref[pl.ds(st, kc_n), (h // 2) * LANES:(h // 2 + 1) * LANES],
                                  oh_ref[pl.ds(st, kc_n), :]], axis=1)
            s = _dot(kc, qc_ref[h])
            return s if mask is None else jnp.where(mask, s, NEG)

        return scores

    own = i // per_chunk
    kpos = own * kc_n + lax.broadcasted_iota(jnp.int32, (kc_n, 1), 0)
    qpos = i * tq + lax.broadcasted_iota(jnp.int32, (1, tq), 1)
    own_scores = chunk_scores(own, kpos <= qpos)
    refs = (s_ref, cm_ref, m_ref, acc_ref)
    last = jnp.maximum(own - 1, 0)
    _stage_scores(own_scores, range(s_ref.shape[0] - 1), s_ref, cm_ref)
    _attend_chunk(n_heads, own_scores, chunk_scores(0), values(own), *refs)

    def body(c, carry):
        _attend_chunk(n_heads, chunk_scores(c), chunk_scores(jnp.minimum(c + 1, last)), values(c), *refs)
        return carry

    lax.fori_loop(0, own, body, 0)
    _write_heads(o_ref, acc_ref)


def _moba(qt, k, vt, km, onehot, batch, seq):
    t, width = k.shape
    tq = Q_TILE
    assert tq == MOBA_BLOCK
    nb = seq // MOBA_BLOCK
    nq = seq // tq
    n_heads = width // HEAD_DIM
    one = pl.Buffered(1)
    return pl.pallas_call(
        _moba_kernel,
        grid=(batch, nq),
        in_specs=[pl.BlockSpec((1, width, tq), lambda b, i: (b, 0, i)),
                  pl.BlockSpec((seq, width), lambda b, i: (b, 0), pipeline_mode=one),
                  pl.BlockSpec((1, nb, width, MOBA_BLOCK), lambda b, i: (b, 0, 0, 0), pipeline_mode=one),
                  pl.BlockSpec((nb, width), lambda b, i: (b, 0)),
                  _const_spec(onehot.shape)],
        out_specs=pl.BlockSpec((tq, width), lambda b, i: (b * nq + i, 0)),
        out_shape=jax.ShapeDtypeStruct((t, width), BF16),
        scratch_shapes=[pltpu.VMEM((n_heads, 2 * LANES, tq), BF16), pltpu.VMEM((MOBA_SLOTS, MOBA_K_CHUNK, tq), F32),
                        pltpu.VMEM((MOBA_SLOTS, SUBLANES, tq), F32),
                        pltpu.VMEM((n_heads, 1, tq), F32), pltpu.VMEM((n_heads, HEAD_DIM + ONES_ROWS, tq), F32)],
        compiler_params=_params(2),
        name="moba",
    )(qt, k, vt, km, onehot)


def _post_kernel(*refs, n_mix, final):
    h_ref = refs[0]
    y_refs = refs[1:1 + n_mix]
    wo_refs = refs[1 + n_mix:1 + 2 * n_mix]
    (p_ref, gf_ref, wg_ref, wu_ref, wd_ref, gp_ref, wpg_ref, wpp_ref, gl_ref, o_ref, a_ref) = refs[1 + 2 * n_mix:]
    h = h_ref[...]
    for y_ref, wo_ref in zip(y_refs, wo_refs):
        h = h + _dot(y_ref[...], wo_ref[...])
    xn = _rms(h, gf_ref[...]).astype(BF16)
    d_ff = wg_ref.shape[1]
    step = 2 * LANES
    for lo in range(0, d_ff, step):
        gate = _dot(xn, wg_ref[:, lo:lo + step])
        up = _dot(xn, wu_ref[:, lo:lo + step])
        a_ref[:, lo:lo + step] = (gate * jax.nn.sigmoid(gate) * up).astype(BF16)
    h = h + _dot(a_ref[...], wd_ref[...])
    pg = jax.nn.sigmoid(_dot(_rms(h, gp_ref[...]).astype(BF16), wpg_ref[...]))
    h = h + _dot(p_ref[...].astype(BF16), wpp_ref[...]) * pg
    o_ref[...] = _rms(h, gl_ref[...]) if final else h


def _post(h, ys, wos, p, gf, wg, wu, wd, gp, wpg, wpp, gl, final):
    t, d = h.shape
    tm = ROW_TILE
    row = lambda n: pl.BlockSpec((tm, n), lambda i: (i, 0))
    consts = [gf, wg, wu, wd, gp, wpg, wpp, gl]
    return pl.pallas_call(
        functools.partial(_post_kernel, n_mix=len(ys), final=final),
        grid=(t // tm,),
        in_specs=[row(d)] + [row(y.shape[1]) for y in ys] + [_const_spec(w.shape) for w in wos]
                 + [row(p.shape[1])] + [_const_spec(c.shape) for c in consts],
        out_specs=row(d),
        out_shape=jax.ShapeDtypeStruct((t, d), F32),
        scratch_shapes=[pltpu.VMEM((tm, wg.shape[1]), BF16)],
        compiler_params=_params(1),
        name="post_final" if final else "post",
    )(h, *ys, *wos, p, *consts)


def _front1_kernel(h_ref, g_ref, w_ref, c_ref, sa_ref, sb_ref, lng_ref, lnb_ref,
                   qt_ref, k_ref, vt_ref, qit_ref, ki_ref, wit_ref):
    xn = _rms(h_ref[...], g_ref[...]).astype(BF16)

    def proj(lo, hi):
        return _dot(xn, w_ref[:, lo:hi])

    c, sa, sb = c_ref[...], sa_ref[...], sb_ref[...]
    qt_ref[0] = (_rope(proj(0, WIDTH_C), c, sa, sb) * (HEAD_DIM ** -0.5 * LOG2E)).T.astype(BF16)
    k_ref[...] = _rope(proj(WIDTH_C, 2 * WIDTH_C), c, sa, sb).astype(BF16)
    vt_ref[0, 0] = proj(2 * WIDTH_C, 3 * WIDTH_C).T.astype(BF16)
    o = 3 * WIDTH_C
    qit_ref[0] = _rope(proj(o, o + IDX_HEADS * IDX_DIM), c, sa, sb).T.astype(BF16)
    o += IDX_HEADS * IDX_DIM
    ki = proj(o, o + LANES)
    xc = ki - jnp.mean(ki, axis=-1, keepdims=True)
    var = jnp.mean(xc * xc, axis=-1, keepdims=True)
    ki = xc * lax.rsqrt(var + EPS) * lng_ref[...] + lnb_ref[...]
    ki_ref[...] = _rope(ki, c, sa, sb).astype(BF16)
    wi = proj(o + LANES, o + 2 * LANES) * ((IDX_HEADS ** -0.5) * (IDX_DIM ** -0.5))
    wit_ref[0] = wi.T[:IDX_HEADS]


def _front1(h, g, w, c, sa, sb, lng, lnb, batch, seq):
    t, d = h.shape
    tm = ROW_TILE
    assert tm == DSA_K_CHUNK
    nt = seq // tm
    row = lambda n: pl.BlockSpec((tm, n), lambda i: (i, 0))
    tr = lambda n: pl.BlockSpec((1, n, tm), lambda i: (i // nt, 0, i % nt))
    return pl.pallas_call(
        _front1_kernel,
        grid=(t // tm,),
        in_specs=[row(d), _const_spec((1, d)), _const_spec(w.shape), row(LANES), row(LANES), row(LANES),
                  _const_spec(lng.shape), _const_spec(lnb.shape)],
        out_specs=[tr(WIDTH_C), row(WIDTH_C),
                   pl.BlockSpec((1, 1, WIDTH_C, tm), lambda i: (i // nt, i % nt, 0, 0)),
                   tr(IDX_HEADS * IDX_DIM), row(LANES), tr(IDX_HEADS)],
        out_shape=[jax.ShapeDtypeStruct((batch, WIDTH_C, seq), BF16),
                   jax.ShapeDtypeStruct((t, WIDTH_C), BF16),
                   jax.ShapeDtypeStruct((batch, nt, WIDTH_C, tm), BF16),
                   jax.ShapeDtypeStruct((batch, IDX_HEADS * IDX_DIM, seq), BF16),
                   jax.ShapeDtypeStruct((t, LANES), BF16),
                   jax.ShapeDtypeStruct((batch, IDX_HEADS, seq), F32)],
        compiler_params=_params(1),
        name="front1",
    )(h, g, w, c, sa, sb, lng, lnb)


def _dsa_index_kernel(qit_ref, wit_ref, ki_ref, b_ref, sc_ref, *, n_sel):
    i = pl.program_id(1)
    tq = qit_ref.shape[2]
    kc_n = DSA_K_CHUNK
    n_chunks_total = sc_ref.shape[0]
    n_chunks = (i * tq + tq + kc_n - 1) // kc_n
    qpos = i * tq + lax.broadcasted_iota(jnp.int32, (1, tq), 1)
    kofs = lax.broadcasted_iota(jnp.int32, (kc_n, 1), 0)
    wv = wit_ref[0]

    def causal(c):
        return (c * kc_n + kofs) <= qpos

    big = 3e38

    def score_chunk(c, carry):
        mx, mn = carry
        kc = ki_ref[pl.ds(pl.multiple_of(c * kc_n, kc_n), kc_n), :]
        acc = jnp.zeros((kc_n, tq), F32)
        for h in range(IDX_HEADS):
            qm = _head_rows(qit_ref[0, (h // 2) * LANES:(h // 2 + 1) * LANES, :], h % 2)
            acc = acc + jnp.maximum(_dot(kc, qm), 0.0) * wv[h:h + 1, :]
        ok = causal(c)
        sc_ref[c] = jnp.where(ok, acc, NEG)
        return (jnp.maximum(mx, _fold_rows(jnp.where(ok, acc, -big), jnp.maximum)),
                jnp.minimum(mn, _fold_rows(jnp.where(ok, acc, big), jnp.minimum)))

    mx, mn = lax.fori_loop(0, n_chunks, score_chunk,
                           (jnp.full((SUBLANES, tq), -big, F32), jnp.full((SUBLANES, tq), big, F32)))
    col_max = jnp.max(mx, axis=0, keepdims=True)
    col_min = jnp.min(mn, axis=0, keepdims=True)

    def reduce_chunks(fn, op, init):
        def body(c, acc):
            return op(acc, _fold_rows(fn(sc_ref[c]), op))
        return lax.fori_loop(0, n_chunks, body, jnp.full((SUBLANES, tq), init, F32))

    def count_ge(thr):
        acc = reduce_chunks(lambda x: jnp.where(x >= thr, 1.0, 0.0), jnp.add, 0.0)
        return jnp.sum(acc, axis=0, keepdims=True)

    def pending(flags):
        return jnp.max(jnp.where(flags, 0.0, 1.0))

    k_f = float(n_sel)
    n_causal = (qpos + 1).astype(F32)
    small = n_causal < k_f
    lo = jnp.where(small, NEG, col_min)
    hi = col_max + jnp.maximum(jnp.maximum(col_max - col_min, jnp.abs(col_max) * 1e-3), 1e-30)
    c_lo = jnp.where(small, 2.0 * k_f, n_causal)
    c_hi = jnp.zeros((1, tq), F32)

    def bisect(_, carry):
        lo, hi, c_lo, c_hi = carry
        mid = 0.5 * (lo + hi)
        cnt = count_ge(mid)
        ok = cnt >= k_f
        return (jnp.where(ok, mid, lo), jnp.where(ok, hi, mid), jnp.where(ok, cnt, c_lo), jnp.where(ok, c_hi, cnt))

    def bracket_ends(lo, hi):
        def body(c, carry):
            below, above = carry
            x = sc_ref[c]
            return (jnp.maximum(below, _fold_rows(jnp.where(x < hi, x, -big), jnp.maximum)),
                    jnp.minimum(above, _fold_rows(jnp.where(x >= lo, x, big), jnp.minimum)))

        below, above = lax.fori_loop(0, n_chunks, body, (jnp.full((SUBLANES, tq), -big, F32),
                                                         jnp.full((SUBLANES, tq), big, F32)))
        return jnp.max(below, axis=0, keepdims=True), jnp.min(above, axis=0, keepdims=True)

    def settled(c_lo, c_hi, below, above):
        return small | (c_hi == k_f - 1.0) | (c_lo == k_f) | (below == above)

    def narrow_cond(carry):
        return (carry[6] < BISECT_ROUNDS) & (carry[7] > 0.5)

    def narrow(carry):
        state = lax.fori_loop(0, BISECT_PER_ROUND, bisect, carry[:4])
        below, above = bracket_ends(state[0], state[1])
        return (*state, below, above, carry[6] + 1, pending(settled(state[2], state[3], below, above)))

    state = lax.fori_loop(0, BISECT_WARMUP, bisect, (lo, hi, c_lo, c_hi))
    zeros = jnp.zeros((1, tq), F32)
    lo, hi, c_lo, c_hi, below, above, _, _ = lax.while_loop(
        narrow_cond, narrow, (*state, zeros, zeros, jnp.int32(0), jnp.float32(1.0)))
    thr = jnp.where(small, NEG, jnp.where(c_lo == k_f, above, below))
    done = jnp.where(settled(c_lo, c_hi, below, above), 1.0, 0.0)

    def snap_cond(carry):
        return carry[3] > 0.5

    def snap(carry):
        hi, thr, done, _ = carry
        cand = jnp.max(reduce_chunks(lambda x: jnp.where(x < hi, x, -big), jnp.maximum, -big),
                       axis=0, keepdims=True)
        ok = count_ge(cand) >= k_f
        thr = jnp.where(done > 0.5, thr, cand)
        done = jnp.where(ok, 1.0, done)
        hi = jnp.where(done > 0.5, hi, cand)
        return hi, thr, done, jnp.max(1.0 - done)

    _, thr, _, _ = lax.while_loop(snap_cond, snap, (hi, thr, done, jnp.max(1.0 - done)))

    n_ge = count_ge(thr)
    tied = jnp.max(jnp.where((n_ge > k_f) & jnp.logical_not(small), 1.0, 0.0))

    @pl.when(tied < 0.5)
    def _():
        def body(c, carry):
            x = sc_ref[c]
            b_ref[0, 0, c] = jnp.where((x >= thr) & causal(c), 0.0, NEG)
            return carry
        lax.fori_loop(0, n_chunks, body, 0)

    @pl.when(tied > 0.5)
    def _():
        r = lax.broadcasted_iota(jnp.int32, (kc_n, kc_n), 0)
        cc = lax.broadcasted_iota(jnp.int32, (kc_n, kc_n), 1)
        prefix = jnp.where(cc <= r, 1.0, 0.0).astype(BF16)
        n_gt = reduce_chunks(lambda x: jnp.where(x > thr, 1.0, 0.0), jnp.add, 0.0)
        need = k_f - jnp.sum(n_gt, axis=0, keepdims=True)

        def body(c, seen):
            x = sc_ref[c]
            eq = x == thr
            cnt = seen + _dot(prefix, jnp.where(eq, 1.0, 0.0).astype(BF16))
            b_ref[0, 0, c] = jnp.where(((x > thr) | (eq & (cnt <= need))) & causal(c), 0.0, NEG)
            return cnt[kc_n - 1:kc_n, :]
        lax.fori_loop(0, n_chunks, body, jnp.zeros((1, tq), F32))

    def fill(c, carry):
        b_ref[0, 0, c] = jnp.full((kc_n, tq), NEG, F32)
        return carry

    lax.fori_loop(n_chunks, n_chunks_total, fill, 0)


def _dsa_index(qit, wit, ki, batch, seq):
    tq = Q_TILE
    nq = seq // tq
    nch = seq // DSA_K_CHUNK
    n_sel = min(DSA_TOPK, seq // 4)
    return pl.pallas_call(
        functools.partial(_dsa_index_kernel, n_sel=n_sel),
        grid=(batch, nq),
        in_specs=[pl.BlockSpec((1, qit.shape[1], tq), lambda b, i: (b, 0, i)),
                  pl.BlockSpec((1, IDX_HEADS, tq), lambda b, i: (b, 0, i)),
                  pl.BlockSpec((seq, LANES), lambda b, i: (b, 0))],
        out_specs=pl.BlockSpec((1, 1, nch, DSA_K_CHUNK, tq), lambda b, i: (b, i, 0, 0, 0)),
        out_shape=jax.ShapeDtypeStruct((batch, nq, nch, DSA_K_CHUNK, tq), F32),
        scratch_shapes=[pltpu.VMEM((nch, DSA_K_CHUNK, tq), F32)],
        compiler_params=_params(2),
        name="dsa_index",
    )(qit, wit, ki)


def _dsa_attn_kernel(qt_ref, k_ref, vt_ref, b_ref, o_ref, qm_ref, s_ref, cm_ref, m_ref, acc_ref):
    i = pl.program_id(1)
    tq = qt_ref.shape[2]
    kc_n = DSA_K_CHUNK
    n_heads = acc_ref.shape[0]
    n_chunks = (i * tq + tq + kc_n - 1) // kc_n
    _softmax_init(m_ref, acc_ref)
    for h in range(n_heads):
        qm_ref[h] = _head_rows(qt_ref[0, (h // 2) * LANES:(h // 2 + 1) * LANES, :], h % 2)

    def chunk_scores(c):
        st = pl.multiple_of(c * kc_n, kc_n)

        def scores(h):
            return _dot(k_ref[pl.ds(st, kc_n), (h // 2) * LANES:(h // 2 + 1) * LANES], qm_ref[h]) + b_ref[0, 0, c]

        return scores

    _stage_scores(chunk_scores(0), range(s_ref.shape[0] - 1), s_ref, cm_ref)

    def body(c, carry):
        def values(h):
            return vt_ref[0, c, h * HEAD_DIM:(h + 1) * HEAD_DIM, :]

        _attend_chunk(n_heads, chunk_scores(c), chunk_scores(jnp.minimum(c + 1, n_chunks - 1)), values,
                      s_ref, cm_ref, m_ref, acc_ref)
        return carry

    lax.fori_loop(0, n_chunks, body, 0)
    _write_heads(o_ref, acc_ref)


def _dsa_attn(qt, k, vt, bias, batch, seq):
    t, width = k.shape
    tq = Q_TILE
    nq = seq // tq
    n_heads = width // HEAD_DIM
    one = pl.Buffered(1)
    return pl.pallas_call(
        _dsa_attn_kernel,
        grid=(batch, nq),
        in_specs=[pl.BlockSpec((1, width, tq), lambda b, i: (b, 0, i)),
                  pl.BlockSpec((seq, width), lambda b, i: (b, 0), pipeline_mode=one),
                  pl.BlockSpec((1,) + vt.shape[1:], lambda b, i: (b, 0, 0, 0), pipeline_mode=one),
                  pl.BlockSpec((1, 1) + bias.shape[2:], lambda b, i: (b, i, 0, 0, 0))],
        out_specs=pl.BlockSpec((tq, width), lambda b, i: (b * nq + i, 0)),
        out_shape=jax.ShapeDtypeStruct((t, width), BF16),
        scratch_shapes=[pltpu.VMEM((n_heads, LANES, tq), BF16), pltpu.VMEM((SCORE_SLOTS, DSA_K_CHUNK, tq), F32),
                        pltpu.VMEM((SCORE_SLOTS, SUBLANES, tq), F32),
                        pltpu.VMEM((n_heads, 1, tq), F32), pltpu.VMEM((n_heads, HEAD_DIM + ONES_ROWS, tq), F32)],
        compiler_params=_params(2),
        name="dsa_attn",
    )(qt, k, vt, bias)


def _rope_tables(positions):
    inv_freq = ROPE_THETA ** (-jnp.arange(0, ROPE_DIM, 2, dtype=F32) / ROPE_DIM)
    ang = positions.astype(F32).reshape(-1, 1) * inv_freq
    cos, sin = jnp.cos(ang), jnp.sin(ang)
    t = cos.shape[0]
    rest = HEAD_DIM - ROPE_DIM
    c = jnp.concatenate([cos, cos, jnp.ones((t, rest), F32)], axis=1)
    sa = jnp.concatenate([-sin, jnp.zeros((t, HEAD_DIM - ROPE_HALF), F32)], axis=1)
    sb = jnp.concatenate([jnp.zeros((t, ROPE_HALF), F32), sin, jnp.zeros((t, rest), F32)], axis=1)
    return tuple(jnp.tile(x, (1, LANES // HEAD_DIM)) for x in (c, sa, sb))


def kernel(x, p, positions, g_mix, w_in_even, a_ln_g, a_ln_b, a_w_s, a_b_s, w_out_even, w_in_odd, c_kidx_ln_g, c_kidx_ln_b, w_out_odd, g_ffn, w_ffn_gate, w_ffn_up, w_ffn_down, g_ple, w_ple_proj, w_ple_gate, g_final):
    batch, seq, d = x.shape
    depth = p.shape[0]
    t = batch * seq
    assert seq % ROW_TILE == 0 and seq % DSA_K_CHUNK == 0 and min(DSA_TOPK, seq // 4) <= DSA_K_CHUNK
    c, sa, sb = _rope_tables(positions)
    h = x.reshape(t, d)
    p2 = p.reshape(depth, t, p.shape[-1])
    row = lambda a: a.reshape(1, -1)
    onehot = (jnp.arange(seq)[:, None] // MOBA_BLOCK == jnp.arange(LANES)[None, :]).astype(BF16)
    for i in range(depth):
        j = i // 2
        if i % 2 == 0:
            bs = jnp.repeat(a_b_s[j].T, HEAD_DIM, axis=1)
            ya, qt, k, vt, km = _front0(h, row(g_mix[i]), w_in_even[j].astype(BF16), c, sa, sb,
                                        row(a_ln_g[j]), row(a_ln_b[j]), a_w_s[j], bs, batch, seq)
            yb = _moba(qt, k, vt, km.reshape(t // MOBA_BLOCK, WIDTH_B), onehot, batch, seq)
            wo = w_out_even[j].astype(BF16)
            ys, wos = [ya, yb], [wo[:WIDTH_A], wo[WIDTH_A:]]
        else:
            w = w_in_odd[j]
            o = 3 * WIDTH_C + IDX_HEADS * IDX_DIM
            w1 = jnp.concatenate([w[:, :o], w[:, o:o + IDX_DIM], w[:, o:o + IDX_DIM], w[:, o + IDX_DIM:],
                                  jnp.zeros((d, LANES - IDX_HEADS), w.dtype)], axis=1).astype(BF16)
            qt, k, vt, qit, ki, wit = _front1(h, row(g_mix[i]), w1, c, sa, sb, row(jnp.tile(c_kidx_ln_g[j], 2)),
                                              row(jnp.tile(c_kidx_ln_b[j], 2)), batch, seq)
            bias = _dsa_index(qit, wit, ki, batch, seq)
            ys, wos = [_dsa_attn(qt, k, vt, bias, batch, seq)], [w_out_odd[j].astype(BF16)]
        h = _post(h, ys, wos, p2[i], row(g_ffn[i]), w_ffn_gate[i].astype(BF16), w_ffn_up[i].astype(BF16),
                  w_ffn_down[i].astype(BF16), row(g_ple[i]), w_ple_gate[i].astype(BF16),
                  w_ple_proj[i].astype(BF16), row(g_final), final=(i == depth - 1))
    return h.reshape(batch, seq, d)
```

```python
import functools

import jax
import jax.numpy as jnp
from jax import lax
from jax.experimental import pallas as pl
from jax.experimental.pallas import tpu as pltpu

HEAD_DIM = 64
ROPE_DIM = HEAD_DIM // 4
ROPE_HALF = ROPE_DIM // 2
ROPE_THETA = 500000.0
N_GROUPS_A = 8
WIDTH_A = N_GROUPS_A * HEAD_DIM
N_HEADS_B = 8
WIDTH_B = N_HEADS_B * HEAD_DIM
N_HEADS_C = 16
WIDTH_C = N_HEADS_C * HEAD_DIM
SGU_CHUNK = 128
MOBA_BLOCK = 256
MOBA_TOPK = 3
IDX_HEADS = 8
IDX_DIM = 64
DSA_TOPK = 256
EPS = 1e-6
NEG = -1e30
LOG2E = 1.4426950408889634

LANES = 128
SUBLANES = 8
ROW_TILE = 512
Q_TILE = 256
DSA_K_CHUNK = 512
FOLD_CHAINS = 8
ONES_ROWS = 16
SCORE_SLOTS = 4
MOBA_SLOTS = 4
MOBA_K_CHUNK = 2 * MOBA_BLOCK
COARSE_STEPS = 16
BISECT_WARMUP = 0
BISECT_PER_ROUND = 4
BISECT_ROUNDS = 12
VMEM_LIMIT = 56 * 1024 * 1024

F32 = jnp.float32
BF16 = jnp.bfloat16


def _dot(a, b):
    return jnp.dot(a, b, preferred_element_type=F32)


def _rms(x, g):
    return x * lax.rsqrt(jnp.mean(x * x, axis=-1, keepdims=True) + EPS) * g


def _rope(z, c, sa, sb):
    outs = []
    for j in range(z.shape[1] // LANES):
        zz = z[:, j * LANES:(j + 1) * LANES]
        outs.append(zz * c + pltpu.roll(zz, LANES - ROPE_HALF, 1) * sa + pltpu.roll(zz, ROPE_HALF, 1) * sb)
    return outs[0] if len(outs) == 1 else jnp.concatenate(outs, axis=1)


def _gelu(x):
    return 0.5 * x * (1.0 + lax.erf(x * (2.0 ** -0.5)))


def _fold_rows(x, op, group=SUBLANES):
    parts = [x[a:a + group] for a in range(0, x.shape[0], group)]
    chains = parts[:FOLD_CHAINS]
    for a, part in enumerate(parts[FOLD_CHAINS:]):
        chains[a % FOLD_CHAINS] = op(chains[a % FOLD_CHAINS], part)
    while len(chains) > 1:
        chains = [op(chains[a], chains[a + 1]) if a + 1 < len(chains) else chains[a] for a in range(0, len(chains), 2)]
    return chains[0]


def _head_rows(x, hh):
    r = lax.broadcasted_iota(jnp.int32, (LANES, 1), 0)
    return jnp.where((r >= hh * HEAD_DIM) & (r < (hh + 1) * HEAD_DIM), x, jnp.zeros_like(x))


def _const_spec(shape):
    nd = len(shape)
    return pl.BlockSpec(shape, lambda *_: (0,) * nd, pipeline_mode=pl.Buffered(1))


def _params(n_axes, flags=None):
    return pltpu.CompilerParams(dimension_semantics=("arbitrary",) * n_axes, vmem_limit_bytes=VMEM_LIMIT, flags=flags)


def _stage_scores(score_fn, heads, s_ref, cm_ref):
    n_slots = s_ref.shape[0]
    for h in heads:
        s = score_fn(h)
        s_ref[h % n_slots] = s
        cm_ref[h % n_slots] = _fold_rows(s, jnp.maximum)


def _attend_chunk(n_heads, score_fn, next_score_fn, vt_fn, s_ref, cm_ref, m_ref, acc_ref):
    n_slots = s_ref.shape[0]
    ahead = n_slots - 1
    assert n_heads % n_slots == 0
    for h in range(n_heads):
        if h + ahead < n_heads:
            _stage_scores(score_fn, [h + ahead], s_ref, cm_ref)
        else:
            _stage_scores(next_score_fn, [h + ahead - n_heads], s_ref, cm_ref)
        slot = h % n_slots
        m_old = m_ref[h]
        m_new = jnp.maximum(m_old, jnp.max(cm_ref[slot], axis=0, keepdims=True))
        p = jnp.exp2(s_ref[slot] - m_new).astype(BF16)
        m_ref[h] = m_new
        vt = vt_fn(h)
        vt = jnp.concatenate([vt, jnp.ones((acc_ref.shape[1] - vt.shape[0], vt.shape[1]), BF16)], axis=0)
        acc_ref[h] = jnp.exp2(m_old - m_new) * acc_ref[h] + _dot(vt, p)


def _softmax_init(m_ref, acc_ref):
    m_ref[...] = jnp.full(m_ref.shape, -jnp.inf, F32)
    acc_ref[...] = jnp.zeros(acc_ref.shape, F32)


def _write_heads(o_ref, acc_ref):
    def head(h):
        return acc_ref[h, :HEAD_DIM, :] / acc_ref[h, HEAD_DIM:HEAD_DIM + 1, :]

    for p in range(acc_ref.shape[0] // 2):
        o_ref[:, p * LANES:(p + 1) * LANES] = jnp.concatenate([head(2 * p), head(2 * p + 1)], axis=0).T.astype(BF16)


def _front0_kernel(h_ref, g_ref, w_ref, c_ref, sa_ref, sb_ref, lng_ref, lnb_ref, ws_ref, bs_ref,
                   ya_ref, qt_ref, k_ref, vt_ref, km_ref):
    tm = h_ref.shape[0]
    xn = _rms(h_ref[...], g_ref[...]).astype(BF16)

    def proj(lo, hi):
        return _dot(xn, w_ref[:, lo:hi])

    u = _gelu(proj(0, WIDTH_A))
    v = _gelu(proj(WIDTH_A, 2 * WIDTH_A))
    lane = lax.broadcasted_iota(jnp.int32, (1, LANES), 1)
    low = lane < HEAD_DIM
    row = lax.broadcasted_iota(jnp.int32, (SGU_CHUNK, SGU_CHUNK), 0)
    col = lax.broadcasted_iota(jnp.int32, (SGU_CHUNK, SGU_CHUNK), 1)
    tril = col <= row
    for j in range(WIDTH_A // LANES):
        sl = slice(j * LANES, (j + 1) * LANES)
        vv = v[:, sl]

        def half_mean(t):
            s_lo = jnp.sum(jnp.where(low, t, 0.0), axis=-1, keepdims=True)
            s_hi = jnp.sum(jnp.where(low, 0.0, t), axis=-1, keepdims=True)
            return jnp.where(low, s_lo, s_hi) * (1.0 / HEAD_DIM)

        xc = vv - half_mean(vv)
        var = half_mean(xc * xc)
        vn = (xc * lax.rsqrt(var + EPS) * lng_ref[:, sl] + lnb_ref[:, sl]).astype(BF16)
        w_lo = jnp.where(tril, ws_ref[2 * j], 0.0).astype(BF16)
        w_hi = jnp.where(tril, ws_ref[2 * j + 1], 0.0).astype(BF16)
        for c in range(tm // SGU_CHUNK):
            rs = slice(c * SGU_CHUNK, (c + 1) * SGU_CHUNK)
            vc = vn[rs]
            mixed = jnp.where(low, _dot(w_lo, vc), _dot(w_hi, vc)) + bs_ref[:, sl]
            ya_ref[rs, sl] = (u[rs, sl] * mixed).astype(BF16)

    o = 2 * WIDTH_A
    c, sa, sb = c_ref[...], sa_ref[...], sb_ref[...]
    q = _rope(proj(o, o + WIDTH_B), c, sa, sb) * (HEAD_DIM ** -0.5 * LOG2E)
    qt_ref[0] = q.T.astype(BF16)
    k = _rope(proj(o + WIDTH_B, o + 2 * WIDTH_B), c, sa, sb)
    k_ref[...] = k.astype(BF16)
    v_b = proj(o + 2 * WIDTH_B, o + 3 * WIDTH_B)
    for blk in range(tm // MOBA_BLOCK):
        rs = slice(blk * MOBA_BLOCK, (blk + 1) * MOBA_BLOCK)
        km_ref[0, blk:blk + 1, :] = jnp.mean(k[rs], axis=0, keepdims=True)
        vt_ref[0, blk] = v_b[rs].T.astype(BF16)


def _front0(h, g, w, c, sa, sb, lng, lnb, ws, bs, batch, seq):
    t, d = h.shape
    tm = ROW_TILE
    nt = seq // tm
    nblk = tm // MOBA_BLOCK
    row = lambda n: pl.BlockSpec((tm, n), lambda i: (i, 0))
    return pl.pallas_call(
        _front0_kernel,
        grid=(t // tm,),
        in_specs=[row(d), _const_spec((1, d)), _const_spec(w.shape), row(LANES), row(LANES), row(LANES),
                  _const_spec(lng.shape), _const_spec(lnb.shape), _const_spec(ws.shape), _const_spec(bs.shape)],
        out_specs=[row(WIDTH_A),
                   pl.BlockSpec((1, WIDTH_B, tm), lambda i: (i // nt, 0, i % nt)),
                   row(WIDTH_B),
                   pl.BlockSpec((1, nblk, WIDTH_B, MOBA_BLOCK), lambda i: (i // nt, i % nt, 0, 0)),
                   pl.BlockSpec((1, nblk, WIDTH_B), lambda i: (i, 0, 0))],
        out_shape=[jax.ShapeDtypeStruct((t, WIDTH_A), BF16),
                   jax.ShapeDtypeStruct((batch, WIDTH_B, seq), BF16),
                   jax.ShapeDtypeStruct((t, WIDTH_B), BF16),
                   jax.ShapeDtypeStruct((batch, seq // MOBA_BLOCK, WIDTH_B, MOBA_BLOCK), BF16),
                   jax.ShapeDtypeStruct((t // tm, nblk, WIDTH_B), F32)],
        compiler_params=_params(1),
        name="front0",
    )(h, g, w, c, sa, sb, lng, lnb, ws, bs)


def _moba_kernel(qt_ref, k_ref, vt_ref, km_ref, oh_ref, o_ref, qc_ref, s_ref, cm_ref, m_ref, acc_ref):
    i = pl.program_id(1)
    tq = qt_ref.shape[2]
    nb = km_ref.shape[0]
    n_heads = acc_ref.shape[0]
    kc_n = s_ref.shape[1]
    per_chunk = kc_n // MOBA_BLOCK
    blk = lax.broadcasted_iota(jnp.int32, (nb, 1), 0)
    blk_f = blk.astype(F32)
    past = blk < i
    _softmax_init(m_ref, acc_ref)
    for h in range(n_heads):
        sl = slice((h // 2) * LANES, (h // 2 + 1) * LANES)
        qm = _head_rows(qt_ref[0, sl, :], h % 2)
        g = jnp.where(past, _dot(km_ref[:, sl].astype(BF16), qm), NEG)
        sel = jnp.zeros(g.shape, jnp.bool_)
        for _ in range(min(MOBA_TOPK, nb)):
            top = jnp.max(g, axis=0, keepdims=True)
            idx = jnp.min(jnp.where(g == top, blk_f, 1e9), axis=0, keepdims=True)
            pick = blk_f == idx
            sel = sel | pick
            g = jnp.where(pick, -jnp.inf, g)
        selb = jnp.where((sel & past) | (blk == i), 0.0, NEG).astype(BF16)
        qc_ref[h] = jnp.concatenate([qm, selb, jnp.zeros((LANES - nb, tq), BF16)], axis=0)

    def values(c):
        def vt(h):
            rows = slice(h * HEAD_DIM, (h + 1) * HEAD_DIM)
            return jnp.concatenate([vt_ref[0, per_chunk * c + a, rows, :] for a in range(per_chunk)], axis=1)
        return vt

    def chunk_scores(c, mask=None):
        st = pl.multiple_of(c * kc_n, kc_n)

        def scores(h):
            kc = jnp.concatenate([k_ref[pl.ds(st, kc_n), (h // 2) * LANES:(h // 2 + 1) * LANES],
                                  oh_ref[pl.ds(st, kc_n), :]], axis=1)
            s = _dot(kc, qc_ref[h])
            return s if mask is None else jnp.where(mask, s, NEG)

        return scores

    own = i // per_chunk
    kpos = own * kc_n + lax.broadcasted_iota(jnp.int32, (kc_n, 1), 0)
    qpos = i * tq + lax.broadcasted_iota(jnp.int32, (1, tq), 1)
    own_scores = chunk_scores(own, kpos <= qpos)
    refs = (s_ref, cm_ref, m_ref, acc_ref)
    last = jnp.maximum(own - 1, 0)
    _stage_scores(own_scores, range(s_ref.shape[0] - 1), s_ref, cm_ref)
    _attend_chunk(n_heads, own_scores, chunk_scores(0), values(own), *refs)

    def body(c, carry):
        _attend_chunk(n_heads, chunk_scores(c), chunk_scores(jnp.minimum(c + 1, last)), values(c), *refs)
        return carry

    lax.fori_loop(0, own, body, 0)
    _write_heads(o_ref, acc_ref)


def _moba(qt, k, vt, km, onehot, batch, seq):
    t, width = k.shape
    tq = Q_TILE
    assert tq == MOBA_BLOCK
    nb = seq // MOBA_BLOCK
    nq = seq // tq
    n_heads = width // HEAD_DIM
    one = pl.Buffered(1)
    return pl.pallas_call(
        _moba_kernel,
        grid=(batch, nq),
        in_specs=[pl.BlockSpec((1, width, tq), lambda b, i: (b, 0, i)),
                  pl.BlockSpec((seq, width), lambda b, i: (b, 0), pipeline_mode=one),
                  pl.BlockSpec((1, nb, width, MOBA_BLOCK), lambda b, i: (b, 0, 0, 0), pipeline_mode=one),
                  pl.BlockSpec((nb, width), lambda b, i: (b, 0)),
                  _const_spec(onehot.shape)],
        out_specs=pl.BlockSpec((tq, width), lambda b, i: (b * nq + i, 0)),
        out_shape=jax.ShapeDtypeStruct((t, width), BF16),
        scratch_shapes=[pltpu.VMEM((n_heads, 2 * LANES, tq), BF16), pltpu.VMEM((MOBA_SLOTS, MOBA_K_CHUNK, tq), F32),
                        pltpu.VMEM((MOBA_SLOTS, SUBLANES, tq), F32),
                        pltpu.VMEM((n_heads, 1, tq), F32), pltpu.VMEM((n_heads, HEAD_DIM + ONES_ROWS, tq), F32)],
        compiler_params=_params(2),
        name="moba",
    )(qt, k, vt, km, onehot)


def _post_kernel(*refs, n_mix, final):
    h_ref = refs[0]
    y_refs = refs[1:1 + n_mix]
    (wo_ref, p_ref, gf_ref, wg_ref, wu_ref, wd_ref, gp_ref, wpg_ref, wpp_ref, gl_ref, o_ref, a_ref) = refs[1 + n_mix:]
    h = h_ref[...]
    lo = 0
    for y_ref in y_refs:
        h = h + _dot(y_ref[...], wo_ref[0, lo:lo + y_ref.shape[1], :])
        lo += y_ref.shape[1]
    xn = _rms(h, gf_ref[...]).astype(BF16)
    d_ff = wg_ref.shape[2]
    step = 2 * LANES
    for lo in range(0, d_ff, step):
        gate = _dot(xn, wg_ref[0, :, lo:lo + step])
        up = _dot(xn, wu_ref[0, :, lo:lo + step])
        a_ref[:, lo:lo + step] = (gate * jax.nn.sigmoid(gate) * up).astype(BF16)
    h = h + _dot(a_ref[...], wd_ref[0])
    pg = jax.nn.sigmoid(_dot(_rms(h, gp_ref[...]).astype(BF16), wpg_ref[0]))
    h = h + _dot(p_ref[0].astype(BF16), wpp_ref[0]) * pg
    o_ref[...] = _rms(h, gl_ref[...]) if final else h


def _layer_spec(shape, layer):
    rest = (0,) * (len(shape) - 1)
    return pl.BlockSpec((1,) + tuple(shape[1:]), lambda *_: (layer,) + rest, pipeline_mode=pl.Buffered(1))


def _post(h, ys, wo, wo_layer, p, layer, gf, wg, wu, wd, gp, wpg, wpp, gl, final):
    t, d = h.shape
    tm = ROW_TILE
    row = lambda n: pl.BlockSpec((tm, n), lambda i: (i, 0))
    stack = lambda w: _layer_spec(w.shape, layer)
    return pl.pallas_call(
        functools.partial(_post_kernel, n_mix=len(ys), final=final),
        grid=(t // tm,),
        in_specs=[row(d)] + [row(y.shape[1]) for y in ys]
                 + [_layer_spec(wo.shape, wo_layer), pl.BlockSpec((1, tm, p.shape[2]), lambda i: (layer, i, 0)),
                    _const_spec(gf.shape), stack(wg), stack(wu), stack(wd), _const_spec(gp.shape), stack(wpg),
                    stack(wpp), _const_spec(gl.shape)],
        out_specs=row(d),
        out_shape=jax.ShapeDtypeStruct((t, d), F32),
        scratch_shapes=[pltpu.VMEM((tm, wg.shape[2]), BF16)],
        compiler_params=_params(1),
        name="post_final" if final else "post",
    )(h, *ys, wo, p, gf, wg, wu, wd, gp, wpg, wpp, gl)


def _front1_kernel(h_ref, g_ref, w_ref, c_ref, sa_ref, sb_ref, lng_ref, lnb_ref,
                   qt_ref, k_ref, vt_ref, qit_ref, ki_ref, wit_ref):
    xn = _rms(h_ref[...], g_ref[...]).astype(BF16)

    def proj(lo, hi):
        return _dot(xn, w_ref[:, lo:hi])

    c, sa, sb = c_ref[...], sa_ref[...], sb_ref[...]
    qt_ref[0] = (_rope(proj(0, WIDTH_C), c, sa, sb) * (HEAD_DIM ** -0.5 * LOG2E)).T.astype(BF16)
    k_ref[...] = _rope(proj(WIDTH_C, 2 * WIDTH_C), c, sa, sb).astype(BF16)
    vt_ref[0, 0] = proj(2 * WIDTH_C, 3 * WIDTH_C).T.astype(BF16)
    o = 3 * WIDTH_C
    qit_ref[0] = _rope(proj(o, o + IDX_HEADS * IDX_DIM), c, sa, sb).T.astype(BF16)
    o += IDX_HEADS * IDX_DIM
    ki = proj(o, o + LANES)
    xc = ki - jnp.mean(ki, axis=-1, keepdims=True)
    var = jnp.mean(xc * xc, axis=-1, keepdims=True)
    ki = xc * lax.rsqrt(var + EPS) * lng_ref[...] + lnb_ref[...]
    ki_ref[...] = _rope(ki, c, sa, sb).astype(BF16)
    wi = proj(o + LANES, o + 2 * LANES) * ((IDX_HEADS ** -0.5) * (IDX_DIM ** -0.5))
    wit_ref[0] = wi.T[:IDX_HEADS]


def _front1(h, g, w, c, sa, sb, lng, lnb, batch, seq):
    t, d = h.shape
    tm = ROW_TILE
    assert tm == DSA_K_CHUNK
    nt = seq // tm
    row = lambda n: pl.BlockSpec((tm, n), lambda i: (i, 0))
    tr = lambda n: pl.BlockSpec((1, n, tm), lambda i: (i // nt, 0, i % nt))
    return pl.pallas_call(
        _front1_kernel,
        grid=(t // tm,),
        in_specs=[row(d), _const_spec((1, d)), _const_spec(w.shape), row(LANES), row(LANES), row(LANES),
                  _const_spec(lng.shape), _const_spec(lnb.shape)],
        out_specs=[tr(WIDTH_C), row(WIDTH_C),
                   pl.BlockSpec((1, 1, WIDTH_C, tm), lambda i: (i // nt, i % nt, 0, 0)),
                   tr(IDX_HEADS * IDX_DIM), row(LANES), tr(IDX_HEADS)],
        out_shape=[jax.ShapeDtypeStruct((batch, WIDTH_C, seq), BF16),
                   jax.ShapeDtypeStruct((t, WIDTH_C), BF16),
                   jax.ShapeDtypeStruct((batch, nt, WIDTH_C, tm), BF16),
                   jax.ShapeDtypeStruct((batch, IDX_HEADS * IDX_DIM, seq), BF16),
                   jax.ShapeDtypeStruct((t, LANES), BF16),
                   jax.ShapeDtypeStruct((batch, IDX_HEADS, seq), F32)],
        compiler_params=_params(1),
        name="front1",
    )(h, g, w, c, sa, sb, lng, lnb)


def _dsa_index_kernel(qit_ref, wit_ref, ki_ref, b_ref, sc_ref, sc16_ref, *, n_sel):
    i = pl.program_id(1)
    tq = qit_ref.shape[2]
    kc_n = DSA_K_CHUNK
    n_chunks_total = sc_ref.shape[0]
    n_chunks = (i * tq + tq + kc_n - 1) // kc_n
    qpos = i * tq + lax.broadcasted_iota(jnp.int32, (1, tq), 1)
    kofs = lax.broadcasted_iota(jnp.int32, (kc_n, 1), 0)
    wv = wit_ref[0]

    def causal(c):
        return (c * kc_n + kofs) <= qpos

    big = 3e38

    def score_chunk(c, carry):
        mx, mn = carry
        kc = ki_ref[pl.ds(pl.multiple_of(c * kc_n, kc_n), kc_n), :]
        acc = jnp.zeros((kc_n, tq), F32)
        for h in range(IDX_HEADS):
            qm = _head_rows(qit_ref[0, (h // 2) * LANES:(h // 2 + 1) * LANES, :], h % 2)
            acc = acc + jnp.maximum(_dot(kc, qm), 0.0) * wv[h:h + 1, :]
        ok = causal(c)
        sc_ref[c] = jnp.where(ok, acc, NEG)
        sc16_ref[c] = jnp.where(ok, acc, NEG).astype(BF16)
        return (jnp.maximum(mx, _fold_rows(jnp.where(ok, acc, -big), jnp.maximum)),
                jnp.minimum(mn, _fold_rows(jnp.where(ok, acc, big), jnp.minimum)))

    mx, mn = lax.fori_loop(0, n_chunks, score_chunk,
                           (jnp.full((SUBLANES, tq), -big, F32), jnp.full((SUBLANES, tq), big, F32)))
    col_max = jnp.max(mx, axis=0, keepdims=True)
    col_min = jnp.min(mn, axis=0, keepdims=True)

    def reduce_chunks(fn, op, init):
        def body(c, acc):
            return op(acc, _fold_rows(fn(sc_ref[c]), op))
        return lax.fori_loop(0, n_chunks, body, jnp.full((SUBLANES, tq), init, F32))

    def count_ge(thr):
        acc = reduce_chunks(lambda x: jnp.where(x >= thr, 1.0, 0.0), jnp.add, 0.0)
        return jnp.sum(acc, axis=0, keepdims=True)

    def pending(flags):
        return jnp.max(jnp.where(flags, 0.0, 1.0))

    k_f = float(n_sel)
    n_causal = (qpos + 1).astype(F32)
    small = n_causal < k_f
    rows16 = 2 * SUBLANES

    def count_ge_bf16(thr):
        one, zero = jnp.ones((), BF16), jnp.zeros((), BF16)

        def body(c, acc):
            hits = jnp.where(sc16_ref[c] >= thr, one, zero)
            return acc + _fold_rows(hits, jnp.add, rows16).astype(F32)

        return jnp.sum(lax.fori_loop(0, n_chunks, body, jnp.zeros((rows16, tq), F32)), axis=0, keepdims=True)

    def coarse(_, carry):
        lo, hi = carry
        mid = (0.5 * (lo + hi)).astype(BF16)
        ok = count_ge_bf16(mid) >= k_f
        mid = mid.astype(F32)
        return jnp.where(ok, mid, lo), jnp.where(ok, hi, mid)

    top = col_max.astype(BF16).astype(F32)
    lo, hi = lax.fori_loop(0, COARSE_STEPS, coarse,
                           (col_min.astype(BF16).astype(F32), (top + jnp.abs(top) * 2.0 ** -6 + 1e-30).astype(BF16).astype(F32)))
    lo = jnp.where(small, NEG, lo - jnp.abs(lo) * 2.0 ** -7 - 1e-37)
    c_lo = jnp.full((1, tq), 2.0 * k_f, F32)
    c_hi = jnp.full((1, tq), -1.0, F32)

    def bisect(_, carry):
        lo, hi, c_lo, c_hi = carry
        mid = 0.5 * (lo + hi)
        cnt = count_ge(mid)
        ok = cnt >= k_f
        return (jnp.where(ok, mid, lo), jnp.where(ok, hi, mid), jnp.where(ok, cnt, c_lo), jnp.where(ok, c_hi, cnt))

    def bracket_ends(lo, hi):
        def body(c, carry):
            below, above = carry
            x = sc_ref[c]
            return (jnp.maximum(below, _fold_rows(jnp.where(x < hi, x, -big), jnp.maximum)),
                    jnp.minimum(above, _fold_rows(jnp.where(x >= lo, x, big), jnp.minimum)))

        below, above = lax.fori_loop(0, n_chunks, body, (jnp.full((SUBLANES, tq), -big, F32),
                                                         jnp.full((SUBLANES, tq), big, F32)))
        return jnp.max(below, axis=0, keepdims=True), jnp.min(above, axis=0, keepdims=True)

    def settled(c_lo, c_hi, below, above):
        return small | (c_hi == k_f - 1.0) | (c_lo == k_f) | (below == above)

    def narrow_cond(carry):
        return (carry[6] < BISECT_ROUNDS) & (carry[7] > 0.5)

    def narrow(carry):
        state = lax.fori_loop(0, BISECT_PER_ROUND, bisect, carry[:4])
        below, above = bracket_ends(state[0], state[1])
        return (*state, below, above, carry[6] + 1, pending(settled(state[2], state[3], below, above)))

    state = lax.fori_loop(0, BISECT_WARMUP, bisect, (lo, hi, c_lo, c_hi))
    zeros = jnp.zeros((1, tq), F32)
    lo, hi, c_lo, c_hi, below, above, _, _ = lax.while_loop(
        narrow_cond, narrow, (*state, zeros, zeros, jnp.int32(0), jnp.float32(1.0)))
    thr = jnp.where(small, NEG, jnp.where(c_lo == k_f, above, below))
    done = jnp.where(settled(c_lo, c_hi, below, above), 1.0, 0.0)

    def snap_cond(carry):
        return carry[3] > 0.5

    def snap(carry):
        hi, thr, done, _ = carry
        cand = jnp.max(reduce_chunks(lambda x: jnp.where(x < hi, x, -big), jnp.maximum, -big),
                       axis=0, keepdims=True)
        ok = count_ge(cand) >= k_f
        thr = jnp.where(done > 0.5, thr, cand)
        done = jnp.where(ok, 1.0, done)
        hi = jnp.where(done > 0.5, hi, cand)
        return hi, thr, done, jnp.max(1.0 - done)

    _, thr, _, _ = lax.while_loop(snap_cond, snap, (hi, thr, done, jnp.max(1.0 - done)))

    n_ge = count_ge(thr)
    tied = jnp.max(jnp.where((n_ge > k_f) & jnp.logical_not(small), 1.0, 0.0))

    @pl.when(tied < 0.5)
    def _():
        def body(c, carry):
            x = sc_ref[c]
            b_ref[0, 0, c] = jnp.where((x >= thr) & causal(c), 0.0, NEG)
            return carry
        lax.fori_loop(0, n_chunks, body, 0)

    @pl.when(tied > 0.5)
    def _():
        n_gt = reduce_chunks(lambda x: jnp.where(x > thr, 1.0, 0.0), jnp.add, 0.0)
        need = k_f - jnp.sum(n_gt, axis=0, keepdims=True)

        def body(c, seen):
            x = sc_ref[c]
            eq = x == thr
            total = seen + jnp.sum(_fold_rows(jnp.where(eq, 1.0, 0.0), jnp.add), axis=0, keepdims=True)
            keep_all = total <= need
            crossing = jnp.max(jnp.where(keep_all | (seen >= need), 0.0, 1.0))

            @pl.when(crossing < 0.5)
            def _():
                b_ref[0, 0, c] = jnp.where(((x > thr) | (eq & keep_all)) & causal(c), 0.0, NEG)

            @pl.when(crossing > 0.5)
            def _():
                r = lax.broadcasted_iota(jnp.int32, (kc_n, kc_n), 0)
                cc = lax.broadcasted_iota(jnp.int32, (kc_n, kc_n), 1)
                prefix = jnp.where(cc <= r, 1.0, 0.0).astype(BF16)
                cnt = seen + _dot(prefix, jnp.where(eq, 1.0, 0.0).astype(BF16))
                b_ref[0, 0, c] = jnp.where(((x > thr) | (eq & (cnt <= need))) & causal(c), 0.0, NEG)

            return total
        lax.fori_loop(0, n_chunks, body, jnp.zeros((1, tq), F32))

    def fill(c, carry):
        b_ref[0, 0, c] = jnp.full((kc_n, tq), NEG, F32)
        return carry

    lax.fori_loop(n_chunks, n_chunks_total, fill, 0)


def _dsa_index(qit, wit, ki, batch, seq):
    tq = Q_TILE
    nq = seq // tq
    nch = seq // DSA_K_CHUNK
    n_sel = min(DSA_TOPK, seq // 4)
    return pl.pallas_call(
        functools.partial(_dsa_index_kernel, n_sel=n_sel),
        grid=(batch, nq),
        in_specs=[pl.BlockSpec((1, qit.shape[1], tq), lambda b, i: (b, 0, i)),
                  pl.BlockSpec((1, IDX_HEADS, tq), lambda b, i: (b, 0, i)),
                  pl.BlockSpec((seq, LANES), lambda b, i: (b, 0))],
        out_specs=pl.BlockSpec((1, 1, nch, DSA_K_CHUNK, tq), lambda b, i: (b, i, 0, 0, 0)),
        out_shape=jax.ShapeDtypeStruct((batch, nq, nch, DSA_K_CHUNK, tq), F32),
        scratch_shapes=[pltpu.VMEM((nch, DSA_K_CHUNK, tq), F32), pltpu.VMEM((nch, DSA_K_CHUNK, tq), BF16)],
        compiler_params=_params(2),
        name="dsa_index",
    )(qit, wit, ki)


def _dsa_attn_kernel(qt_ref, k_ref, vt_ref, b_ref, o_ref, qm_ref, s_ref, cm_ref, m_ref, acc_ref):
    i = pl.program_id(1)
    tq = qt_ref.shape[2]
    kc_n = DSA_K_CHUNK
    n_heads = acc_ref.shape[0]
    n_chunks = (i * tq + tq + kc_n - 1) // kc_n
    _softmax_init(m_ref, acc_ref)
    for h in range(n_heads):
        qm_ref[h] = _head_rows(qt_ref[0, (h // 2) * LANES:(h // 2 + 1) * LANES, :], h % 2)

    def chunk_scores(c):
        st = pl.multiple_of(c * kc_n, kc_n)

        def scores(h):
            return _dot(k_ref[pl.ds(st, kc_n), (h // 2) * LANES:(h // 2 + 1) * LANES], qm_ref[h]) + b_ref[0, 0, c]

        return scores

    _stage_scores(chunk_scores(0), range(s_ref.shape[0] - 1), s_ref, cm_ref)

    def body(c, carry):
        def values(h):
            return vt_ref[0, c, h * HEAD_DIM:(h + 1) * HEAD_DIM, :]

        _attend_chunk(n_heads, chunk_scores(c), chunk_scores(jnp.minimum(c + 1, n_chunks - 1)), values,
                      s_ref, cm_ref, m_ref, acc_ref)
        return carry

    lax.fori_loop(0, n_chunks, body, 0)
    _write_heads(o_ref, acc_ref)


def _dsa_attn(qt, k, vt, bias, batch, seq):
    t, width = k.shape
    tq = Q_TILE
    nq = seq // tq
    n_heads = width // HEAD_DIM
    one = pl.Buffered(1)
    return pl.pallas_call(
        _dsa_attn_kernel,
        grid=(batch, nq),
        in_specs=[pl.BlockSpec((1, width, tq), lambda b, i: (b, 0, i)),
                  pl.BlockSpec((seq, width), lambda b, i: (b, 0), pipeline_mode=one),
                  pl.BlockSpec((1,) + vt.shape[1:], lambda b, i: (b, 0, 0, 0), pipeline_mode=one),
                  pl.BlockSpec((1, 1) + bias.shape[2:], lambda b, i: (b, i, 0, 0, 0))],
        out_specs=pl.BlockSpec((tq, width), lambda b, i: (b * nq + i, 0)),
        out_shape=jax.ShapeDtypeStruct((t, width), BF16),
        scratch_shapes=[pltpu.VMEM((n_heads, LANES, tq), BF16), pltpu.VMEM((SCORE_SLOTS, DSA_K_CHUNK, tq), F32),
                        pltpu.VMEM((SCORE_SLOTS, SUBLANES, tq), F32),
                        pltpu.VMEM((n_heads, 1, tq), F32), pltpu.VMEM((n_heads, HEAD_DIM + ONES_ROWS, tq), F32)],
        compiler_params=_params(2),
        name="dsa_attn",
    )(qt, k, vt, bias)


def _rope_tables(positions):
    inv_freq = ROPE_THETA ** (-jnp.arange(0, ROPE_DIM, 2, dtype=F32) / ROPE_DIM)
    ang = positions.astype(F32).reshape(-1, 1) * inv_freq
    cos, sin = jnp.cos(ang), jnp.sin(ang)
    t = cos.shape[0]
    rest = HEAD_DIM - ROPE_DIM
    c = jnp.concatenate([cos, cos, jnp.ones((t, rest), F32)], axis=1)
    sa = jnp.concatenate([-sin, jnp.zeros((t, HEAD_DIM - ROPE_HALF), F32)], axis=1)
    sb = jnp.concatenate([jnp.zeros((t, ROPE_HALF), F32), sin, jnp.zeros((t, rest), F32)], axis=1)
    return tuple(jnp.tile(x, (1, LANES // HEAD_DIM)) for x in (c, sa, sb))


def kernel(x, p, positions, g_mix, w_in_even, a_ln_g, a_ln_b, a_w_s, a_b_s, w_out_even, w_in_odd, c_kidx_ln_g, c_kidx_ln_b, w_out_odd, g_ffn, w_ffn_gate, w_ffn_up, w_ffn_down, g_ple, w_ple_proj, w_ple_gate, g_final):
    batch, seq, d = x.shape
    depth = p.shape[0]
    t = batch * seq
    assert seq % ROW_TILE == 0 and seq % DSA_K_CHUNK == 0 and min(DSA_TOPK, seq // 4) <= DSA_K_CHUNK
    c, sa, sb = _rope_tables(positions)
    h = x.reshape(t, d)
    p2 = p.reshape(depth, t, p.shape[-1])
    row = lambda a: a.reshape(1, -1)
    onehot = (jnp.arange(seq)[:, None] // MOBA_BLOCK == jnp.arange(LANES)[None, :]).astype(BF16)
    bf = lambda w: w.astype(BF16)
    wo_even, wo_odd, wg, wu, wd, wpg, wpp = map(bf, (w_out_even, w_out_odd, w_ffn_gate, w_ffn_up, w_ffn_down,
                                                     w_ple_gate, w_ple_proj))
    for i in range(depth):
        j = i // 2
        if i % 2 == 0:
            bs = jnp.repeat(a_b_s[j].T, HEAD_DIM, axis=1)
            ya, qt, k, vt, km = _front0(h, row(g_mix[i]), bf(w_in_even[j]), c, sa, sb,
                                        row(a_ln_g[j]), row(a_ln_b[j]), a_w_s[j], bs, batch, seq)
            ys = [ya, _moba(qt, k, vt, km.reshape(t // MOBA_BLOCK, WIDTH_B), onehot, batch, seq)]
            wo = wo_even
        else:
            w = w_in_odd[j]
            o = 3 * WIDTH_C + IDX_HEADS * IDX_DIM
            w1 = bf(jnp.concatenate([w[:, :o], w[:, o:o + IDX_DIM], w[:, o:o + IDX_DIM], w[:, o + IDX_DIM:],
                                     jnp.zeros((d, LANES - IDX_HEADS), w.dtype)], axis=1))
            qt, k, vt, qit, ki, wit = _front1(h, row(g_mix[i]), w1, c, sa, sb, row(jnp.tile(c_kidx_ln_g[j], 2)),
                                              row(jnp.tile(c_kidx_ln_b[j], 2)), batch, seq)
            ys = [_dsa_attn(qt, k, vt, _dsa_index(qit, wit, ki, batch, seq), batch, seq)]
            wo = wo_odd
        h = _post(h, ys, wo, j, p2, i, row(g_ffn[i]), wg, wu, wd, row(g_ple[i]), wpg, wpp, row(g_final),
                  final=(i == depth - 1))
    return h.reshape(batch, seq, d)
```

```python
import functools

import jax
import jax.numpy as jnp
from jax import lax
from jax.experimental import pallas as pl
from jax.experimental.pallas import tpu as pltpu

HEAD_DIM = 64
ROPE_DIM = HEAD_DIM // 4
ROPE_HALF = ROPE_DIM // 2
ROPE_THETA = 500000.0
N_GROUPS_A = 8
WIDTH_A = N_GROUPS_A * HEAD_DIM
N_HEADS_B = 8
WIDTH_B = N_HEADS_B * HEAD_DIM
N_HEADS_C = 16
WIDTH_C = N_HEADS_C * HEAD_DIM
SGU_CHUNK = 128
MOBA_BLOCK = 256
MOBA_TOPK = 3
IDX_HEADS = 8
IDX_DIM = 64
DSA_TOPK = 256
EPS = 1e-6
NEG = -1e30
LOG2E = 1.4426950408889634
BF16_SPACING = 2.0 ** -7
TINY = 1e-30

LANES = 128
SUBLANES = 8
ROW_TILE = 512
Q_TILE = 256
DSA_K_CHUNK = 512
FOLD_CHAINS = 8
ONES_ROWS = 16
SCORE_SLOTS = 4
MOBA_SLOTS = 4
MOBA_K_CHUNK = 2 * MOBA_BLOCK
COARSE_STEPS = 16
BISECT_PER_ROUND = 8
BISECT_ROUNDS = 12
VMEM_LIMIT = 56 * 1024 * 1024

F32 = jnp.float32
BF16 = jnp.bfloat16


def _dot(a, b):
    return jnp.dot(a, b, preferred_element_type=F32)


def _rms(x, g):
    return x * lax.rsqrt(jnp.mean(x * x, axis=-1, keepdims=True) + EPS) * g


def _rope(z, c, sa, sb):
    outs = []
    for j in range(z.shape[1] // LANES):
        zz = z[:, j * LANES:(j + 1) * LANES]
        outs.append(zz * c + pltpu.roll(zz, LANES - ROPE_HALF, 1) * sa + pltpu.roll(zz, ROPE_HALF, 1) * sb)
    return outs[0] if len(outs) == 1 else jnp.concatenate(outs, axis=1)


def _gelu(x):
    return 0.5 * x * (1.0 + lax.erf(x * (2.0 ** -0.5)))


def _fold_rows(x, op, group=SUBLANES):
    parts = [x[a:a + group] for a in range(0, x.shape[0], group)]
    chains = parts[:FOLD_CHAINS]
    for a, part in enumerate(parts[FOLD_CHAINS:]):
        chains[a % FOLD_CHAINS] = op(chains[a % FOLD_CHAINS], part)
    while len(chains) > 1:
        chains = [op(chains[a], chains[a + 1]) if a + 1 < len(chains) else chains[a] for a in range(0, len(chains), 2)]
    return chains[0]


def _head_rows(x, hh):
    r = lax.broadcasted_iota(jnp.int32, (LANES, 1), 0)
    return jnp.where((r >= hh * HEAD_DIM) & (r < (hh + 1) * HEAD_DIM), x, jnp.zeros_like(x))


def _const_spec(shape):
    nd = len(shape)
    return pl.BlockSpec(shape, lambda *_: (0,) * nd, pipeline_mode=pl.Buffered(1))


def _params(n_axes, flags=None):
    return pltpu.CompilerParams(dimension_semantics=("arbitrary",) * n_axes, vmem_limit_bytes=VMEM_LIMIT, flags=flags)


def _stage_scores(score_fn, heads, s_ref, cm_ref):
    n_slots = s_ref.shape[0]
    for h in heads:
        s = score_fn(h)
        s_ref[h % n_slots] = s
        cm_ref[h % n_slots] = _fold_rows(s, jnp.maximum)


def _attend_chunk(n_heads, score_fn, next_score_fn, vt_fn, s_ref, cm_ref, m_ref, acc_ref):
    n_slots = s_ref.shape[0]
    ahead = n_slots - 1
    assert n_heads % n_slots == 0
    for h in range(n_heads):
        if h + ahead < n_heads:
            _stage_scores(score_fn, [h + ahead], s_ref, cm_ref)
        else:
            _stage_scores(next_score_fn, [h + ahead - n_heads], s_ref, cm_ref)
        slot = h % n_slots
        m_old = m_ref[h]
        m_new = jnp.maximum(m_old, jnp.max(cm_ref[slot], axis=0, keepdims=True))
        p = jnp.exp2(s_ref[slot] - m_new).astype(BF16)
        m_ref[h] = m_new
        vt = vt_fn(h)
        vt = jnp.concatenate([vt, jnp.ones((acc_ref.shape[1] - vt.shape[0], vt.shape[1]), BF16)], axis=0)
        acc_ref[h] = jnp.exp2(m_old - m_new) * acc_ref[h] + _dot(vt, p)


def _softmax_init(m_ref, acc_ref):
    m_ref[...] = jnp.full(m_ref.shape, -jnp.inf, F32)
    acc_ref[...] = jnp.zeros(acc_ref.shape, F32)


def _write_heads(o_ref, acc_ref):
    def head(h):
        return acc_ref[h, :HEAD_DIM, :] / acc_ref[h, HEAD_DIM:HEAD_DIM + 1, :]

    for p in range(acc_ref.shape[0] // 2):
        o_ref[:, p * LANES:(p + 1) * LANES] = jnp.concatenate([head(2 * p), head(2 * p + 1)], axis=0).T.astype(BF16)


def _front0_kernel(h_ref, g_ref, w_ref, c_ref, sa_ref, sb_ref, lng_ref, lnb_ref, ws_ref, bs_ref,
                   ya_ref, qt_ref, k_ref, vt_ref, km_ref):
    tm = h_ref.shape[0]
    xn = _rms(h_ref[...], g_ref[...]).astype(BF16)

    def proj(lo, hi):
        return _dot(xn, w_ref[:, lo:hi])

    u = _gelu(proj(0, WIDTH_A))
    v = _gelu(proj(WIDTH_A, 2 * WIDTH_A))
    lane = lax.broadcasted_iota(jnp.int32, (1, LANES), 1)
    low = lane < HEAD_DIM
    row = lax.broadcasted_iota(jnp.int32, (SGU_CHUNK, SGU_CHUNK), 0)
    col = lax.broadcasted_iota(jnp.int32, (SGU_CHUNK, SGU_CHUNK), 1)
    tril = col <= row
    for j in range(WIDTH_A // LANES):
        sl = slice(j * LANES, (j + 1) * LANES)
        vv = v[:, sl]

        def half_mean(t):
            s_lo = jnp.sum(jnp.where(low, t, 0.0), axis=-1, keepdims=True)
            s_hi = jnp.sum(jnp.where(low, 0.0, t), axis=-1, keepdims=True)
            return jnp.where(low, s_lo, s_hi) * (1.0 / HEAD_DIM)

        xc = vv - half_mean(vv)
        var = half_mean(xc * xc)
        vn = (xc * lax.rsqrt(var + EPS) * lng_ref[:, sl] + lnb_ref[:, sl]).astype(BF16)
        w_lo = jnp.where(tril, ws_ref[2 * j], 0.0).astype(BF16)
        w_hi = jnp.where(tril, ws_ref[2 * j + 1], 0.0).astype(BF16)
        for c in range(tm // SGU_CHUNK):
            rs = slice(c * SGU_CHUNK, (c + 1) * SGU_CHUNK)
            vc = vn[rs]
            mixed = jnp.where(low, _dot(w_lo, vc), _dot(w_hi, vc)) + bs_ref[:, sl]
            ya_ref[rs, sl] = (u[rs, sl] * mixed).astype(BF16)

    o = 2 * WIDTH_A
    c, sa, sb = c_ref[...], sa_ref[...], sb_ref[...]
    q = _rope(proj(o, o + WIDTH_B), c, sa, sb) * (HEAD_DIM ** -0.5 * LOG2E)
    qt_ref[0] = q.T.astype(BF16)
    k = _rope(proj(o + WIDTH_B, o + 2 * WIDTH_B), c, sa, sb)
    k_ref[...] = k.astype(BF16)
    v_b = proj(o + 2 * WIDTH_B, o + 3 * WIDTH_B)
    for blk in range(tm // MOBA_BLOCK):
        rs = slice(blk * MOBA_BLOCK, (blk + 1) * MOBA_BLOCK)
        km_ref[0, blk:blk + 1, :] = jnp.mean(k[rs], axis=0, keepdims=True)
        vt_ref[0, blk] = v_b[rs].T.astype(BF16)


def _front0(h, g, w, c, sa, sb, lng, lnb, ws, bs, batch, seq):
    t, d = h.shape
    tm = ROW_TILE
    nt = seq // tm
    nblk = tm // MOBA_BLOCK
    row = lambda n: pl.BlockSpec((tm, n), lambda i: (i, 0))
    return pl.pallas_call(
        _front0_kernel,
        grid=(t // tm,),
        in_specs=[row(d), _const_spec((1, d)), _const_spec(w.shape), row(LANES), row(LANES), row(LANES),
                  _const_spec(lng.shape), _const_spec(lnb.shape), _const_spec(ws.shape), _const_spec(bs.shape)],
        out_specs=[row(WIDTH_A),
                   pl.BlockSpec((1, WIDTH_B, tm), lambda i: (i // nt, 0, i % nt)),
                   row(WIDTH_B),
                   pl.BlockSpec((1, nblk, WIDTH_B, MOBA_BLOCK), lambda i: (i // nt, i % nt, 0, 0)),
                   pl.BlockSpec((1, nblk, WIDTH_B), lambda i: (i, 0, 0))],
        out_shape=[jax.ShapeDtypeStruct((t, WIDTH_A), BF16),
                   jax.ShapeDtypeStruct((batch, WIDTH_B, seq), BF16),
                   jax.ShapeDtypeStruct((t, WIDTH_B), BF16),
                   jax.ShapeDtypeStruct((batch, seq // MOBA_BLOCK, WIDTH_B, MOBA_BLOCK), BF16),
                   jax.ShapeDtypeStruct((t // tm, nblk, WIDTH_B), F32)],
        compiler_params=_params(1),
        name="front0",
    )(h, g, w, c, sa, sb, lng, lnb, ws, bs)


def _moba_kernel(qt_ref, k_ref, vt_ref, km_ref, oh_ref, o_ref, qc_ref, s_ref, cm_ref, m_ref, acc_ref):
    i = pl.program_id(1)
    tq = qt_ref.shape[2]
    nb = km_ref.shape[0]
    n_heads = acc_ref.shape[0]
    kc_n = s_ref.shape[1]
    per_chunk = kc_n // MOBA_BLOCK
    blk = lax.broadcasted_iota(jnp.int32, (nb, 1), 0)
    blk_f = blk.astype(F32)
    past = blk < i
    _softmax_init(m_ref, acc_ref)
    for h in range(n_heads):
        sl = slice((h // 2) * LANES, (h // 2 + 1) * LANES)
        qm = _head_rows(qt_ref[0, sl, :], h % 2)
        g = jnp.where(past, _dot(km_ref[:, sl].astype(BF16), qm), NEG)
        sel = jnp.zeros(g.shape, jnp.bool_)
        for _ in range(min(MOBA_TOPK, nb)):
            top = jnp.max(g, axis=0, keepdims=True)
            idx = jnp.min(jnp.where(g == top, blk_f, 1e9), axis=0, keepdims=True)
            pick = blk_f == idx
            sel = sel | pick
            g = jnp.where(pick, -jnp.inf, g)
        selb = jnp.where((sel & past) | (blk == i), 0.0, NEG).astype(BF16)
        qc_ref[h] = jnp.concatenate([qm, selb, jnp.zeros((LANES - nb, tq), BF16)], axis=0)

    def values(c):
        def vt(h):
            rows = slice(h * HEAD_DIM, (h + 1) * HEAD_DIM)
            return jnp.concatenate([vt_ref[0, per_chunk * c + a, rows, :] for a in range(per_chunk)], axis=1)
        return vt

    def chunk_scores(c, mask=None):
        st = pl.multiple_of(c * kc_n, kc_n)

        def scores(h):
            kc = jnp.concatenate([k_ref[pl.ds(st, kc_n), (h // 2) * LANES:(h // 2 + 1) * LANES],
                                  oh_ref[pl.ds(st, kc_n), :]], axis=1)
            s = _dot(kc, qc_ref[h])
            return s if mask is None else jnp.where(mask, s, NEG)

        return scores

    own = i // per_chunk
    kpos = own * kc_n + lax.broadcasted_iota(jnp.int32, (kc_n, 1), 0)
    qpos = i * tq + lax.broadcasted_iota(jnp.int32, (1, tq), 1)
    own_scores = chunk_scores(own, kpos <= qpos)
    refs = (s_ref, cm_ref, m_ref, acc_ref)
    last = jnp.maximum(own - 1, 0)
    _stage_scores(own_scores, range(s_ref.shape[0] - 1), s_ref, cm_ref)
    _attend_chunk(n_heads, own_scores, chunk_scores(0), values(own), *refs)

    def body(c, carry):
        _attend_chunk(n_heads, chunk_scores(c), chunk_scores(jnp.minimum(c + 1, last)), values(c), *refs)
        return carry

    lax.fori_loop(0, own, body, 0)
    _write_heads(o_ref, acc_ref)


def _moba(qt, k, vt, km, onehot, batch, seq):
    t, width = k.shape
    tq = Q_TILE
    assert tq == MOBA_BLOCK
    nb = seq // MOBA_BLOCK
    nq = seq // tq
    n_heads = width // HEAD_DIM
    one = pl.Buffered(1)
    return pl.pallas_call(
        _moba_kernel,
        grid=(batch, nq),
        in_specs=[pl.BlockSpec((1, width, tq), lambda b, i: (b, 0, i)),
                  pl.BlockSpec((seq, width), lambda b, i: (b, 0), pipeline_mode=one),
                  pl.BlockSpec((1, nb, width, MOBA_BLOCK), lambda b, i: (b, 0, 0, 0), pipeline_mode=one),
                  pl.BlockSpec((nb, width), lambda b, i: (b, 0)),
                  _const_spec(onehot.shape)],
        out_specs=pl.BlockSpec((tq, width), lambda b, i: (b * nq + i, 0)),
        out_shape=jax.ShapeDtypeStruct((t, width), BF16),
        scratch_shapes=[pltpu.VMEM((n_heads, 2 * LANES, tq), BF16), pltpu.VMEM((MOBA_SLOTS, MOBA_K_CHUNK, tq), F32),
                        pltpu.VMEM((MOBA_SLOTS, SUBLANES, tq), F32),
                        pltpu.VMEM((n_heads, 1, tq), F32), pltpu.VMEM((n_heads, HEAD_DIM + ONES_ROWS, tq), F32)],
        compiler_params=_params(2),
        name="moba",
    )(qt, k, vt, km, onehot)


def _post_kernel(*refs, n_mix, final):
    h_ref = refs[0]
    y_refs = refs[1:1 + n_mix]
    (wo_ref, p_ref, gf_ref, wg_ref, wu_ref, wd_ref, gp_ref, wpg_ref, wpp_ref, gl_ref, o_ref, a_ref) = refs[1 + n_mix:]
    h = h_ref[...]
    lo = 0
    for y_ref in y_refs:
        h = h + _dot(y_ref[...], wo_ref[0, lo:lo + y_ref.shape[1], :])
        lo += y_ref.shape[1]
    xn = _rms(h, gf_ref[...]).astype(BF16)
    d_ff = wg_ref.shape[2]
    step = 2 * LANES
    for lo in range(0, d_ff, step):
        gate = _dot(xn, wg_ref[0, :, lo:lo + step])
        up = _dot(xn, wu_ref[0, :, lo:lo + step])
        a_ref[:, lo:lo + step] = (gate * jax.nn.sigmoid(gate) * up).astype(BF16)
    h = h + _dot(a_ref[...], wd_ref[0])
    pg = jax.nn.sigmoid(_dot(_rms(h, gp_ref[...]).astype(BF16), wpg_ref[0]))
    h = h + _dot(p_ref[0].astype(BF16), wpp_ref[0]) * pg
    o_ref[...] = _rms(h, gl_ref[...]) if final else h


def _layer_spec(shape, layer):
    rest = (0,) * (len(shape) - 1)
    return pl.BlockSpec((1,) + tuple(shape[1:]), lambda *_: (layer,) + rest, pipeline_mode=pl.Buffered(1))


def _post(h, ys, wo, wo_layer, p, layer, gf, wg, wu, wd, gp, wpg, wpp, gl, final):
    t, d = h.shape
    tm = ROW_TILE
    row = lambda n: pl.BlockSpec((tm, n), lambda i: (i, 0))
    stack = lambda w: _layer_spec(w.shape, layer)
    return pl.pallas_call(
        functools.partial(_post_kernel, n_mix=len(ys), final=final),
        grid=(t // tm,),
        in_specs=[row(d)] + [row(y.shape[1]) for y in ys]
                 + [_layer_spec(wo.shape, wo_layer), pl.BlockSpec((1, tm, p.shape[2]), lambda i: (layer, i, 0)),
                    _const_spec(gf.shape), stack(wg), stack(wu), stack(wd), _const_spec(gp.shape), stack(wpg),
                    stack(wpp), _const_spec(gl.shape)],
        out_specs=row(d),
        out_shape=jax.ShapeDtypeStruct((t, d), F32),
        scratch_shapes=[pltpu.VMEM((tm, wg.shape[2]), BF16)],
        compiler_params=_params(1),
        name="post_final" if final else "post",
    )(h, *ys, wo, p, gf, wg, wu, wd, gp, wpg, wpp, gl)


def _front1_kernel(h_ref, g_ref, w_ref, c_ref, sa_ref, sb_ref, lng_ref, lnb_ref,
                   qt_ref, k_ref, vt_ref, qit_ref, ki_ref, wit_ref):
    xn = _rms(h_ref[...], g_ref[...]).astype(BF16)

    def proj(lo, hi):
        return _dot(xn, w_ref[:, lo:hi])

    c, sa, sb = c_ref[...], sa_ref[...], sb_ref[...]
    qt_ref[0] = (_rope(proj(0, WIDTH_C), c, sa, sb) * (HEAD_DIM ** -0.5 * LOG2E)).T.astype(BF16)
    k_ref[...] = _rope(proj(WIDTH_C, 2 * WIDTH_C), c, sa, sb).astype(BF16)
    vt_ref[0, 0] = proj(2 * WIDTH_C, 3 * WIDTH_C).T.astype(BF16)
    o = 3 * WIDTH_C
    qit_ref[0] = _rope(proj(o, o + IDX_HEADS * IDX_DIM), c, sa, sb).T.astype(BF16)
    o += IDX_HEADS * IDX_DIM
    ki = proj(o, o + LANES)
    xc = ki - jnp.mean(ki, axis=-1, keepdims=True)
    var = jnp.mean(xc * xc, axis=-1, keepdims=True)
    ki = xc * lax.rsqrt(var + EPS) * lng_ref[...] + lnb_ref[...]
    ki_ref[...] = _rope(ki, c, sa, sb).astype(BF16)
    wi = proj(o + LANES, o + 2 * LANES) * ((IDX_HEADS ** -0.5) * (IDX_DIM ** -0.5))
    wit_ref[0] = wi.T[:IDX_HEADS]


def _front1(h, g, w, c, sa, sb, lng, lnb, batch, seq):
    t, d = h.shape
    tm = ROW_TILE
    assert tm == DSA_K_CHUNK
    nt = seq // tm
    row = lambda n: pl.BlockSpec((tm, n), lambda i: (i, 0))
    tr = lambda n: pl.BlockSpec((1, n, tm), lambda i: (i // nt, 0, i % nt))
    return pl.pallas_call(
        _front1_kernel,
        grid=(t // tm,),
        in_specs=[row(d), _const_spec((1, d)), _const_spec(w.shape), row(LANES), row(LANES), row(LANES),
                  _const_spec(lng.shape), _const_spec(lnb.shape)],
        out_specs=[tr(WIDTH_C), row(WIDTH_C),
                   pl.BlockSpec((1, 1, WIDTH_C, tm), lambda i: (i // nt, i % nt, 0, 0)),
                   tr(IDX_HEADS * IDX_DIM), row(LANES), tr(IDX_HEADS)],
        out_shape=[jax.ShapeDtypeStruct((batch, WIDTH_C, seq), BF16),
                   jax.ShapeDtypeStruct((t, WIDTH_C), BF16),
                   jax.ShapeDtypeStruct((batch, nt, WIDTH_C, tm), BF16),
                   jax.ShapeDtypeStruct((batch, IDX_HEADS * IDX_DIM, seq), BF16),
                   jax.ShapeDtypeStruct((t, LANES), BF16),
                   jax.ShapeDtypeStruct((batch, IDX_HEADS, seq), F32)],
        compiler_params=_params(1),
        name="front1",
    )(h, g, w, c, sa, sb, lng, lnb)


def _dsa_index_kernel(qit_ref, wit_ref, ki_ref, b_ref, sc_ref, sc16_ref, *, n_sel):
    i = pl.program_id(1)
    tq = qit_ref.shape[2]
    kc_n = DSA_K_CHUNK
    n_chunks_total = sc_ref.shape[0]
    n_chunks = (i * tq + tq + kc_n - 1) // kc_n
    qpos = i * tq + lax.broadcasted_iota(jnp.int32, (1, tq), 1)
    kofs = lax.broadcasted_iota(jnp.int32, (kc_n, 1), 0)
    wv = wit_ref[0]

    def causal(c):
        return (c * kc_n + kofs) <= qpos

    big = 3e38

    def score_chunk(c, carry):
        mx, mn = carry
        kc = ki_ref[pl.ds(pl.multiple_of(c * kc_n, kc_n), kc_n), :]
        acc = jnp.zeros((kc_n, tq), F32)
        for h in range(IDX_HEADS):
            qm = _head_rows(qit_ref[0, (h // 2) * LANES:(h // 2 + 1) * LANES, :], h % 2)
            acc = acc + jnp.maximum(_dot(kc, qm), 0.0) * wv[h:h + 1, :]
        ok = causal(c)
        sc_ref[c] = jnp.where(ok, acc, NEG)
        sc16_ref[c] = jnp.where(ok, acc, NEG).astype(BF16)
        return (jnp.maximum(mx, _fold_rows(jnp.where(ok, acc, -big), jnp.maximum)),
                jnp.minimum(mn, _fold_rows(jnp.where(ok, acc, big), jnp.minimum)))

    mx, mn = lax.fori_loop(0, n_chunks, score_chunk,
                           (jnp.full((SUBLANES, tq), -big, F32), jnp.full((SUBLANES, tq), big, F32)))
    col_max = jnp.max(mx, axis=0, keepdims=True)
    col_min = jnp.min(mn, axis=0, keepdims=True)

    def reduce_chunks(fn, op, init):
        def body(c, acc):
            return op(acc, _fold_rows(fn(sc_ref[c]), op))
        return lax.fori_loop(0, n_chunks, body, jnp.full((SUBLANES, tq), init, F32))

    def count_ge(thr):
        acc = reduce_chunks(lambda x: jnp.where(x >= thr, 1.0, 0.0), jnp.add, 0.0)
        return jnp.sum(acc, axis=0, keepdims=True)

    def pending(flags):
        return jnp.max(jnp.where(flags, 0.0, 1.0))

    k_f = float(n_sel)
    n_causal = (qpos + 1).astype(F32)
    small = n_causal < k_f
    rows16 = 2 * SUBLANES

    def count_ge_bf16(thr):
        one, zero = jnp.ones((), BF16), jnp.zeros((), BF16)

        def body(c, acc):
            hits = jnp.where(sc16_ref[c] >= thr, one, zero)
            return acc + _fold_rows(hits, jnp.add, rows16).astype(F32)

        return jnp.sum(lax.fori_loop(0, n_chunks, body, jnp.zeros((rows16, tq), F32)), axis=0, keepdims=True)

    def coarse(_, carry):
        lo, hi = carry
        mid = (0.5 * (lo + hi)).astype(BF16)
        ok = count_ge_bf16(mid) >= k_f
        mid = mid.astype(F32)
        return jnp.where(ok, mid, lo), jnp.where(ok, hi, mid)

    top = col_max.astype(BF16).astype(F32)
    lo, hi = lax.fori_loop(0, COARSE_STEPS, coarse,
                           (col_min.astype(BF16).astype(F32),
                            (top + jnp.abs(top) * 2.0 * BF16_SPACING + TINY).astype(BF16).astype(F32)))
    lo = jnp.where(small, NEG, lo - jnp.abs(lo) * BF16_SPACING - TINY)
    c_lo = jnp.full((1, tq), 2.0 * k_f, F32)
    c_hi = jnp.full((1, tq), -1.0, F32)

    def bisect(_, carry):
        lo, hi, c_lo, c_hi = carry
        mid = 0.5 * (lo + hi)
        cnt = count_ge(mid)
        ok = cnt >= k_f
        return (jnp.where(ok, mid, lo), jnp.where(ok, hi, mid), jnp.where(ok, cnt, c_lo), jnp.where(ok, c_hi, cnt))

    def bracket_ends(lo, hi):
        def body(c, carry):
            below, above = carry
            x = sc_ref[c]
            return (jnp.maximum(below, _fold_rows(jnp.where(x < hi, x, -big), jnp.maximum)),
                    jnp.minimum(above, _fold_rows(jnp.where(x >= lo, x, big), jnp.minimum)))

        below, above = lax.fori_loop(0, n_chunks, body, (jnp.full((SUBLANES, tq), -big, F32),
                                                         jnp.full((SUBLANES, tq), big, F32)))
        return jnp.max(below, axis=0, keepdims=True), jnp.min(above, axis=0, keepdims=True)

    def settled(c_lo, c_hi, below, above):
        return small | (c_hi == k_f - 1.0) | (c_lo == k_f) | (below == above)

    def narrow_cond(carry):
        return (carry[6] < BISECT_ROUNDS) & (carry[7] > 0.5)

    def narrow(carry):
        state = lax.fori_loop(0, BISECT_PER_ROUND, bisect, carry[:4])
        below, above = bracket_ends(state[0], state[1])
        return (*state, below, above, carry[6] + 1, pending(settled(state[2], state[3], below, above)))

    zeros = jnp.zeros((1, tq), F32)
    lo, hi, c_lo, c_hi, below, above, _, _ = lax.while_loop(
        narrow_cond, narrow, (lo, hi, c_lo, c_hi, zeros, zeros, jnp.int32(0), jnp.float32(1.0)))
    thr = jnp.where(small, NEG, jnp.where(c_lo == k_f, above, below))
    done = jnp.where(settled(c_lo, c_hi, below, above), 1.0, 0.0)

    def snap_cond(carry):
        return carry[3] > 0.5

    def snap(carry):
        hi, thr, done, _ = carry
        cand = jnp.max(reduce_chunks(lambda x: jnp.where(x < hi, x, -big), jnp.maximum, -big),
                       axis=0, keepdims=True)
        ok = count_ge(cand) >= k_f
        thr = jnp.where(done > 0.5, thr, cand)
        done = jnp.where(ok, 1.0, done)
        hi = jnp.where(done > 0.5, hi, cand)
        return hi, thr, done, jnp.max(1.0 - done)

    _, thr, _, _ = lax.while_loop(snap_cond, snap, (hi, thr, done, jnp.max(1.0 - done)))

    n_ge = count_ge(thr)
    tied = jnp.max(jnp.where((n_ge > k_f) & jnp.logical_not(small), 1.0, 0.0))

    @pl.when(tied < 0.5)
    def _():
        def body(c, carry):
            x = sc_ref[c]
            b_ref[0, 0, c] = jnp.where((x >= thr) & causal(c), 0.0, NEG)
            return carry
        lax.fori_loop(0, n_chunks, body, 0)

    @pl.when(tied > 0.5)
    def _():
        n_gt = reduce_chunks(lambda x: jnp.where(x > thr, 1.0, 0.0), jnp.add, 0.0)
        need = k_f - jnp.sum(n_gt, axis=0, keepdims=True)

        def body(c, seen):
            x = sc_ref[c]
            eq = x == thr
            total = seen + jnp.sum(_fold_rows(jnp.where(eq, 1.0, 0.0), jnp.add), axis=0, keepdims=True)
            keep_all = total <= need
            crossing = jnp.max(jnp.where(keep_all | (seen >= need), 0.0, 1.0))

            @pl.when(crossing < 0.5)
            def _():
                b_ref[0, 0, c] = jnp.where(((x > thr) | (eq & keep_all)) & causal(c), 0.0, NEG)

            @pl.when(crossing > 0.5)
            def _():
                r = lax.broadcasted_iota(jnp.int32, (kc_n, kc_n), 0)
                cc = lax.broadcasted_iota(jnp.int32, (kc_n, kc_n), 1)
                prefix = jnp.where(cc <= r, 1.0, 0.0).astype(BF16)
                cnt = seen + _dot(prefix, jnp.where(eq, 1.0, 0.0).astype(BF16))
                b_ref[0, 0, c] = jnp.where(((x > thr) | (eq & (cnt <= need))) & causal(c), 0.0, NEG)

            return total
        lax.fori_loop(0, n_chunks, body, jnp.zeros((1, tq), F32))

    def fill(c, carry):
        b_ref[0, 0, c] = jnp.full((kc_n, tq), NEG, F32)
        return carry

    lax.fori_loop(n_chunks, n_chunks_total, fill, 0)


def _dsa_index(qit, wit, ki, batch, seq):
    tq = Q_TILE
    nq = seq // tq
    nch = seq // DSA_K_CHUNK
    n_sel = min(DSA_TOPK, seq // 4)
    return pl.pallas_call(
        functools.partial(_dsa_index_kernel, n_sel=n_sel),
        grid=(batch, nq),
        in_specs=[pl.BlockSpec((1, qit.shape[1], tq), lambda b, i: (b, 0, i)),
                  pl.BlockSpec((1, IDX_HEADS, tq), lambda b, i: (b, 0, i)),
                  pl.BlockSpec((seq, LANES), lambda b, i: (b, 0))],
        out_specs=pl.BlockSpec((1, 1, nch, DSA_K_CHUNK, tq), lambda b, i: (b, i, 0, 0, 0)),
        out_shape=jax.ShapeDtypeStruct((batch, nq, nch, DSA_K_CHUNK, tq), F32),
        scratch_shapes=[pltpu.VMEM((nch, DSA_K_CHUNK, tq), F32), pltpu.VMEM((nch, DSA_K_CHUNK, tq), BF16)],
        compiler_params=_params(2),
        name="dsa_index",
    )(qit, wit, ki)


def _dsa_attn_kernel(qt_ref, k_ref, vt_ref, b_ref, o_ref, qm_ref, s_ref, cm_ref, m_ref, acc_ref):
    i = pl.program_id(1)
    tq = qt_ref.shape[2]
    kc_n = DSA_K_CHUNK
    n_heads = acc_ref.shape[0]
    n_chunks = (i * tq + tq + kc_n - 1) // kc_n
    _softmax_init(m_ref, acc_ref)
    for h in range(n_heads):
        qm_ref[h] = _head_rows(qt_ref[0, (h // 2) * LANES:(h // 2 + 1) * LANES, :], h % 2)

    def chunk_scores(c):
        st = pl.multiple_of(c * kc_n, kc_n)

        def scores(h):
            return _dot(k_ref[pl.ds(st, kc_n), (h // 2) * LANES:(h // 2 + 1) * LANES], qm_ref[h]) + b_ref[0, 0, c]

        return scores

    _stage_scores(chunk_scores(0), range(s_ref.shape[0] - 1), s_ref, cm_ref)

    def body(c, carry):
        def values(h):
            return vt_ref[0, c, h * HEAD_DIM:(h + 1) * HEAD_DIM, :]

        _attend_chunk(n_heads, chunk_scores(c), chunk_scores(jnp.minimum(c + 1, n_chunks - 1)), values,
                      s_ref, cm_ref, m_ref, acc_ref)
        return carry

    lax.fori_loop(0, n_chunks, body, 0)
    _write_heads(o_ref, acc_ref)


def _dsa_attn(qt, k, vt, bias, batch, seq):
    t, width = k.shape
    tq = Q_TILE
    nq = seq // tq
    n_heads = width // HEAD_DIM
    one = pl.Buffered(1)
    return pl.pallas_call(
        _dsa_attn_kernel,
        grid=(batch, nq),
        in_specs=[pl.BlockSpec((1, width, tq), lambda b, i: (b, 0, i)),
                  pl.BlockSpec((seq, width), lambda b, i: (b, 0), pipeline_mode=one),
                  pl.BlockSpec((1,) + vt.shape[1:], lambda b, i: (b, 0, 0, 0), pipeline_mode=one),
                  pl.BlockSpec((1, 1) + bias.shape[2:], lambda b, i: (b, i, 0, 0, 0))],
        out_specs=pl.BlockSpec((tq, width), lambda b, i: (b * nq + i, 0)),
        out_shape=jax.ShapeDtypeStruct((t, width), BF16),
        scratch_shapes=[pltpu.VMEM((n_heads, LANES, tq), BF16), pltpu.VMEM((SCORE_SLOTS, DSA_K_CHUNK, tq), F32),
                        pltpu.VMEM((SCORE_SLOTS, SUBLANES, tq), F32),
                        pltpu.VMEM((n_heads, 1, tq), F32), pltpu.VMEM((n_heads, HEAD_DIM + ONES_ROWS, tq), F32)],
        compiler_params=_params(2),
        name="dsa_attn",
    )(qt, k, vt, bias)


def _rope_tables(positions):
    inv_freq = ROPE_THETA ** (-jnp.arange(0, ROPE_DIM, 2, dtype=F32) / ROPE_DIM)
    m = jnp.arange(LANES) % HEAD_DIM
    ang = positions.astype(F32).reshape(-1, 1) * inv_freq[m % ROPE_HALF][None, :]
    cos, sin = jnp.cos(ang), jnp.sin(ang)
    c = jnp.where(m < ROPE_DIM, cos, 1.0)
    sa = jnp.where(m < ROPE_HALF, -sin, 0.0)
    sb = jnp.where((m >= ROPE_HALF) & (m < ROPE_DIM), sin, 0.0)
    return c, sa, sb


def kernel(x, p, positions, g_mix, w_in_even, a_ln_g, a_ln_b, a_w_s, a_b_s, w_out_even, w_in_odd, c_kidx_ln_g, c_kidx_ln_b, w_out_odd, g_ffn, w_ffn_gate, w_ffn_up, w_ffn_down, g_ple, w_ple_proj, w_ple_gate, g_final):
    batch, seq, d = x.shape
    depth = p.shape[0]
    t = batch * seq
    assert seq % ROW_TILE == 0 and seq % DSA_K_CHUNK == 0 and min(DSA_TOPK, seq // 4) <= DSA_K_CHUNK
    c, sa, sb = _rope_tables(positions)
    h = x.reshape(t, d)
    p2 = p.reshape(depth, t, p.shape[-1])
    row = lambda a: a.reshape(1, -1)
    onehot = (jnp.arange(seq)[:, None] // MOBA_BLOCK == jnp.arange(LANES)[None, :]).astype(BF16)
    bf = lambda w: w.astype(BF16)
    wo_even, wo_odd, wg, wu, wd, wpg, wpp = map(bf, (w_out_even, w_out_odd, w_ffn_gate, w_ffn_up, w_ffn_down,
                                                     w_ple_gate, w_ple_proj))
    for i in range(depth):
        j = i // 2
        if i % 2 == 0:
            bs = jnp.repeat(a_b_s[j].T, HEAD_DIM, axis=1)
            ya, qt, k, vt, km = _front0(h, row(g_mix[i]), bf(w_in_even[j]), c, sa, sb,
                                        row(a_ln_g[j]), row(a_ln_b[j]), a_w_s[j], bs, batch, seq)
            ys = [ya, _moba(qt, k, vt, km.reshape(t // MOBA_BLOCK, WIDTH_B), onehot, batch, seq)]
            wo = wo_even
        else:
            w = w_in_odd[j]
            o = 3 * WIDTH_C + IDX_HEADS * IDX_DIM
            w1 = bf(jnp.concatenate([w[:, :o], w[:, o:o + IDX_DIM], w[:, o:o + IDX_DIM], w[:, o + IDX_DIM:],
                                     jnp.zeros((d, LANES - IDX_HEADS), w.dtype)], axis=1))
            qt, k, vt, qit, ki, wit = _front1(h, row(g_mix[i]), w1, c, sa, sb, row(jnp.tile(c_kidx_ln_g[j], 2)),
                                              row(jnp.tile(c_kidx_ln_b[j], 2)), batch, seq)
            ys = [_dsa_attn(qt, k, vt, _dsa_index(qit, wit, ki, batch, seq), batch, seq)]
            wo = wo_odd
        h = _post(h, ys, wo, j, p2, i, row(g_ffn[i]), wg, wu, wd, row(g_ple[i]), wpg, wpp, row(g_final),
                  final=(i == depth - 1))
    return h.reshape(batch, seq, d)
```

```python
import functools

import jax
import jax.numpy as jnp
from jax import lax
from jax.experimental import pallas as pl
from jax.experimental.pallas import tpu as pltpu

HEAD_DIM = 64
ROPE_DIM = HEAD_DIM // 4
ROPE_HALF = ROPE_DIM // 2
ROPE_THETA = 500000.0
N_GROUPS_A = 8
WIDTH_A = N_GROUPS_A * HEAD_DIM
N_HEADS_B = 8
WIDTH_B = N_HEADS_B * HEAD_DIM
N_HEADS_C = 16
WIDTH_C = N_HEADS_C * HEAD_DIM
SGU_CHUNK = 128
MOBA_BLOCK = 256
MOBA_TOPK = 3
IDX_HEADS = 8
IDX_DIM = 64
DSA_TOPK = 256
EPS = 1e-6
NEG = -1e30
LOG2E = 1.4426950408889634
BF16_SPACING = 2.0 ** -7
TINY = 1e-30

LANES = 128
SUBLANES = 8
ROW_TILE = 512
Q_TILE = 256
DSA_K_CHUNK = 512
FOLD_CHAINS = 8
ONES_ROWS = 16
SCORE_SLOTS = 4
MOBA_SLOTS = 4
MOBA_K_CHUNK = 2 * MOBA_BLOCK
COARSE_STEPS = 10
BISECT_PER_ROUND = 8
BISECT_ROUNDS = 12
VMEM_LIMIT = 56 * 1024 * 1024

F32 = jnp.float32
BF16 = jnp.bfloat16


def _dot(a, b):
    return jnp.dot(a, b, preferred_element_type=F32)


def _rms(x, g):
    return x * lax.rsqrt(jnp.mean(x * x, axis=-1, keepdims=True) + EPS) * g


def _rope(z, c, sa, sb):
    outs = []
    for j in range(z.shape[1] // LANES):
        zz = z[:, j * LANES:(j + 1) * LANES]
        outs.append(zz * c + pltpu.roll(zz, LANES - ROPE_HALF, 1) * sa + pltpu.roll(zz, ROPE_HALF, 1) * sb)
    return outs[0] if len(outs) == 1 else jnp.concatenate(outs, axis=1)


def _gelu(x):
    return 0.5 * x * (1.0 + lax.erf(x * (2.0 ** -0.5)))


def _fold_rows(x, op, group=SUBLANES):
    parts = [x[a:a + group] for a in range(0, x.shape[0], group)]
    chains = parts[:FOLD_CHAINS]
    for a, part in enumerate(parts[FOLD_CHAINS:]):
        chains[a % FOLD_CHAINS] = op(chains[a % FOLD_CHAINS], part)
    while len(chains) > 1:
        chains = [op(chains[a], chains[a + 1]) if a + 1 < len(chains) else chains[a] for a in range(0, len(chains), 2)]
    return chains[0]


def _head_rows(x, hh):
    r = lax.broadcasted_iota(jnp.int32, (LANES, 1), 0)
    return jnp.where((r >= hh * HEAD_DIM) & (r < (hh + 1) * HEAD_DIM), x, jnp.zeros_like(x))


def _const_spec(shape):
    nd = len(shape)
    return pl.BlockSpec(shape, lambda *_: (0,) * nd, pipeline_mode=pl.Buffered(1))


def _params(n_axes):
    return pltpu.CompilerParams(dimension_semantics=("arbitrary",) * n_axes, vmem_limit_bytes=VMEM_LIMIT)


def _stage_scores(score_fn, heads, s_ref, cm_ref):
    n_slots = s_ref.shape[0]
    for h in heads:
        s = score_fn(h)
        s_ref[h % n_slots] = s
        cm_ref[h % n_slots] = _fold_rows(s, jnp.maximum)


def _attend_chunk(n_heads, score_fn, next_score_fn, vt_fn, s_ref, cm_ref, m_ref, acc_ref):
    n_slots = s_ref.shape[0]
    ahead = n_slots - 1
    assert n_heads % n_slots == 0
    for h in range(n_heads):
        if h + ahead < n_heads:
            _stage_scores(score_fn, [h + ahead], s_ref, cm_ref)
        else:
            _stage_scores(next_score_fn, [h + ahead - n_heads], s_ref, cm_ref)
        slot = h % n_slots
        m_old = m_ref[h]
        m_new = jnp.maximum(m_old, jnp.max(cm_ref[slot], axis=0, keepdims=True))
        p = jnp.exp2(s_ref[slot] - m_new).astype(BF16)
        m_ref[h] = m_new
        vt = vt_fn(h)
        vt = jnp.concatenate([vt, jnp.ones((acc_ref.shape[1] - vt.shape[0], vt.shape[1]), BF16)], axis=0)
        acc_ref[h] = jnp.exp2(m_old - m_new) * acc_ref[h] + _dot(vt, p)


def _softmax_init(m_ref, acc_ref):
    m_ref[...] = jnp.full(m_ref.shape, -jnp.inf, F32)
    acc_ref[...] = jnp.zeros(acc_ref.shape, F32)


def _write_heads(o_ref, acc_ref):
    def head(h):
        return acc_ref[h, :HEAD_DIM, :] / acc_ref[h, HEAD_DIM:HEAD_DIM + 1, :]

    for p in range(acc_ref.shape[0] // 2):
        o_ref[:, p * LANES:(p + 1) * LANES] = jnp.concatenate([head(2 * p), head(2 * p + 1)], axis=0).T.astype(BF16)


def _front0_kernel(h_ref, g_ref, w_ref, c_ref, sa_ref, sb_ref, lng_ref, lnb_ref, ws_ref, bs_ref,
                   ya_ref, qt_ref, k_ref, vt_ref, km_ref):
    tm = h_ref.shape[0]
    xn = _rms(h_ref[...], g_ref[...]).astype(BF16)

    def proj(lo, hi):
        return _dot(xn, w_ref[:, lo:hi])

    u = _gelu(proj(0, WIDTH_A))
    v = _gelu(proj(WIDTH_A, 2 * WIDTH_A))
    lane = lax.broadcasted_iota(jnp.int32, (1, LANES), 1)
    low = lane < HEAD_DIM
    row = lax.broadcasted_iota(jnp.int32, (SGU_CHUNK, SGU_CHUNK), 0)
    col = lax.broadcasted_iota(jnp.int32, (SGU_CHUNK, SGU_CHUNK), 1)
    tril = col <= row
    for j in range(WIDTH_A // LANES):
        sl = slice(j * LANES, (j + 1) * LANES)
        vv = v[:, sl]

        def half_mean(t):
            s_lo = jnp.sum(jnp.where(low, t, 0.0), axis=-1, keepdims=True)
            s_hi = jnp.sum(jnp.where(low, 0.0, t), axis=-1, keepdims=True)
            return jnp.where(low, s_lo, s_hi) * (1.0 / HEAD_DIM)

        xc = vv - half_mean(vv)
        var = half_mean(xc * xc)
        vn = (xc * lax.rsqrt(var + EPS) * lng_ref[:, sl] + lnb_ref[:, sl]).astype(BF16)
        w_lo = jnp.where(tril, ws_ref[2 * j], 0.0).astype(BF16)
        w_hi = jnp.where(tril, ws_ref[2 * j + 1], 0.0).astype(BF16)
        for c in range(tm // SGU_CHUNK):
            rs = slice(c * SGU_CHUNK, (c + 1) * SGU_CHUNK)
            vc = vn[rs]
            mixed = jnp.where(low, _dot(w_lo, vc), _dot(w_hi, vc)) + bs_ref[:, sl]
            ya_ref[rs, sl] = (u[rs, sl] * mixed).astype(BF16)

    o = 2 * WIDTH_A
    c, sa, sb = c_ref[...], sa_ref[...], sb_ref[...]
    q = _rope(proj(o, o + WIDTH_B), c, sa, sb) * (HEAD_DIM ** -0.5 * LOG2E)
    qt_ref[0] = q.T.astype(BF16)
    k = _rope(proj(o + WIDTH_B, o + 2 * WIDTH_B), c, sa, sb)
    k_ref[...] = k.astype(BF16)
    v_b = proj(o + 2 * WIDTH_B, o + 3 * WIDTH_B)
    for blk in range(tm // MOBA_BLOCK):
        rs = slice(blk * MOBA_BLOCK, (blk + 1) * MOBA_BLOCK)
        km_ref[0, blk:blk + 1, :] = jnp.mean(k[rs], axis=0, keepdims=True)
        vt_ref[0, blk] = v_b[rs].T.astype(BF16)


def _front0(h, g, w, c, sa, sb, lng, lnb, ws, bs, batch, seq):
    t, d = h.shape
    tm = ROW_TILE
    nt = seq // tm
    nblk = tm // MOBA_BLOCK
    row = lambda n: pl.BlockSpec((tm, n), lambda i: (i, 0))
    return pl.pallas_call(
        _front0_kernel,
        grid=(t // tm,),
        in_specs=[row(d), _const_spec((1, d)), _const_spec(w.shape), row(LANES), row(LANES), row(LANES),
                  _const_spec(lng.shape), _const_spec(lnb.shape), _const_spec(ws.shape), _const_spec(bs.shape)],
        out_specs=[row(WIDTH_A),
                   pl.BlockSpec((1, WIDTH_B, tm), lambda i: (i // nt, 0, i % nt)),
                   row(WIDTH_B),
                   pl.BlockSpec((1, nblk, WIDTH_B, MOBA_BLOCK), lambda i: (i // nt, i % nt, 0, 0)),
                   pl.BlockSpec((1, nblk, WIDTH_B), lambda i: (i, 0, 0))],
        out_shape=[jax.ShapeDtypeStruct((t, WIDTH_A), BF16),
                   jax.ShapeDtypeStruct((batch, WIDTH_B, seq), BF16),
                   jax.ShapeDtypeStruct((t, WIDTH_B), BF16),
                   jax.ShapeDtypeStruct((batch, seq // MOBA_BLOCK, WIDTH_B, MOBA_BLOCK), BF16),
                   jax.ShapeDtypeStruct((t // tm, nblk, WIDTH_B), F32)],
        compiler_params=_params(1),
        name="front0",
    )(h, g, w, c, sa, sb, lng, lnb, ws, bs)


def _moba_kernel(qt_ref, k_ref, vt_ref, km_ref, oh_ref, o_ref, qc_ref, s_ref, cm_ref, m_ref, acc_ref):
    i = pl.program_id(1)
    tq = qt_ref.shape[2]
    nb = km_ref.shape[0]
    n_heads = acc_ref.shape[0]
    kc_n = s_ref.shape[1]
    per_chunk = kc_n // MOBA_BLOCK
    blk = lax.broadcasted_iota(jnp.int32, (nb, 1), 0)
    blk_f = blk.astype(F32)
    past = blk < i
    _softmax_init(m_ref, acc_ref)
    for h in range(n_heads):
        sl = slice((h // 2) * LANES, (h // 2 + 1) * LANES)
        qm = _head_rows(qt_ref[0, sl, :], h % 2)
        g = jnp.where(past, _dot(km_ref[:, sl].astype(BF16), qm), NEG)
        sel = jnp.zeros(g.shape, jnp.bool_)
        for _ in range(min(MOBA_TOPK, nb)):
            top = jnp.max(g, axis=0, keepdims=True)
            idx = jnp.min(jnp.where(g == top, blk_f, 1e9), axis=0, keepdims=True)
            pick = blk_f == idx
            sel = sel | pick
            g = jnp.where(pick, -jnp.inf, g)
        selb = jnp.where((sel & past) | (blk == i), 0.0, NEG).astype(BF16)
        qc_ref[h] = jnp.concatenate([qm, selb, jnp.zeros((LANES - nb, tq), BF16)], axis=0)

    def values(c):
        def vt(h):
            rows = slice(h * HEAD_DIM, (h + 1) * HEAD_DIM)
            return jnp.concatenate([vt_ref[0, per_chunk * c + a, rows, :] for a in range(per_chunk)], axis=1)
        return vt

    def chunk_scores(c, mask=None):
        st = pl.multiple_of(c * kc_n, kc_n)

        def scores(h):
            kc = jnp.concatenate([k_ref[pl.ds(st, kc_n), (h // 2) * LANES:(h // 2 + 1) * LANES],
                                  oh_ref[pl.ds(st, kc_n), :]], axis=1)
            s = _dot(kc, qc_ref[h])
            return s if mask is None else jnp.where(mask, s, NEG)

        return scores

    own = i // per_chunk
    kpos = own * kc_n + lax.broadcasted_iota(jnp.int32, (kc_n, 1), 0)
    qpos = i * tq + lax.broadcasted_iota(jnp.int32, (1, tq), 1)
    own_scores = chunk_scores(own, kpos <= qpos)
    refs = (s_ref, cm_ref, m_ref, acc_ref)
    last = jnp.maximum(own - 1, 0)
    _stage_scores(own_scores, range(s_ref.shape[0] - 1), s_ref, cm_ref)
    _attend_chunk(n_heads, own_scores, chunk_scores(0), values(own), *refs)

    def body(c, carry):
        _attend_chunk(n_heads, chunk_scores(c), chunk_scores(jnp.minimum(c + 1, last)), values(c), *refs)
        return carry

    lax.fori_loop(0, own, body, 0)
    _write_heads(o_ref, acc_ref)


def _moba(qt, k, vt, km, onehot, batch, seq):
    t, width = k.shape
    tq = Q_TILE
    assert tq == MOBA_BLOCK
    nb = seq // MOBA_BLOCK
    nq = seq // tq
    n_heads = width // HEAD_DIM
    one = pl.Buffered(1)
    return pl.pallas_call(
        _moba_kernel,
        grid=(batch, nq),
        in_specs=[pl.BlockSpec((1, width, tq), lambda b, i: (b, 0, i)),
                  pl.BlockSpec((seq, width), lambda b, i: (b, 0), pipeline_mode=one),
                  pl.BlockSpec((1, nb, width, MOBA_BLOCK), lambda b, i: (b, 0, 0, 0), pipeline_mode=one),
                  pl.BlockSpec((nb, width), lambda b, i: (b, 0)),
                  _const_spec(onehot.shape)],
        out_specs=pl.BlockSpec((tq, width), lambda b, i: (b * nq + i, 0)),
        out_shape=jax.ShapeDtypeStruct((t, width), BF16),
        scratch_shapes=[pltpu.VMEM((n_heads, 2 * LANES, tq), BF16), pltpu.VMEM((MOBA_SLOTS, MOBA_K_CHUNK, tq), F32),
                        pltpu.VMEM((MOBA_SLOTS, SUBLANES, tq), F32),
                        pltpu.VMEM((n_heads, 1, tq), F32), pltpu.VMEM((n_heads, HEAD_DIM + ONES_ROWS, tq), F32)],
        compiler_params=_params(2),
        name="moba",
    )(qt, k, vt, km, onehot)


def _post_kernel(*refs, n_mix, final):
    h_ref = refs[0]
    y_refs = refs[1:1 + n_mix]
    (wo_ref, p_ref, gf_ref, wg_ref, wu_ref, wd_ref, gp_ref, wpg_ref, wpp_ref, gl_ref, o_ref, a_ref) = refs[1 + n_mix:]
    h = h_ref[...]
    lo = 0
    for y_ref in y_refs:
        h = h + _dot(y_ref[...], wo_ref[0, lo:lo + y_ref.shape[1], :])
        lo += y_ref.shape[1]
    xn = _rms(h, gf_ref[...]).astype(BF16)
    d_ff = wg_ref.shape[2]
    step = 2 * LANES
    for lo in range(0, d_ff, step):
        gate = _dot(xn, wg_ref[0, :, lo:lo + step])
        up = _dot(xn, wu_ref[0, :, lo:lo + step])
        a_ref[:, lo:lo + step] = (gate * jax.nn.sigmoid(gate) * up).astype(BF16)
    h = h + _dot(a_ref[...], wd_ref[0])
    pg = jax.nn.sigmoid(_dot(_rms(h, gp_ref[...]).astype(BF16), wpg_ref[0]))
    h = h + _dot(p_ref[0].astype(BF16), wpp_ref[0]) * pg
    o_ref[...] = _rms(h, gl_ref[...]) if final else h


def _layer_spec(shape, layer):
    rest = (0,) * (len(shape) - 1)
    return pl.BlockSpec((1,) + tuple(shape[1:]), lambda *_: (layer,) + rest, pipeline_mode=pl.Buffered(1))


def _post(h, ys, wo, wo_layer, p, layer, gf, wg, wu, wd, gp, wpg, wpp, gl, final):
    t, d = h.shape
    tm = ROW_TILE
    row = lambda n: pl.BlockSpec((tm, n), lambda i: (i, 0))
    stack = lambda w: _layer_spec(w.shape, layer)
    return pl.pallas_call(
        functools.partial(_post_kernel, n_mix=len(ys), final=final),
        grid=(t // tm,),
        in_specs=[row(d)] + [row(y.shape[1]) for y in ys]
                 + [_layer_spec(wo.shape, wo_layer), pl.BlockSpec((1, tm, p.shape[2]), lambda i: (layer, i, 0)),
                    _const_spec(gf.shape), stack(wg), stack(wu), stack(wd), _const_spec(gp.shape), stack(wpg),
                    stack(wpp), _const_spec(gl.shape)],
        out_specs=row(d),
        out_shape=jax.ShapeDtypeStruct((t, d), F32),
        scratch_shapes=[pltpu.VMEM((tm, wg.shape[2]), BF16)],
        compiler_params=_params(1),
        name="post_final" if final else "post",
    )(h, *ys, wo, p, gf, wg, wu, wd, gp, wpg, wpp, gl)


def _front1_kernel(h_ref, g_ref, w_ref, c_ref, sa_ref, sb_ref, lng_ref, lnb_ref,
                   qt_ref, k_ref, vt_ref, qit_ref, ki_ref, wit_ref):
    xn = _rms(h_ref[...], g_ref[...]).astype(BF16)

    def proj(lo, hi):
        return _dot(xn, w_ref[:, lo:hi])

    c, sa, sb = c_ref[...], sa_ref[...], sb_ref[...]
    qt_ref[0] = (_rope(proj(0, WIDTH_C), c, sa, sb) * (HEAD_DIM ** -0.5 * LOG2E)).T.astype(BF16)
    k_ref[...] = _rope(proj(WIDTH_C, 2 * WIDTH_C), c, sa, sb).astype(BF16)
    vt_ref[0, 0] = proj(2 * WIDTH_C, 3 * WIDTH_C).T.astype(BF16)
    o = 3 * WIDTH_C
    qit_ref[0] = _rope(proj(o, o + IDX_HEADS * IDX_DIM), c, sa, sb).T.astype(BF16)
    o += IDX_HEADS * IDX_DIM
    ki = proj(o, o + LANES)
    xc = ki - jnp.mean(ki, axis=-1, keepdims=True)
    var = jnp.mean(xc * xc, axis=-1, keepdims=True)
    ki = xc * lax.rsqrt(var + EPS) * lng_ref[...] + lnb_ref[...]
    ki_ref[...] = _rope(ki, c, sa, sb).astype(BF16)
    wi = proj(o + LANES, o + 2 * LANES) * ((IDX_HEADS ** -0.5) * (IDX_DIM ** -0.5))
    wit_ref[0] = wi.T[:IDX_HEADS]


def _front1(h, g, w, c, sa, sb, lng, lnb, batch, seq):
    t, d = h.shape
    tm = ROW_TILE
    assert tm == DSA_K_CHUNK
    nt = seq // tm
    row = lambda n: pl.BlockSpec((tm, n), lambda i: (i, 0))
    tr = lambda n: pl.BlockSpec((1, n, tm), lambda i: (i // nt, 0, i % nt))
    return pl.pallas_call(
        _front1_kernel,
        grid=(t // tm,),
        in_specs=[row(d), _const_spec((1, d)), _const_spec(w.shape), row(LANES), row(LANES), row(LANES),
                  _const_spec(lng.shape), _const_spec(lnb.shape)],
        out_specs=[tr(WIDTH_C), row(WIDTH_C),
                   pl.BlockSpec((1, 1, WIDTH_C, tm), lambda i: (i // nt, i % nt, 0, 0)),
                   tr(IDX_HEADS * IDX_DIM), row(LANES), tr(IDX_HEADS)],
        out_shape=[jax.ShapeDtypeStruct((batch, WIDTH_C, seq), BF16),
                   jax.ShapeDtypeStruct((t, WIDTH_C), BF16),
                   jax.ShapeDtypeStruct((batch, nt, WIDTH_C, tm), BF16),
                   jax.ShapeDtypeStruct((batch, IDX_HEADS * IDX_DIM, seq), BF16),
                   jax.ShapeDtypeStruct((t, LANES), BF16),
                   jax.ShapeDtypeStruct((batch, IDX_HEADS, seq), F32)],
        compiler_params=_params(1),
        name="front1",
    )(h, g, w, c, sa, sb, lng, lnb)


def _dsa_index_kernel(qit_ref, wit_ref, ki_ref, b_ref, sc_ref, sc16_ref, *, n_sel):
    i = pl.program_id(1)
    tq = qit_ref.shape[2]
    kc_n = DSA_K_CHUNK
    n_chunks_total = sc_ref.shape[0]
    n_chunks = (i * tq + tq + kc_n - 1) // kc_n
    qpos = i * tq + lax.broadcasted_iota(jnp.int32, (1, tq), 1)
    kofs = lax.broadcasted_iota(jnp.int32, (kc_n, 1), 0)
    wv = wit_ref[0]

    def causal(c):
        return (c * kc_n + kofs) <= qpos

    big = 3e38

    def score_chunk(c, carry):
        mx, mn = carry
        kc = ki_ref[pl.ds(pl.multiple_of(c * kc_n, kc_n), kc_n), :]
        acc = jnp.zeros((kc_n, tq), F32)
        for h in range(IDX_HEADS):
            qm = _head_rows(qit_ref[0, (h // 2) * LANES:(h // 2 + 1) * LANES, :], h % 2)
            acc = acc + jnp.maximum(_dot(kc, qm), 0.0) * wv[h:h + 1, :]
        ok = causal(c)
        sc_ref[c] = jnp.where(ok, acc, NEG)
        sc16_ref[c] = jnp.where(ok, acc, NEG).astype(BF16)
        return (jnp.maximum(mx, _fold_rows(jnp.where(ok, acc, -big), jnp.maximum)),
                jnp.minimum(mn, _fold_rows(jnp.where(ok, acc, big), jnp.minimum)))

    mx, mn = lax.fori_loop(0, n_chunks, score_chunk,
                           (jnp.full((SUBLANES, tq), -big, F32), jnp.full((SUBLANES, tq), big, F32)))
    col_max = jnp.max(mx, axis=0, keepdims=True)
    col_min = jnp.min(mn, axis=0, keepdims=True)

    def reduce_chunks(fn, op, init):
        def body(c, acc):
            return op(acc, _fold_rows(fn(sc_ref[c]), op))
        return lax.fori_loop(0, n_chunks, body, jnp.full((SUBLANES, tq), init, F32))

    def count_ge(thr):
        acc = reduce_chunks(lambda x: jnp.where(x >= thr, 1.0, 0.0), jnp.add, 0.0)
        return jnp.sum(acc, axis=0, keepdims=True)

    def pending(flags):
        return jnp.max(jnp.where(flags, 0.0, 1.0))

    k_f = float(n_sel)
    n_causal = (qpos + 1).astype(F32)
    small = n_causal < k_f
    rows16 = 2 * SUBLANES

    def count_ge_bf16(thr):
        one, zero = jnp.ones((), BF16), jnp.zeros((), BF16)

        def body(c, acc):
            hits = jnp.where(sc16_ref[c] >= thr, one, zero)
            return acc + _fold_rows(hits, jnp.add, rows16).astype(F32)

        return jnp.sum(lax.fori_loop(0, n_chunks, body, jnp.zeros((rows16, tq), F32)), axis=0, keepdims=True)

    def coarse(_, carry):
        lo, hi = carry
        mid = (0.5 * (lo + hi)).astype(BF16)
        ok = count_ge_bf16(mid) >= k_f
        mid = mid.astype(F32)
        return jnp.where(ok, mid, lo), jnp.where(ok, hi, mid)

    top = col_max.astype(BF16).astype(F32)
    lo, hi = lax.fori_loop(0, COARSE_STEPS, coarse,
                           (col_min.astype(BF16).astype(F32),
                            (top + jnp.abs(top) * 2.0 * BF16_SPACING + TINY).astype(BF16).astype(F32)))
    lo = jnp.where(small, NEG, lo - jnp.abs(lo) * BF16_SPACING - TINY)
    c_lo = jnp.full((1, tq), 2.0 * k_f, F32)
    c_hi = jnp.full((1, tq), -1.0, F32)

    def bisect(_, carry):
        lo, hi, c_lo, c_hi = carry
        mid = 0.5 * (lo + hi)
        cnt = count_ge(mid)
        ok = cnt >= k_f
        return (jnp.where(ok, mid, lo), jnp.where(ok, hi, mid), jnp.where(ok, cnt, c_lo), jnp.where(ok, c_hi, cnt))

    def bracket_ends(lo, hi):
        def body(c, carry):
            below, above = carry
            x = sc_ref[c]
            return (jnp.maximum(below, _fold_rows(jnp.where(x < hi, x, -big), jnp.maximum)),
                    jnp.minimum(above, _fold_rows(jnp.where(x >= lo, x, big), jnp.minimum)))

        below, above = lax.fori_loop(0, n_chunks, body, (jnp.full((SUBLANES, tq), -big, F32),
                                                         jnp.full((SUBLANES, tq), big, F32)))
        return jnp.max(below, axis=0, keepdims=True), jnp.min(above, axis=0, keepdims=True)

    def settled(c_lo, c_hi, below, above):
        return small | (c_hi == k_f - 1.0) | (c_lo == k_f) | (below == above)

    def narrow_cond(carry):
        return (carry[6] < BISECT_ROUNDS) & (carry[7] > 0.5)

    def narrow(carry):
        state = lax.fori_loop(0, BISECT_PER_ROUND, bisect, carry[:4])
        below, above = bracket_ends(state[0], state[1])
        return (*state, below, above, carry[6] + 1, pending(settled(state[2], state[3], below, above)))

    zeros = jnp.zeros((1, tq), F32)
    lo, hi, c_lo, c_hi, below, above, _, _ = lax.while_loop(
        narrow_cond, narrow, (lo, hi, c_lo, c_hi, zeros, zeros, jnp.int32(0), jnp.float32(1.0)))
    thr = jnp.where(small, NEG, jnp.where(c_lo == k_f, above, below))
    done = jnp.where(settled(c_lo, c_hi, below, above), 1.0, 0.0)

    def snap_cond(carry):
        return carry[3] > 0.5

    def snap(carry):
        hi, thr, done, _ = carry
        cand = jnp.max(reduce_chunks(lambda x: jnp.where(x < hi, x, -big), jnp.maximum, -big),
                       axis=0, keepdims=True)
        ok = count_ge(cand) >= k_f
        thr = jnp.where(done > 0.5, thr, cand)
        done = jnp.where(ok, 1.0, done)
        hi = jnp.where(done > 0.5, hi, cand)
        return hi, thr, done, jnp.max(1.0 - done)

    _, thr, _, _ = lax.while_loop(snap_cond, snap, (hi, thr, done, jnp.max(1.0 - done)))

    n_ge = count_ge(thr)
    tied = jnp.max(jnp.where((n_ge > k_f) & jnp.logical_not(small), 1.0, 0.0))

    @pl.when(tied < 0.5)
    def _():
        def body(c, carry):
            x = sc_ref[c]
            b_ref[0, 0, c] = jnp.where((x >= thr) & causal(c), 0.0, NEG)
            return carry
        lax.fori_loop(0, n_chunks, body, 0)

    @pl.when(tied > 0.5)
    def _():
        n_gt = reduce_chunks(lambda x: jnp.where(x > thr, 1.0, 0.0), jnp.add, 0.0)
        need = k_f - jnp.sum(n_gt, axis=0, keepdims=True)

        def body(c, seen):
            x = sc_ref[c]
            eq = x == thr
            total = seen + jnp.sum(_fold_rows(jnp.where(eq, 1.0, 0.0), jnp.add), axis=0, keepdims=True)
            keep_all = total <= need
            crossing = jnp.max(jnp.where(keep_all | (seen >= need), 0.0, 1.0))

            @pl.when(crossing < 0.5)
            def _():
                b_ref[0, 0, c] = jnp.where(((x > thr) | (eq & keep_all)) & causal(c), 0.0, NEG)

            @pl.when(crossing > 0.5)
            def _():
                r = lax.broadcasted_iota(jnp.int32, (kc_n, kc_n), 0)
                cc = lax.broadcasted_iota(jnp.int32, (kc_n, kc_n), 1)
                prefix = jnp.where(cc <= r, 1.0, 0.0).astype(BF16)
                cnt = seen + _dot(prefix, jnp.where(eq, 1.0, 0.0).astype(BF16))
                b_ref[0, 0, c] = jnp.where(((x > thr) | (eq & (cnt <= need))) & causal(c), 0.0, NEG)

            return total
        lax.fori_loop(0, n_chunks, body, jnp.zeros((1, tq), F32))

    def fill(c, carry):
        b_ref[0, 0, c] = jnp.full((kc_n, tq), NEG, F32)
        return carry

    lax.fori_loop(n_chunks, n_chunks_total, fill, 0)


def _dsa_index(qit, wit, ki, batch, seq):
    tq = Q_TILE
    nq = seq // tq
    nch = seq // DSA_K_CHUNK
    n_sel = min(DSA_TOPK, seq // 4)
    return pl.pallas_call(
        functools.partial(_dsa_index_kernel, n_sel=n_sel),
        grid=(batch, nq),
        in_specs=[pl.BlockSpec((1, qit.shape[1], tq), lambda b, i: (b, 0, i)),
                  pl.BlockSpec((1, IDX_HEADS, tq), lambda b, i: (b, 0, i)),
                  pl.BlockSpec((seq, LANES), lambda b, i: (b, 0))],
        out_specs=pl.BlockSpec((1, 1, nch, DSA_K_CHUNK, tq), lambda b, i: (b, i, 0, 0, 0)),
        out_shape=jax.ShapeDtypeStruct((batch, nq, nch, DSA_K_CHUNK, tq), F32),
        scratch_shapes=[pltpu.VMEM((nch, DSA_K_CHUNK, tq), F32), pltpu.VMEM((nch, DSA_K_CHUNK, tq), BF16)],
        compiler_params=_params(2),
        name="dsa_index",
    )(qit, wit, ki)


def _dsa_attn_kernel(qt_ref, k_ref, vt_ref, b_ref, o_ref, qm_ref, s_ref, cm_ref, m_ref, acc_ref):
    i = pl.program_id(1)
    tq = qt_ref.shape[2]
    kc_n = DSA_K_CHUNK
    n_heads = acc_ref.shape[0]
    n_chunks = (i * tq + tq + kc_n - 1) // kc_n
    _softmax_init(m_ref, acc_ref)
    for h in range(n_heads):
        qm_ref[h] = _head_rows(qt_ref[0, (h // 2) * LANES:(h // 2 + 1) * LANES, :], h % 2)

    def chunk_scores(c):
        st = pl.multiple_of(c * kc_n, kc_n)

        def scores(h):
            return _dot(k_ref[pl.ds(st, kc_n), (h // 2) * LANES:(h // 2 + 1) * LANES], qm_ref[h]) + b_ref[0, 0, c]

        return scores

    _stage_scores(chunk_scores(0), range(s_ref.shape[0] - 1), s_ref, cm_ref)

    def body(c, carry):
        def values(h):
            return vt_ref[0, c, h * HEAD_DIM:(h + 1) * HEAD_DIM, :]

        _attend_chunk(n_heads, chunk_scores(c), chunk_scores(jnp.minimum(c + 1, n_chunks - 1)), values,
                      s_ref, cm_ref, m_ref, acc_ref)
        return carry

    lax.fori_loop(0, n_chunks, body, 0)
    _write_heads(o_ref, acc_ref)


def _dsa_attn(qt, k, vt, bias, batch, seq):
    t, width = k.shape
    tq = Q_TILE
    nq = seq // tq
    n_heads = width // HEAD_DIM
    one = pl.Buffered(1)
    return pl.pallas_call(
        _dsa_attn_kernel,
        grid=(batch, nq),
        in_specs=[pl.BlockSpec((1, width, tq), lambda b, i: (b, 0, i)),
                  pl.BlockSpec((seq, width), lambda b, i: (b, 0), pipeline_mode=one),
                  pl.BlockSpec((1,) + vt.shape[1:], lambda b, i: (b, 0, 0, 0), pipeline_mode=one),
                  pl.BlockSpec((1, 1) + bias.shape[2:], lambda b, i: (b, i, 0, 0, 0))],
        out_specs=pl.BlockSpec((tq, width), lambda b, i: (b * nq + i, 0)),
        out_shape=jax.ShapeDtypeStruct((t, width), BF16),
        scratch_shapes=[pltpu.VMEM((n_heads, LANES, tq), BF16), pltpu.VMEM((SCORE_SLOTS, DSA_K_CHUNK, tq), F32),
                        pltpu.VMEM((SCORE_SLOTS, SUBLANES, tq), F32),
                        pltpu.VMEM((n_heads, 1, tq), F32), pltpu.VMEM((n_heads, HEAD_DIM + ONES_ROWS, tq), F32)],
        compiler_params=_params(2),
        name="dsa_attn",
    )(qt, k, vt, bias)


def _rope_tables(positions):
    inv_freq = ROPE_THETA ** (-jnp.arange(0, ROPE_DIM, 2, dtype=F32) / ROPE_DIM)
    m = jnp.arange(LANES) % HEAD_DIM
    ang = positions.astype(F32).reshape(-1, 1) * inv_freq[m % ROPE_HALF][None, :]
    cos, sin = jnp.cos(ang), jnp.sin(ang)
    c = jnp.where(m < ROPE_DIM, cos, 1.0)
    sa = jnp.where(m < ROPE_HALF, -sin, 0.0)
    sb = jnp.where((m >= ROPE_HALF) & (m < ROPE_DIM), sin, 0.0)
    return c, sa, sb


def kernel(x, p, positions, g_mix, w_in_even, a_ln_g, a_ln_b, a_w_s, a_b_s, w_out_even, w_in_odd, c_kidx_ln_g, c_kidx_ln_b, w_out_odd, g_ffn, w_ffn_gate, w_ffn_up, w_ffn_down, g_ple, w_ple_proj, w_ple_gate, g_final):
    batch, seq, d = x.shape
    depth = p.shape[0]
    t = batch * seq
    assert seq % ROW_TILE == 0 and seq % DSA_K_CHUNK == 0 and min(DSA_TOPK, seq // 4) <= DSA_K_CHUNK
    c, sa, sb = _rope_tables(positions)
    h = x.reshape(t, d)
    p2 = p.reshape(depth, t, p.shape[-1])
    row = lambda a: a.reshape(1, -1)
    onehot = (jnp.arange(seq)[:, None] // MOBA_BLOCK == jnp.arange(LANES)[None, :]).astype(BF16)
    bf = lambda w: w.astype(BF16)
    wo_even, wo_odd, wg, wu, wd, wpg, wpp = map(bf, (w_out_even, w_out_odd, w_ffn_gate, w_ffn_up, w_ffn_down,
                                                     w_ple_gate, w_ple_proj))
    for i in range(depth):
        j = i // 2
        if i % 2 == 0:
            bs = jnp.repeat(a_b_s[j].T, HEAD_DIM, axis=1)
            ya, qt, k, vt, km = _front0(h, row(g_mix[i]), bf(w_in_even[j]), c, sa, sb,
                                        row(a_ln_g[j]), row(a_ln_b[j]), a_w_s[j], bs, batch, seq)
            ys = [ya, _moba(qt, k, vt, km.reshape(t // MOBA_BLOCK, WIDTH_B), onehot, batch, seq)]
            wo = wo_even
        else:
            w = w_in_odd[j]
            o = 3 * WIDTH_C + IDX_HEADS * IDX_DIM
            w1 = bf(jnp.concatenate([w[:, :o], w[:, o:o + IDX_DIM], w[:, o:o + IDX_DIM], w[:, o + IDX_DIM:],
                                     jnp.zeros((d, LANES - IDX_HEADS), w.dtype)], axis=1))
            qt, k, vt, qit, ki, wit = _front1(h, row(g_mix[i]), w1, c, sa, sb, row(jnp.tile(c_kidx_ln_g[j], 2)),
                                              row(jnp.tile(c_kidx_ln_b[j], 2)), batch, seq)
            ys = [_dsa_attn(qt, k, vt, _dsa_index(qit, wit, ki, batch, seq), batch, seq)]
            wo = wo_odd
        h = _post(h, ys, wo, j, p2, i, row(g_ffn[i]), wg, wu, wd, row(g_ple[i]), wpg, wpp, row(g_final),
                  final=(i == depth - 1))
    return h.reshape(batch, seq, d)
```

```python
import functools

import jax
import jax.numpy as jnp
from jax import lax
from jax.experimental import pallas as pl
from jax.experimental.pallas import tpu as pltpu

HEAD_DIM = 64
ROPE_DIM = HEAD_DIM // 4
ROPE_HALF = ROPE_DIM // 2
ROPE_THETA = 500000.0
N_GROUPS_A = 8
WIDTH_A = N_GROUPS_A * HEAD_DIM
N_HEADS_B = 8
WIDTH_B = N_HEADS_B * HEAD_DIM
N_HEADS_C = 16
WIDTH_C = N_HEADS_C * HEAD_DIM
SGU_CHUNK = 128
MOBA_BLOCK = 256
MOBA_TOPK = 3
IDX_HEADS = 8
IDX_DIM = 64
DSA_TOPK = 256
EPS = 1e-6
NEG = -1e30
LOG2E = 1.4426950408889634
BF16_SPACING = 2.0 ** -7
TINY = 1e-30

LANES = 128
SUBLANES = 8
ROW_TILE = 512
Q_TILE = 256
DSA_K_CHUNK = 512
FOLD_CHAINS = 8
ONES_ROWS = 16
SCORE_SLOTS = 4
MOBA_SLOTS = 4
MOBA_K_CHUNK = 2 * MOBA_BLOCK
COARSE_STEPS = 10
BISECT_PER_ROUND = 8
BISECT_ROUNDS = 12
VMEM_LIMIT = 56 * 1024 * 1024

F32 = jnp.float32
BF16 = jnp.bfloat16


def _dot(a, b):
    return jnp.dot(a, b, preferred_element_type=F32)


def _rms(x, g):
    return x * lax.rsqrt(jnp.mean(x * x, axis=-1, keepdims=True) + EPS) * g


def _rope(z, c, sa, sb):
    outs = []
    for j in range(z.shape[1] // LANES):
        zz = z[:, j * LANES:(j + 1) * LANES]
        outs.append(zz * c + pltpu.roll(zz, LANES - ROPE_HALF, 1) * sa + pltpu.roll(zz, ROPE_HALF, 1) * sb)
    return outs[0] if len(outs) == 1 else jnp.concatenate(outs, axis=1)


def _gelu(x):
    return 0.5 * x * (1.0 + lax.erf(x * (2.0 ** -0.5)))


def _fold_rows(x, op, group=SUBLANES):
    parts = [x[a:a + group] for a in range(0, x.shape[0], group)]
    chains = parts[:FOLD_CHAINS]
    for a, part in enumerate(parts[FOLD_CHAINS:]):
        chains[a % FOLD_CHAINS] = op(chains[a % FOLD_CHAINS], part)
    while len(chains) > 1:
        chains = [op(chains[a], chains[a + 1]) if a + 1 < len(chains) else chains[a] for a in range(0, len(chains), 2)]
    return chains[0]


def _head_rows(x, hh):
    r = lax.broadcasted_iota(jnp.int32, (LANES, 1), 0)
    return jnp.where((r >= hh * HEAD_DIM) & (r < (hh + 1) * HEAD_DIM), x, jnp.zeros_like(x))


def _const_spec(shape):
    nd = len(shape)
    return pl.BlockSpec(shape, lambda *_: (0,) * nd, pipeline_mode=pl.Buffered(1))


def _params(n_axes):
    return pltpu.CompilerParams(dimension_semantics=("arbitrary",) * n_axes, vmem_limit_bytes=VMEM_LIMIT)


def _stage_scores(score_fn, heads, s_ref, cm_ref):
    n_slots = s_ref.shape[0]
    for h in heads:
        s = score_fn(h)
        s_ref[h % n_slots] = s
        cm_ref[h % n_slots] = _fold_rows(s, jnp.maximum)


def _attend_chunk(n_heads, score_fn, next_score_fn, vt_fn, s_ref, cm_ref, m_ref, acc_ref):
    n_slots = s_ref.shape[0]
    ahead = n_slots - 1
    assert n_heads % n_slots == 0
    for h in range(n_heads):
        if h + ahead < n_heads:
            _stage_scores(score_fn, [h + ahead], s_ref, cm_ref)
        else:
            _stage_scores(next_score_fn, [h + ahead - n_heads], s_ref, cm_ref)
        slot = h % n_slots
        m_old = m_ref[h]
        m_new = jnp.maximum(m_old, jnp.max(cm_ref[slot], axis=0, keepdims=True))
        p = jnp.exp2(s_ref[slot] - m_new).astype(BF16)
        m_ref[h] = m_new
        vt = vt_fn(h)
        vt = jnp.concatenate([vt, jnp.ones((acc_ref.shape[1] - vt.shape[0], vt.shape[1]), BF16)], axis=0)
        acc_ref[h] = jnp.exp2(m_old - m_new) * acc_ref[h] + _dot(vt, p)


def _softmax_init(m_ref, acc_ref):
    m_ref[...] = jnp.full(m_ref.shape, -jnp.inf, F32)
    acc_ref[...] = jnp.zeros(acc_ref.shape, F32)


def _write_heads(o_ref, acc_ref):
    def head(h):
        return acc_ref[h, :HEAD_DIM, :] / acc_ref[h, HEAD_DIM:HEAD_DIM + 1, :]

    for p in range(acc_ref.shape[0] // 2):
        o_ref[:, p * LANES:(p + 1) * LANES] = jnp.concatenate([head(2 * p), head(2 * p + 1)], axis=0).T.astype(BF16)


def _front0_kernel(h_ref, g_ref, w_ref, c_ref, sa_ref, sb_ref, lng_ref, lnb_ref, ws_ref, bs_ref,
                   ya_ref, qt_ref, k_ref, vt_ref, km_ref):
    tm = h_ref.shape[0]
    xn = _rms(h_ref[...], g_ref[...]).astype(BF16)

    def proj(lo, hi):
        return _dot(xn, w_ref[:, lo:hi])

    u = _gelu(proj(0, WIDTH_A))
    v = _gelu(proj(WIDTH_A, 2 * WIDTH_A))
    lane = lax.broadcasted_iota(jnp.int32, (1, LANES), 1)
    low = lane < HEAD_DIM
    row = lax.broadcasted_iota(jnp.int32, (SGU_CHUNK, SGU_CHUNK), 0)
    col = lax.broadcasted_iota(jnp.int32, (SGU_CHUNK, SGU_CHUNK), 1)
    tril = col <= row
    for j in range(WIDTH_A // LANES):
        sl = slice(j * LANES, (j + 1) * LANES)
        vv = v[:, sl]

        def half_mean(t):
            s_lo = jnp.sum(jnp.where(low, t, 0.0), axis=-1, keepdims=True)
            s_hi = jnp.sum(jnp.where(low, 0.0, t), axis=-1, keepdims=True)
            return jnp.where(low, s_lo, s_hi) * (1.0 / HEAD_DIM)

        xc = vv - half_mean(vv)
        var = half_mean(xc * xc)
        vn = (xc * lax.rsqrt(var + EPS) * lng_ref[:, sl] + lnb_ref[:, sl]).astype(BF16)
        w_lo = jnp.where(tril, ws_ref[2 * j], 0.0).astype(BF16)
        w_hi = jnp.where(tril, ws_ref[2 * j + 1], 0.0).astype(BF16)
        for c in range(tm // SGU_CHUNK):
            rs = slice(c * SGU_CHUNK, (c + 1) * SGU_CHUNK)
            vc = vn[rs]
            mixed = jnp.where(low, _dot(w_lo, vc), _dot(w_hi, vc)) + bs_ref[:, sl]
            ya_ref[rs, sl] = (u[rs, sl] * mixed).astype(BF16)

    o = 2 * WIDTH_A
    c, sa, sb = c_ref[...], sa_ref[...], sb_ref[...]
    q = _rope(proj(o, o + WIDTH_B), c, sa, sb) * (HEAD_DIM ** -0.5 * LOG2E)
    qt_ref[0] = q.T.astype(BF16)
    k = _rope(proj(o + WIDTH_B, o + 2 * WIDTH_B), c, sa, sb)
    k_ref[...] = k.astype(BF16)
    v_b = proj(o + 2 * WIDTH_B, o + 3 * WIDTH_B)
    for blk in range(tm // MOBA_BLOCK):
        rs = slice(blk * MOBA_BLOCK, (blk + 1) * MOBA_BLOCK)
        km_ref[0, blk:blk + 1, :] = jnp.mean(k[rs], axis=0, keepdims=True)
        vt_ref[0, blk] = v_b[rs].T.astype(BF16)


def _front0(h, g, w, c, sa, sb, lng, lnb, ws, bs, batch, seq):
    t, d = h.shape
    tm = ROW_TILE
    nt = seq // tm
    nblk = tm // MOBA_BLOCK
    row = lambda n: pl.BlockSpec((tm, n), lambda i: (i, 0))
    return pl.pallas_call(
        _front0_kernel,
        grid=(t // tm,),
        in_specs=[row(d), _const_spec((1, d)), _const_spec(w.shape), row(LANES), row(LANES), row(LANES),
                  _const_spec(lng.shape), _const_spec(lnb.shape), _const_spec(ws.shape), _const_spec(bs.shape)],
        out_specs=[row(WIDTH_A),
                   pl.BlockSpec((1, WIDTH_B, tm), lambda i: (i // nt, 0, i % nt)),
                   row(WIDTH_B),
                   pl.BlockSpec((1, nblk, WIDTH_B, MOBA_BLOCK), lambda i: (i // nt, i % nt, 0, 0)),
                   pl.BlockSpec((1, nblk, WIDTH_B), lambda i: (i, 0, 0))],
        out_shape=[jax.ShapeDtypeStruct((t, WIDTH_A), BF16),
                   jax.ShapeDtypeStruct((batch, WIDTH_B, seq), BF16),
                   jax.ShapeDtypeStruct((t, WIDTH_B), BF16),
                   jax.ShapeDtypeStruct((batch, seq // MOBA_BLOCK, WIDTH_B, MOBA_BLOCK), BF16),
                   jax.ShapeDtypeStruct((t // tm, nblk, WIDTH_B), F32)],
        compiler_params=_params(1),
        name="front0",
    )(h, g, w, c, sa, sb, lng, lnb, ws, bs)


def _moba_kernel(qt_ref, k_ref, vt_ref, km_ref, oh_ref, o_ref, qc_ref, s_ref, cm_ref, m_ref, acc_ref):
    i = pl.program_id(1)
    tq = qt_ref.shape[2]
    nb = km_ref.shape[0]
    n_heads = acc_ref.shape[0]
    kc_n = s_ref.shape[1]
    per_chunk = kc_n // MOBA_BLOCK
    blk = lax.broadcasted_iota(jnp.int32, (nb, 1), 0)
    blk_f = blk.astype(F32)
    q_blk = (i * tq + lax.broadcasted_iota(jnp.int32, (1, tq), 1)) // MOBA_BLOCK
    past = blk < q_blk
    _softmax_init(m_ref, acc_ref)
    for h in range(n_heads):
        sl = slice((h // 2) * LANES, (h // 2 + 1) * LANES)
        qm = _head_rows(qt_ref[0, sl, :], h % 2)
        g = jnp.where(past, _dot(km_ref[:, sl].astype(BF16), qm), NEG)
        sel = jnp.zeros(g.shape, jnp.bool_)
        for _ in range(min(MOBA_TOPK, nb)):
            top = jnp.max(g, axis=0, keepdims=True)
            idx = jnp.min(jnp.where(g == top, blk_f, 1e9), axis=0, keepdims=True)
            pick = blk_f == idx
            sel = sel | pick
            g = jnp.where(pick, -jnp.inf, g)
        selb = jnp.where((sel & past) | (blk == q_blk), 0.0, NEG).astype(BF16)
        qc_ref[h] = jnp.concatenate([qm, selb, jnp.zeros((LANES - nb, tq), BF16)], axis=0)

    def values(c):
        def vt(h):
            rows = slice(h * HEAD_DIM, (h + 1) * HEAD_DIM)
            return jnp.concatenate([vt_ref[0, per_chunk * c + a, rows, :] for a in range(per_chunk)], axis=1)
        return vt

    def chunk_scores(c, mask=None):
        st = pl.multiple_of(c * kc_n, kc_n)

        def scores(h):
            kc = jnp.concatenate([k_ref[pl.ds(st, kc_n), (h // 2) * LANES:(h // 2 + 1) * LANES],
                                  oh_ref[pl.ds(st, kc_n), :]], axis=1)
            s = _dot(kc, qc_ref[h])
            return s if mask is None else jnp.where(mask, s, NEG)

        return scores

    own = (i * tq) // kc_n
    kpos = own * kc_n + lax.broadcasted_iota(jnp.int32, (kc_n, 1), 0)
    qpos = i * tq + lax.broadcasted_iota(jnp.int32, (1, tq), 1)
    own_scores = chunk_scores(own, kpos <= qpos)
    refs = (s_ref, cm_ref, m_ref, acc_ref)
    last = jnp.maximum(own - 1, 0)
    _stage_scores(own_scores, range(s_ref.shape[0] - 1), s_ref, cm_ref)
    _attend_chunk(n_heads, own_scores, chunk_scores(0), values(own), *refs)

    def body(c, carry):
        _attend_chunk(n_heads, chunk_scores(c), chunk_scores(jnp.minimum(c + 1, last)), values(c), *refs)
        return carry

    lax.fori_loop(0, own, body, 0)
    _write_heads(o_ref, acc_ref)


def _moba(qt, k, vt, km, onehot, batch, seq):
    t, width = k.shape
    tq = MOBA_K_CHUNK
    nb = seq // MOBA_BLOCK
    nq = seq // tq
    n_heads = width // HEAD_DIM
    one = pl.Buffered(1)
    return pl.pallas_call(
        _moba_kernel,
        grid=(batch, nq),
        in_specs=[pl.BlockSpec((1, width, tq), lambda b, i: (b, 0, i)),
                  pl.BlockSpec((seq, width), lambda b, i: (b, 0), pipeline_mode=one),
                  pl.BlockSpec((1, nb, width, MOBA_BLOCK), lambda b, i: (b, 0, 0, 0), pipeline_mode=one),
                  pl.BlockSpec((nb, width), lambda b, i: (b, 0)),
                  _const_spec(onehot.shape)],
        out_specs=pl.BlockSpec((tq, width), lambda b, i: (b * nq + i, 0)),
        out_shape=jax.ShapeDtypeStruct((t, width), BF16),
        scratch_shapes=[pltpu.VMEM((n_heads, 2 * LANES, tq), BF16), pltpu.VMEM((MOBA_SLOTS, MOBA_K_CHUNK, tq), F32),
                        pltpu.VMEM((MOBA_SLOTS, SUBLANES, tq), F32),
                        pltpu.VMEM((n_heads, 1, tq), F32), pltpu.VMEM((n_heads, HEAD_DIM + ONES_ROWS, tq), F32)],
        compiler_params=_params(2),
        name="moba",
    )(qt, k, vt, km, onehot)


def _post_kernel(*refs, n_mix, final):
    h_ref = refs[0]
    y_refs = refs[1:1 + n_mix]
    (wo_ref, p_ref, gf_ref, wg_ref, wu_ref, wd_ref, gp_ref, wpg_ref, wpp_ref, gl_ref, o_ref, a_ref) = refs[1 + n_mix:]
    h = h_ref[...]
    lo = 0
    for y_ref in y_refs:
        h = h + _dot(y_ref[...], wo_ref[0, lo:lo + y_ref.shape[1], :])
        lo += y_ref.shape[1]
    xn = _rms(h, gf_ref[...]).astype(BF16)
    d_ff = wg_ref.shape[2]
    step = 2 * LANES
    for lo in range(0, d_ff, step):
        gate = _dot(xn, wg_ref[0, :, lo:lo + step])
        up = _dot(xn, wu_ref[0, :, lo:lo + step])
        a_ref[:, lo:lo + step] = (gate * jax.nn.sigmoid(gate) * up).astype(BF16)
    h = h + _dot(a_ref[...], wd_ref[0])
    pg = jax.nn.sigmoid(_dot(_rms(h, gp_ref[...]).astype(BF16), wpg_ref[0]))
    h = h + _dot(p_ref[0].astype(BF16), wpp_ref[0]) * pg
    o_ref[...] = _rms(h, gl_ref[...]) if final else h


def _layer_spec(shape, layer):
    rest = (0,) * (len(shape) - 1)
    return pl.BlockSpec((1,) + tuple(shape[1:]), lambda *_: (layer,) + rest, pipeline_mode=pl.Buffered(1))


def _post(h, ys, wo, wo_layer, p, layer, gf, wg, wu, wd, gp, wpg, wpp, gl, final):
    t, d = h.shape
    tm = ROW_TILE
    row = lambda n: pl.BlockSpec((tm, n), lambda i: (i, 0))
    stack = lambda w: _layer_spec(w.shape, layer)
    return pl.pallas_call(
        functools.partial(_post_kernel, n_mix=len(ys), final=final),
        grid=(t // tm,),
        in_specs=[row(d)] + [row(y.shape[1]) for y in ys]
                 + [_layer_spec(wo.shape, wo_layer), pl.BlockSpec((1, tm, p.shape[2]), lambda i: (layer, i, 0)),
                    _const_spec(gf.shape), stack(wg), stack(wu), stack(wd), _const_spec(gp.shape), stack(wpg),
                    stack(wpp), _const_spec(gl.shape)],
        out_specs=row(d),
        out_shape=jax.ShapeDtypeStruct((t, d), F32),
        scratch_shapes=[pltpu.VMEM((tm, wg.shape[2]), BF16)],
        compiler_params=_params(1),
        name="post_final" if final else "post",
    )(h, *ys, wo, p, gf, wg, wu, wd, gp, wpg, wpp, gl)


def _front1_kernel(h_ref, g_ref, w_ref, c_ref, sa_ref, sb_ref, lng_ref, lnb_ref,
                   qt_ref, k_ref, vt_ref, qit_ref, ki_ref, wit_ref):
    xn = _rms(h_ref[...], g_ref[...]).astype(BF16)

    def proj(lo, hi):
        return _dot(xn, w_ref[:, lo:hi])

    c, sa, sb = c_ref[...], sa_ref[...], sb_ref[...]
    qt_ref[0] = (_rope(proj(0, WIDTH_C), c, sa, sb) * (HEAD_DIM ** -0.5 * LOG2E)).T.astype(BF16)
    k_ref[...] = _rope(proj(WIDTH_C, 2 * WIDTH_C), c, sa, sb).astype(BF16)
    vt_ref[0, 0] = proj(2 * WIDTH_C, 3 * WIDTH_C).T.astype(BF16)
    o = 3 * WIDTH_C
    qit_ref[0] = _rope(proj(o, o + IDX_HEADS * IDX_DIM), c, sa, sb).T.astype(BF16)
    o += IDX_HEADS * IDX_DIM
    ki = proj(o, o + LANES)
    xc = ki - jnp.mean(ki, axis=-1, keepdims=True)
    var = jnp.mean(xc * xc, axis=-1, keepdims=True)
    ki = xc * lax.rsqrt(var + EPS) * lng_ref[...] + lnb_ref[...]
    ki_ref[...] = _rope(ki, c, sa, sb).astype(BF16)
    wi = proj(o + LANES, o + 2 * LANES) * ((IDX_HEADS ** -0.5) * (IDX_DIM ** -0.5))
    wit_ref[0] = wi.T[:IDX_HEADS]


def _front1(h, g, w, c, sa, sb, lng, lnb, batch, seq):
    t, d = h.shape
    tm = ROW_TILE
    assert tm == DSA_K_CHUNK
    nt = seq // tm
    row = lambda n: pl.BlockSpec((tm, n), lambda i: (i, 0))
    tr = lambda n: pl.BlockSpec((1, n, tm), lambda i: (i // nt, 0, i % nt))
    return pl.pallas_call(
        _front1_kernel,
        grid=(t // tm,),
        in_specs=[row(d), _const_spec((1, d)), _const_spec(w.shape), row(LANES), row(LANES), row(LANES),
                  _const_spec(lng.shape), _const_spec(lnb.shape)],
        out_specs=[tr(WIDTH_C), row(WIDTH_C),
                   pl.BlockSpec((1, 1, WIDTH_C, tm), lambda i: (i // nt, i % nt, 0, 0)),
                   tr(IDX_HEADS * IDX_DIM), row(LANES), tr(IDX_HEADS)],
        out_shape=[jax.ShapeDtypeStruct((batch, WIDTH_C, seq), BF16),
                   jax.ShapeDtypeStruct((t, WIDTH_C), BF16),
                   jax.ShapeDtypeStruct((batch, nt, WIDTH_C, tm), BF16),
                   jax.ShapeDtypeStruct((batch, IDX_HEADS * IDX_DIM, seq), BF16),
                   jax.ShapeDtypeStruct((t, LANES), BF16),
                   jax.ShapeDtypeStruct((batch, IDX_HEADS, seq), F32)],
        compiler_params=_params(1),
        name="front1",
    )(h, g, w, c, sa, sb, lng, lnb)


def _dsa_index_kernel(qit_ref, wit_ref, ki_ref, b_ref, sc_ref, sc16_ref, *, n_sel):
    i = pl.program_id(1)
    tq = qit_ref.shape[2]
    kc_n = DSA_K_CHUNK
    n_chunks_total = sc_ref.shape[0]
    n_chunks = (i * tq + tq + kc_n - 1) // kc_n
    qpos = i * tq + lax.broadcasted_iota(jnp.int32, (1, tq), 1)
    kofs = lax.broadcasted_iota(jnp.int32, (kc_n, 1), 0)
    wv = wit_ref[0]

    def causal(c):
        return (c * kc_n + kofs) <= qpos

    big = 3e38

    def score_chunk(c, carry):
        mx, mn = carry
        kc = ki_ref[pl.ds(pl.multiple_of(c * kc_n, kc_n), kc_n), :]
        acc = jnp.zeros((kc_n, tq), F32)
        for h in range(IDX_HEADS):
            qm = _head_rows(qit_ref[0, (h // 2) * LANES:(h // 2 + 1) * LANES, :], h % 2)
            acc = acc + jnp.maximum(_dot(kc, qm), 0.0) * wv[h:h + 1, :]
        ok = causal(c)
        sc_ref[c] = jnp.where(ok, acc, NEG)
        sc16_ref[c] = jnp.where(ok, acc, NEG).astype(BF16)
        return (jnp.maximum(mx, _fold_rows(jnp.where(ok, acc, -big), jnp.maximum)),
                jnp.minimum(mn, _fold_rows(jnp.where(ok, acc, big), jnp.minimum)))

    mx, mn = lax.fori_loop(0, n_chunks, score_chunk,
                           (jnp.full((SUBLANES, tq), -big, F32), jnp.full((SUBLANES, tq), big, F32)))
    col_max = jnp.max(mx, axis=0, keepdims=True)
    col_min = jnp.min(mn, axis=0, keepdims=True)

    def reduce_chunks(fn, op, init):
        def body(c, acc):
            return op(acc, _fold_rows(fn(sc_ref[c]), op))
        return lax.fori_loop(0, n_chunks, body, jnp.full((SUBLANES, tq), init, F32))

    def count_ge(thr):
        acc = reduce_chunks(lambda x: jnp.where(x >= thr, 1.0, 0.0), jnp.add, 0.0)
        return jnp.sum(acc, axis=0, keepdims=True)

    def pending(flags):
        return jnp.max(jnp.where(flags, 0.0, 1.0))

    k_f = float(n_sel)
    n_causal = (qpos + 1).astype(F32)
    small = n_causal < k_f
    rows16 = 2 * SUBLANES

    def count_ge_bf16(thr):
        one, zero = jnp.ones((), BF16), jnp.zeros((), BF16)

        def body(c, acc):
            hits = jnp.where(sc16_ref[c] >= thr, one, zero)
            return acc + _fold_rows(hits, jnp.add, rows16).astype(F32)

        return jnp.sum(lax.fori_loop(0, n_chunks, body, jnp.zeros((rows16, tq), F32)), axis=0, keepdims=True)

    def coarse(_, carry):
        lo, hi = carry
        mid = (0.5 * (lo + hi)).astype(BF16)
        ok = count_ge_bf16(mid) >= k_f
        mid = mid.astype(F32)
        return jnp.where(ok, mid, lo), jnp.where(ok, hi, mid)

    top = col_max.astype(BF16).astype(F32)
    lo, hi = lax.fori_loop(0, COARSE_STEPS, coarse,
                           (col_min.astype(BF16).astype(F32),
                            (top + jnp.abs(top) * 2.0 * BF16_SPACING + TINY).astype(BF16).astype(F32)))
    lo = jnp.where(small, NEG, lo - jnp.abs(lo) * BF16_SPACING - TINY)
    c_lo = jnp.full((1, tq), 2.0 * k_f, F32)
    c_hi = jnp.full((1, tq), -1.0, F32)

    def bisect(_, carry):
        lo, hi, c_lo, c_hi = carry
        mid = 0.5 * (lo + hi)
        cnt = count_ge(mid)
        ok = cnt >= k_f
        return (jnp.where(ok, mid, lo), jnp.where(ok, hi, mid), jnp.where(ok, cnt, c_lo), jnp.where(ok, c_hi, cnt))

    def bracket_ends(lo, hi):
        def body(c, carry):
            below, above = carry
            x = sc_ref[c]
            return (jnp.maximum(below, _fold_rows(jnp.where(x < hi, x, -big), jnp.maximum)),
                    jnp.minimum(above, _fold_rows(jnp.where(x >= lo, x, big), jnp.minimum)))

        below, above = lax.fori_loop(0, n_chunks, body, (jnp.full((SUBLANES, tq), -big, F32),
                                                         jnp.full((SUBLANES, tq), big, F32)))
        return jnp.max(below, axis=0, keepdims=True), jnp.min(above, axis=0, keepdims=True)

    def settled(c_lo, c_hi, below, above):
        return small | (c_hi == k_f - 1.0) | (c_lo == k_f) | (below == above)

    def narrow_cond(carry):
        return (carry[6] < BISECT_ROUNDS) & (carry[7] > 0.5)

    def narrow(carry):
        state = lax.fori_loop(0, BISECT_PER_ROUND, bisect, carry[:4])
        below, above = bracket_ends(state[0], state[1])
        return (*state, below, above, carry[6] + 1, pending(settled(state[2], state[3], below, above)))

    zeros = jnp.zeros((1, tq), F32)
    lo, hi, c_lo, c_hi, below, above, _, _ = lax.while_loop(
        narrow_cond, narrow, (lo, hi, c_lo, c_hi, zeros, zeros, jnp.int32(0), jnp.float32(1.0)))
    thr = jnp.where(small, NEG, jnp.where(c_lo == k_f, above, below))
    done = jnp.where(settled(c_lo, c_hi, below, above), 1.0, 0.0)

    def snap_cond(carry):
        return carry[3] > 0.5

    def snap(carry):
        hi, thr, done, _ = carry
        cand = jnp.max(reduce_chunks(lambda x: jnp.where(x < hi, x, -big), jnp.maximum, -big),
                       axis=0, keepdims=True)
        ok = count_ge(cand) >= k_f
        thr = jnp.where(done > 0.5, thr, cand)
        done = jnp.where(ok, 1.0, done)
        hi = jnp.where(done > 0.5, hi, cand)
        return hi, thr, done, jnp.max(1.0 - done)

    _, thr, _, _ = lax.while_loop(snap_cond, snap, (hi, thr, done, jnp.max(1.0 - done)))

    n_ge = count_ge(thr)
    tied = jnp.max(jnp.where((n_ge > k_f) & jnp.logical_not(small), 1.0, 0.0))

    @pl.when(tied < 0.5)
    def _():
        def body(c, carry):
            x = sc_ref[c]
            b_ref[0, 0, c] = jnp.where((x >= thr) & causal(c), 0.0, NEG)
            return carry
        lax.fori_loop(0, n_chunks, body, 0)

    @pl.when(tied > 0.5)
    def _():
        n_gt = reduce_chunks(lambda x: jnp.where(x > thr, 1.0, 0.0), jnp.add, 0.0)
        need = k_f - jnp.sum(n_gt, axis=0, keepdims=True)

        def body(c, seen):
            x = sc_ref[c]
            eq = x == thr
            total = seen + jnp.sum(_fold_rows(jnp.where(eq, 1.0, 0.0), jnp.add), axis=0, keepdims=True)
            keep_all = total <= need
            crossing = jnp.max(jnp.where(keep_all | (seen >= need), 0.0, 1.0))

            @pl.when(crossing < 0.5)
            def _():
                b_ref[0, 0, c] = jnp.where(((x > thr) | (eq & keep_all)) & causal(c), 0.0, NEG)

            @pl.when(crossing > 0.5)
            def _():
                r = lax.broadcasted_iota(jnp.int32, (kc_n, kc_n), 0)
                cc = lax.broadcasted_iota(jnp.int32, (kc_n, kc_n), 1)
                prefix = jnp.where(cc <= r, 1.0, 0.0).astype(BF16)
                cnt = seen + _dot(prefix, jnp.where(eq, 1.0, 0.0).astype(BF16))
                b_ref[0, 0, c] = jnp.where(((x > thr) | (eq & (cnt <= need))) & causal(c), 0.0, NEG)

            return total
        lax.fori_loop(0, n_chunks, body, jnp.zeros((1, tq), F32))

    def fill(c, carry):
        b_ref[0, 0, c] = jnp.full((kc_n, tq), NEG, F32)
        return carry

    lax.fori_loop(n_chunks, n_chunks_total, fill, 0)


def _dsa_index(qit, wit, ki, batch, seq):
    tq = Q_TILE
    nq = seq // tq
    nch = seq // DSA_K_CHUNK
    n_sel = min(DSA_TOPK, seq // 4)
    return pl.pallas_call(
        functools.partial(_dsa_index_kernel, n_sel=n_sel),
        grid=(batch, nq),
        in_specs=[pl.BlockSpec((1, qit.shape[1], tq), lambda b, i: (b, 0, i)),
                  pl.BlockSpec((1, IDX_HEADS, tq), lambda b, i: (b, 0, i)),
                  pl.BlockSpec((seq, LANES), lambda b, i: (b, 0))],
        out_specs=pl.BlockSpec((1, 1, nch, DSA_K_CHUNK, tq), lambda b, i: (b, i, 0, 0, 0)),
        out_shape=jax.ShapeDtypeStruct((batch, nq, nch, DSA_K_CHUNK, tq), F32),
        scratch_shapes=[pltpu.VMEM((nch, DSA_K_CHUNK, tq), F32), pltpu.VMEM((nch, DSA_K_CHUNK, tq), BF16)],
        compiler_params=_params(2),
        name="dsa_index",
    )(qit, wit, ki)


def _dsa_attn_kernel(qt_ref, k_ref, vt_ref, b_ref, o_ref, qm_ref, s_ref, cm_ref, m_ref, acc_ref):
    i = pl.program_id(1)
    tq = qt_ref.shape[2]
    kc_n = DSA_K_CHUNK
    n_heads = acc_ref.shape[0]
    n_chunks = (i * tq + tq + kc_n - 1) // kc_n
    _softmax_init(m_ref, acc_ref)
    for h in range(n_heads):
        qm_ref[h] = _head_rows(qt_ref[0, (h // 2) * LANES:(h // 2 + 1) * LANES, :], h % 2)

    def chunk_scores(c):
        st = pl.multiple_of(c * kc_n, kc_n)

        def scores(h):
            return _dot(k_ref[pl.ds(st, kc_n), (h // 2) * LANES:(h // 2 + 1) * LANES], qm_ref[h]) + b_ref[0, 0, c]

        return scores

    _stage_scores(chunk_scores(0), range(s_ref.shape[0] - 1), s_ref, cm_ref)

    def body(c, carry):
        def values(h):
            return vt_ref[0, c, h * HEAD_DIM:(h + 1) * HEAD_DIM, :]

        _attend_chunk(n_heads, chunk_scores(c), chunk_scores(jnp.minimum(c + 1, n_chunks - 1)), values,
                      s_ref, cm_ref, m_ref, acc_ref)
        return carry

    lax.fori_loop(0, n_chunks, body, 0)
    _write_heads(o_ref, acc_ref)


def _dsa_attn(qt, k, vt, bias, batch, seq):
    t, width = k.shape
    tq = Q_TILE
    nq = seq // tq
    n_heads = width // HEAD_DIM
    one = pl.Buffered(1)
    return pl.pallas_call(
        _dsa_attn_kernel,
        grid=(batch, nq),
        in_specs=[pl.BlockSpec((1, width, tq), lambda b, i: (b, 0, i)),
                  pl.BlockSpec((seq, width), lambda b, i: (b, 0), pipeline_mode=one),
                  pl.BlockSpec((1,) + vt.shape[1:], lambda b, i: (b, 0, 0, 0), pipeline_mode=one),
                  pl.BlockSpec((1, 1) + bias.shape[2:], lambda b, i: (b, i, 0, 0, 0))],
        out_specs=pl.BlockSpec((tq, width), lambda b, i: (b * nq + i, 0)),
        out_shape=jax.ShapeDtypeStruct((t, width), BF16),
        scratch_shapes=[pltpu.VMEM((n_heads, LANES, tq), BF16), pltpu.VMEM((SCORE_SLOTS, DSA_K_CHUNK, tq), F32),
                        pltpu.VMEM((SCORE_SLOTS, SUBLANES, tq), F32),
                        pltpu.VMEM((n_heads, 1, tq), F32), pltpu.VMEM((n_heads, HEAD_DIM + ONES_ROWS, tq), F32)],
        compiler_params=_params(2),
        name="dsa_attn",
    )(qt, k, vt, bias)


def _rope_tables(positions):
    inv_freq = ROPE_THETA ** (-jnp.arange(0, ROPE_DIM, 2, dtype=F32) / ROPE_DIM)
    m = jnp.arange(LANES) % HEAD_DIM
    ang = positions.astype(F32).reshape(-1, 1) * inv_freq[m % ROPE_HALF][None, :]
    cos, sin = jnp.cos(ang), jnp.sin(ang)
    c = jnp.where(m < ROPE_DIM, cos, 1.0)
    sa = jnp.where(m < ROPE_HALF, -sin, 0.0)
    sb = jnp.where((m >= ROPE_HALF) & (m < ROPE_DIM), sin, 0.0)
    return c, sa, sb


def kernel(x, p, positions, g_mix, w_in_even, a_ln_g, a_ln_b, a_w_s, a_b_s, w_out_even, w_in_odd, c_kidx_ln_g, c_kidx_ln_b, w_out_odd, g_ffn, w_ffn_gate, w_ffn_up, w_ffn_down, g_ple, w_ple_proj, w_ple_gate, g_final):
    batch, seq, d = x.shape
    depth = p.shape[0]
    t = batch * seq
    assert seq % ROW_TILE == 0 and seq % DSA_K_CHUNK == 0 and min(DSA_TOPK, seq // 4) <= DSA_K_CHUNK
    c, sa, sb = _rope_tables(positions)
    h = x.reshape(t, d)
    p2 = p.reshape(depth, t, p.shape[-1])
    row = lambda a: a.reshape(1, -1)
    onehot = (jnp.arange(seq)[:, None] // MOBA_BLOCK == jnp.arange(LANES)[None, :]).astype(BF16)
    bf = lambda w: w.astype(BF16)
    wo_even, wo_odd, wg, wu, wd, wpg, wpp = map(bf, (w_out_even, w_out_odd, w_ffn_gate, w_ffn_up, w_ffn_down,
                                                     w_ple_gate, w_ple_proj))
    for i in range(depth):
        j = i // 2
        if i % 2 == 0:
            bs = jnp.repeat(a_b_s[j].T, HEAD_DIM, axis=1)
            ya, qt, k, vt, km = _front0(h, row(g_mix[i]), bf(w_in_even[j]), c, sa, sb,
                                        row(a_ln_g[j]), row(a_ln_b[j]), a_w_s[j], bs, batch, seq)
            ys = [ya, _moba(qt, k, vt, km.reshape(t // MOBA_BLOCK, WIDTH_B), onehot, batch, seq)]
            wo = wo_even
        else:
            w = w_in_odd[j]
            o = 3 * WIDTH_C + IDX_HEADS * IDX_DIM
            w1 = bf(jnp.concatenate([w[:, :o], w[:, o:o + IDX_DIM], w[:, o:o + IDX_DIM], w[:, o + IDX_DIM:],
                                     jnp.zeros((d, LANES - IDX_HEADS), w.dtype)], axis=1))
            qt, k, vt, qit, ki, wit = _front1(h, row(g_mix[i]), w1, c, sa, sb, row(jnp.tile(c_kidx_ln_g[j], 2)),
                                              row(jnp.tile(c_kidx_ln_b[j], 2)), batch, seq)
            ys = [_dsa_attn(qt, k, vt, _dsa_index(qit, wit, ki, batch, seq), batch, seq)]
            wo = wo_odd
        h = _post(h, ys, wo, j, p2, i, row(g_ffn[i]), wg, wu, wd, row(g_ple[i]), wpg, wpp, row(g_final),
                  final=(i == depth - 1))
    return h.reshape(batch, seq, d)
```

```python
import functools

import jax
import jax.numpy as jnp
from jax import lax
from jax.experimental import pallas as pl
from jax.experimental.pallas import tpu as pltpu

HEAD_DIM = 64
ROPE_DIM = HEAD_DIM // 4
ROPE_HALF = ROPE_DIM // 2
ROPE_THETA = 500000.0
N_GROUPS_A = 8
WIDTH_A = N_GROUPS_A * HEAD_DIM
N_HEADS_B = 8
WIDTH_B = N_HEADS_B * HEAD_DIM
N_HEADS_C = 16
WIDTH_C = N_HEADS_C * HEAD_DIM
SGU_CHUNK = 128
MOBA_BLOCK = 256
MOBA_TOPK = 3
IDX_HEADS = 8
IDX_DIM = 64
DSA_TOPK = 256
EPS = 1e-6
NEG = -1e30
LOG2E = 1.4426950408889634
BF16_SPACING = 2.0 ** -7
TINY = 1e-30

LANES = 128
SUBLANES = 8
ROW_TILE = 512
Q_TILE = 256
DSA_K_CHUNK = 512
FOLD_CHAINS = 8
ONES_ROWS = 16
SCORE_SLOTS = 4
MOBA_SLOTS = 4
MOBA_K_CHUNK = 2 * MOBA_BLOCK
COARSE_STEPS = 9
BISECT_PER_ROUND = 8
BISECT_ROUNDS = 12
VMEM_LIMIT = 56 * 1024 * 1024

F32 = jnp.float32
BF16 = jnp.bfloat16


def _dot(a, b):
    return jnp.dot(a, b, preferred_element_type=F32)


def _rms(x, g):
    return x * lax.rsqrt(jnp.mean(x * x, axis=-1, keepdims=True) + EPS) * g


def _rope(z, c, sa, sb):
    outs = []
    for j in range(z.shape[1] // LANES):
        zz = z[:, j * LANES:(j + 1) * LANES]
        outs.append(zz * c + pltpu.roll(zz, LANES - ROPE_HALF, 1) * sa + pltpu.roll(zz, ROPE_HALF, 1) * sb)
    return outs[0] if len(outs) == 1 else jnp.concatenate(outs, axis=1)


def _gelu(x):
    return 0.5 * x * (1.0 + lax.erf(x * (2.0 ** -0.5)))


def _fold_rows(x, op, group=SUBLANES):
    parts = [x[a:a + group] for a in range(0, x.shape[0], group)]
    chains = parts[:FOLD_CHAINS]
    for a, part in enumerate(parts[FOLD_CHAINS:]):
        chains[a % FOLD_CHAINS] = op(chains[a % FOLD_CHAINS], part)
    while len(chains) > 1:
        chains = [op(chains[a], chains[a + 1]) if a + 1 < len(chains) else chains[a] for a in range(0, len(chains), 2)]
    return chains[0]


def _head_rows(x, hh):
    r = lax.broadcasted_iota(jnp.int32, (LANES, 1), 0)
    return jnp.where((r >= hh * HEAD_DIM) & (r < (hh + 1) * HEAD_DIM), x, jnp.zeros_like(x))


def _const_spec(shape):
    nd = len(shape)
    return pl.BlockSpec(shape, lambda *_: (0,) * nd, pipeline_mode=pl.Buffered(1))


def _params(n_axes):
    return pltpu.CompilerParams(dimension_semantics=("arbitrary",) * n_axes, vmem_limit_bytes=VMEM_LIMIT)


def _stage_scores(score_fn, heads, s_ref, cm_ref):
    n_slots = s_ref.shape[0]
    for h in heads:
        s = score_fn(h)
        s_ref[h % n_slots] = s
        cm_ref[h % n_slots] = _fold_rows(s, jnp.maximum)


def _attend_chunk(n_heads, score_fn, next_score_fn, vt_fn, s_ref, cm_ref, m_ref, acc_ref):
    n_slots = s_ref.shape[0]
    ahead = n_slots - 1
    assert n_heads % n_slots == 0
    for h in range(n_heads):
        if h + ahead < n_heads:
            _stage_scores(score_fn, [h + ahead], s_ref, cm_ref)
        else:
            _stage_scores(next_score_fn, [h + ahead - n_heads], s_ref, cm_ref)
        slot = h % n_slots
        m_old = m_ref[h]
        m_new = jnp.maximum(m_old, jnp.max(cm_ref[slot], axis=0, keepdims=True))
        p = jnp.exp2(s_ref[slot] - m_new).astype(BF16)
        m_ref[h] = m_new
        vt = vt_fn(h)
        vt = jnp.concatenate([vt, jnp.ones((acc_ref.shape[1] - vt.shape[0], vt.shape[1]), BF16)], axis=0)
        acc_ref[h] = jnp.exp2(m_old - m_new) * acc_ref[h] + _dot(vt, p)


def _softmax_init(m_ref, acc_ref):
    m_ref[...] = jnp.full(m_ref.shape, -jnp.inf, F32)
    acc_ref[...] = jnp.zeros(acc_ref.shape, F32)


def _write_heads(o_ref, acc_ref):
    def head(h):
        return acc_ref[h, :HEAD_DIM, :] / acc_ref[h, HEAD_DIM:HEAD_DIM + 1, :]

    for p in range(acc_ref.shape[0] // 2):
        o_ref[:, p * LANES:(p + 1) * LANES] = jnp.concatenate([head(2 * p), head(2 * p + 1)], axis=0).T.astype(BF16)


def _front0_kernel(h_ref, g_ref, w_ref, c_ref, sa_ref, sb_ref, lng_ref, lnb_ref, ws_ref, bs_ref,
                   ya_ref, qt_ref, k_ref, vt_ref, km_ref):
    tm = h_ref.shape[0]
    xn = _rms(h_ref[...], g_ref[...]).astype(BF16)

    def proj(lo, hi):
        return _dot(xn, w_ref[:, lo:hi])

    u = _gelu(proj(0, WIDTH_A))
    v = _gelu(proj(WIDTH_A, 2 * WIDTH_A))
    lane = lax.broadcasted_iota(jnp.int32, (1, LANES), 1)
    low = lane < HEAD_DIM
    row = lax.broadcasted_iota(jnp.int32, (SGU_CHUNK, SGU_CHUNK), 0)
    col = lax.broadcasted_iota(jnp.int32, (SGU_CHUNK, SGU_CHUNK), 1)
    tril = col <= row
    for j in range(WIDTH_A // LANES):
        sl = slice(j * LANES, (j + 1) * LANES)
        vv = v[:, sl]

        def half_mean(t):
            s_lo = jnp.sum(jnp.where(low, t, 0.0), axis=-1, keepdims=True)
            s_hi = jnp.sum(jnp.where(low, 0.0, t), axis=-1, keepdims=True)
            return jnp.where(low, s_lo, s_hi) * (1.0 / HEAD_DIM)

        xc = vv - half_mean(vv)
        var = half_mean(xc * xc)
        vn = (xc * lax.rsqrt(var + EPS) * lng_ref[:, sl] + lnb_ref[:, sl]).astype(BF16)
        w_lo = jnp.where(tril, ws_ref[2 * j], 0.0).astype(BF16)
        w_hi = jnp.where(tril, ws_ref[2 * j + 1], 0.0).astype(BF16)
        for c in range(tm // SGU_CHUNK):
            rs = slice(c * SGU_CHUNK, (c + 1) * SGU_CHUNK)
            vc = vn[rs]
            mixed = jnp.where(low, _dot(w_lo, vc), _dot(w_hi, vc)) + bs_ref[:, sl]
            ya_ref[rs, sl] = (u[rs, sl] * mixed).astype(BF16)

    o = 2 * WIDTH_A
    c, sa, sb = c_ref[...], sa_ref[...], sb_ref[...]
    q = _rope(proj(o, o + WIDTH_B), c, sa, sb) * (HEAD_DIM ** -0.5 * LOG2E)
    qt_ref[0] = q.T.astype(BF16)
    k = _rope(proj(o + WIDTH_B, o + 2 * WIDTH_B), c, sa, sb)
    k_ref[...] = k.astype(BF16)
    v_b = proj(o + 2 * WIDTH_B, o + 3 * WIDTH_B)
    for blk in range(tm // MOBA_BLOCK):
        rs = slice(blk * MOBA_BLOCK, (blk + 1) * MOBA_BLOCK)
        km_ref[0, blk:blk + 1, :] = jnp.mean(k[rs], axis=0, keepdims=True)
        vt_ref[0, blk] = v_b[rs].T.astype(BF16)


def _front0(h, g, w, c, sa, sb, lng, lnb, ws, bs, batch, seq):
    t, d = h.shape
    tm = ROW_TILE
    nt = seq // tm
    nblk = tm // MOBA_BLOCK
    row = lambda n: pl.BlockSpec((tm, n), lambda i: (i, 0))
    return pl.pallas_call(
        _front0_kernel,
        grid=(t // tm,),
        in_specs=[row(d), _const_spec((1, d)), _const_spec(w.shape), row(LANES), row(LANES), row(LANES),
                  _const_spec(lng.shape), _const_spec(lnb.shape), _const_spec(ws.shape), _const_spec(bs.shape)],
        out_specs=[row(WIDTH_A),
                   pl.BlockSpec((1, WIDTH_B, tm), lambda i: (i // nt, 0, i % nt)),
                   row(WIDTH_B),
                   pl.BlockSpec((1, nblk, WIDTH_B, MOBA_BLOCK), lambda i: (i // nt, i % nt, 0, 0)),
                   pl.BlockSpec((1, nblk, WIDTH_B), lambda i: (i, 0, 0))],
        out_shape=[jax.ShapeDtypeStruct((t, WIDTH_A), BF16),
                   jax.ShapeDtypeStruct((batch, WIDTH_B, seq), BF16),
                   jax.ShapeDtypeStruct((t, WIDTH_B), BF16),
                   jax.ShapeDtypeStruct((batch, seq // MOBA_BLOCK, WIDTH_B, MOBA_BLOCK), BF16),
                   jax.ShapeDtypeStruct((t // tm, nblk, WIDTH_B), F32)],
        compiler_params=_params(1),
        name="front0",
    )(h, g, w, c, sa, sb, lng, lnb, ws, bs)


def _moba_kernel(qt_ref, k_ref, vt_ref, km_ref, oh_ref, o_ref, qc_ref, s_ref, cm_ref, m_ref, acc_ref):
    i = pl.program_id(1)
    tq = qt_ref.shape[2]
    nb = km_ref.shape[0]
    n_heads = acc_ref.shape[0]
    kc_n = s_ref.shape[1]
    per_chunk = kc_n // MOBA_BLOCK
    blk = lax.broadcasted_iota(jnp.int32, (nb, 1), 0)
    blk_f = blk.astype(F32)
    q_blk = (i * tq + lax.broadcasted_iota(jnp.int32, (1, tq), 1)) // MOBA_BLOCK
    past = blk < q_blk
    _softmax_init(m_ref, acc_ref)
    for h in range(n_heads):
        sl = slice((h // 2) * LANES, (h // 2 + 1) * LANES)
        qm = _head_rows(qt_ref[0, sl, :], h % 2)
        g = jnp.where(past, _dot(km_ref[:, sl].astype(BF16), qm), NEG)
        sel = jnp.zeros(g.shape, jnp.bool_)
        for _ in range(min(MOBA_TOPK, nb)):
            top = jnp.max(g, axis=0, keepdims=True)
            idx = jnp.min(jnp.where(g == top, blk_f, 1e9), axis=0, keepdims=True)
            pick = blk_f == idx
            sel = sel | pick
            g = jnp.where(pick, -jnp.inf, g)
        selb = jnp.where((sel & past) | (blk == q_blk), 0.0, NEG).astype(BF16)
        qc_ref[h] = jnp.concatenate([qm, selb, jnp.zeros((LANES - nb, tq), BF16)], axis=0)

    def values(c):
        def vt(h):
            rows = slice(h * HEAD_DIM, (h + 1) * HEAD_DIM)
            return jnp.concatenate([vt_ref[0, per_chunk * c + a, rows, :] for a in range(per_chunk)], axis=1)
        return vt

    def chunk_scores(c, mask=None):
        st = pl.multiple_of(c * kc_n, kc_n)

        def scores(h):
            kc = jnp.concatenate([k_ref[pl.ds(st, kc_n), (h // 2) * LANES:(h // 2 + 1) * LANES],
                                  oh_ref[pl.ds(st, kc_n), :]], axis=1)
            s = _dot(kc, qc_ref[h])
            return s if mask is None else jnp.where(mask, s, NEG)

        return scores

    own = (i * tq) // kc_n
    kpos = own * kc_n + lax.broadcasted_iota(jnp.int32, (kc_n, 1), 0)
    qpos = i * tq + lax.broadcasted_iota(jnp.int32, (1, tq), 1)
    own_scores = chunk_scores(own, kpos <= qpos)
    refs = (s_ref, cm_ref, m_ref, acc_ref)
    last = jnp.maximum(own - 1, 0)
    _stage_scores(own_scores, range(s_ref.shape[0] - 1), s_ref, cm_ref)
    _attend_chunk(n_heads, own_scores, chunk_scores(0), values(own), *refs)

    def body(c, carry):
        _attend_chunk(n_heads, chunk_scores(c), chunk_scores(jnp.minimum(c + 1, last)), values(c), *refs)
        return carry

    lax.fori_loop(0, own, body, 0)
    _write_heads(o_ref, acc_ref)


def _moba(qt, k, vt, km, onehot, batch, seq):
    t, width = k.shape
    tq = MOBA_K_CHUNK
    nb = seq // MOBA_BLOCK
    nq = seq // tq
    n_heads = width // HEAD_DIM
    one = pl.Buffered(1)
    return pl.pallas_call(
        _moba_kernel,
        grid=(batch, nq),
        in_specs=[pl.BlockSpec((1, width, tq), lambda b, i: (b, 0, i)),
                  pl.BlockSpec((seq, width), lambda b, i: (b, 0), pipeline_mode=one),
                  pl.BlockSpec((1, nb, width, MOBA_BLOCK), lambda b, i: (b, 0, 0, 0), pipeline_mode=one),
                  pl.BlockSpec((nb, width), lambda b, i: (b, 0)),
                  _const_spec(onehot.shape)],
        out_specs=pl.BlockSpec((tq, width), lambda b, i: (b * nq + i, 0)),
        out_shape=jax.ShapeDtypeStruct((t, width), BF16),
        scratch_shapes=[pltpu.VMEM((n_heads, 2 * LANES, tq), BF16), pltpu.VMEM((MOBA_SLOTS, MOBA_K_CHUNK, tq), F32),
                        pltpu.VMEM((MOBA_SLOTS, SUBLANES, tq), F32),
                        pltpu.VMEM((n_heads, 1, tq), F32), pltpu.VMEM((n_heads, HEAD_DIM + ONES_ROWS, tq), F32)],
        compiler_params=_params(2),
        name="moba",
    )(qt, k, vt, km, onehot)


def _post_kernel(*refs, n_mix, final):
    h_ref = refs[0]
    y_refs = refs[1:1 + n_mix]
    (wo_ref, p_ref, gf_ref, wg_ref, wu_ref, wd_ref, gp_ref, wpg_ref, wpp_ref, gl_ref, o_ref, a_ref) = refs[1 + n_mix:]
    h = h_ref[...]
    lo = 0
    for y_ref in y_refs:
        h = h + _dot(y_ref[...], wo_ref[0, lo:lo + y_ref.shape[1], :])
        lo += y_ref.shape[1]
    xn = _rms(h, gf_ref[...]).astype(BF16)
    d_ff = wg_ref.shape[2]
    step = 2 * LANES
    for lo in range(0, d_ff, step):
        gate = _dot(xn, wg_ref[0, :, lo:lo + step])
        up = _dot(xn, wu_ref[0, :, lo:lo + step])
        a_ref[:, lo:lo + step] = (gate * jax.nn.sigmoid(gate) * up).astype(BF16)
    h = h + _dot(a_ref[...], wd_ref[0])
    pg = jax.nn.sigmoid(_dot(_rms(h, gp_ref[...]).astype(BF16), wpg_ref[0]))
    h = h + _dot(p_ref[0].astype(BF16), wpp_ref[0]) * pg
    o_ref[...] = _rms(h, gl_ref[...]) if final else h


def _layer_spec(shape, layer):
    rest = (0,) * (len(shape) - 1)
    return pl.BlockSpec((1,) + tuple(shape[1:]), lambda *_: (layer,) + rest, pipeline_mode=pl.Buffered(1))


def _post(h, ys, wo, wo_layer, p, layer, gf, wg, wu, wd, gp, wpg, wpp, gl, final):
    t, d = h.shape
    tm = ROW_TILE
    row = lambda n: pl.BlockSpec((tm, n), lambda i: (i, 0))
    stack = lambda w: _layer_spec(w.shape, layer)
    return pl.pallas_call(
        functools.partial(_post_kernel, n_mix=len(ys), final=final),
        grid=(t // tm,),
        in_specs=[row(d)] + [row(y.shape[1]) for y in ys]
                 + [_layer_spec(wo.shape, wo_layer), pl.BlockSpec((1, tm, p.shape[2]), lambda i: (layer, i, 0)),
                    _const_spec(gf.shape), stack(wg), stack(wu), stack(wd), _const_spec(gp.shape), stack(wpg),
                    stack(wpp), _const_spec(gl.shape)],
        out_specs=row(d),
        out_shape=jax.ShapeDtypeStruct((t, d), F32),
        scratch_shapes=[pltpu.VMEM((tm, wg.shape[2]), BF16)],
        compiler_params=_params(1),
        name="post_final" if final else "post",
    )(h, *ys, wo, p, gf, wg, wu, wd, gp, wpg, wpp, gl)


def _front1_kernel(h_ref, g_ref, w_ref, c_ref, sa_ref, sb_ref, lng_ref, lnb_ref,
                   qt_ref, k_ref, vt_ref, qit_ref, ki_ref, wit_ref):
    xn = _rms(h_ref[...], g_ref[...]).astype(BF16)

    def proj(lo, hi):
        return _dot(xn, w_ref[:, lo:hi])

    c, sa, sb = c_ref[...], sa_ref[...], sb_ref[...]
    qt_ref[0] = (_rope(proj(0, WIDTH_C), c, sa, sb) * (HEAD_DIM ** -0.5 * LOG2E)).T.astype(BF16)
    k_ref[...] = _rope(proj(WIDTH_C, 2 * WIDTH_C), c, sa, sb).astype(BF16)
    vt_ref[0, 0] = proj(2 * WIDTH_C, 3 * WIDTH_C).T.astype(BF16)
    o = 3 * WIDTH_C
    qit_ref[0] = _rope(proj(o, o + IDX_HEADS * IDX_DIM), c, sa, sb).T.astype(BF16)
    o += IDX_HEADS * IDX_DIM
    ki = proj(o, o + LANES)
    xc = ki - jnp.mean(ki, axis=-1, keepdims=True)
    var = jnp.mean(xc * xc, axis=-1, keepdims=True)
    ki = xc * lax.rsqrt(var + EPS) * lng_ref[...] + lnb_ref[...]
    ki_ref[...] = _rope(ki, c, sa, sb).astype(BF16)
    wi = proj(o + LANES, o + 2 * LANES) * ((IDX_HEADS ** -0.5) * (IDX_DIM ** -0.5))
    wit_ref[0] = wi.T[:IDX_HEADS]


def _front1(h, g, w, c, sa, sb, lng, lnb, batch, seq):
    t, d = h.shape
    tm = ROW_TILE
    assert tm == DSA_K_CHUNK
    nt = seq // tm
    row = lambda n: pl.BlockSpec((tm, n), lambda i: (i, 0))
    tr = lambda n: pl.BlockSpec((1, n, tm), lambda i: (i // nt, 0, i % nt))
    return pl.pallas_call(
        _front1_kernel,
        grid=(t // tm,),
        in_specs=[row(d), _const_spec((1, d)), _const_spec(w.shape), row(LANES), row(LANES), row(LANES),
                  _const_spec(lng.shape), _const_spec(lnb.shape)],
        out_specs=[tr(WIDTH_C), row(WIDTH_C),
                   pl.BlockSpec((1, 1, WIDTH_C, tm), lambda i: (i // nt, i % nt, 0, 0)),
                   tr(IDX_HEADS * IDX_DIM), row(LANES), tr(IDX_HEADS)],
        out_shape=[jax.ShapeDtypeStruct((batch, WIDTH_C, seq), BF16),
                   jax.ShapeDtypeStruct((t, WIDTH_C), BF16),
                   jax.ShapeDtypeStruct((batch, nt, WIDTH_C, tm), BF16),
                   jax.ShapeDtypeStruct((batch, IDX_HEADS * IDX_DIM, seq), BF16),
                   jax.ShapeDtypeStruct((t, LANES), BF16),
                   jax.ShapeDtypeStruct((batch, IDX_HEADS, seq), F32)],
        compiler_params=_params(1),
        name="front1",
    )(h, g, w, c, sa, sb, lng, lnb)


def _dsa_index_kernel(qit_ref, wit_ref, ki_ref, b_ref, sc_ref, sc16_ref, *, n_sel):
    i = pl.program_id(1)
    tq = qit_ref.shape[2]
    kc_n = DSA_K_CHUNK
    n_chunks_total = sc_ref.shape[0]
    n_chunks = (i * tq + tq + kc_n - 1) // kc_n
    qpos = i * tq + lax.broadcasted_iota(jnp.int32, (1, tq), 1)
    kofs = lax.broadcasted_iota(jnp.int32, (kc_n, 1), 0)
    wv = wit_ref[0]

    def causal(c):
        return (c * kc_n + kofs) <= qpos

    big = 3e38

    def score_chunk(c, carry):
        mx, mn = carry
        kc = ki_ref[pl.ds(pl.multiple_of(c * kc_n, kc_n), kc_n), :]
        acc = jnp.zeros((kc_n, tq), F32)
        for h in range(IDX_HEADS):
            qm = _head_rows(qit_ref[0, (h // 2) * LANES:(h // 2 + 1) * LANES, :], h % 2)
            acc = acc + jnp.maximum(_dot(kc, qm), 0.0) * wv[h:h + 1, :]
        ok = causal(c)
        sc_ref[c] = jnp.where(ok, acc, NEG)
        sc16_ref[c] = jnp.where(ok, acc, NEG).astype(BF16)
        return (jnp.maximum(mx, _fold_rows(jnp.where(ok, acc, -big), jnp.maximum)),
                jnp.minimum(mn, _fold_rows(jnp.where(ok, acc, big), jnp.minimum)))

    mx, mn = lax.fori_loop(0, n_chunks, score_chunk,
                           (jnp.full((SUBLANES, tq), -big, F32), jnp.full((SUBLANES, tq), big, F32)))
    col_max = jnp.max(mx, axis=0, keepdims=True)
    col_min = jnp.min(mn, axis=0, keepdims=True)

    def reduce_chunks(fn, op, init):
        def body(c, acc):
            return op(acc, _fold_rows(fn(sc_ref[c]), op))
        return lax.fori_loop(0, n_chunks, body, jnp.full((SUBLANES, tq), init, F32))

    def count_ge(thr):
        acc = reduce_chunks(lambda x: jnp.where(x >= thr, 1.0, 0.0), jnp.add, 0.0)
        return jnp.sum(acc, axis=0, keepdims=True)

    def pending(flags):
        return jnp.max(jnp.where(flags, 0.0, 1.0))

    k_f = float(n_sel)
    n_causal = (qpos + 1).astype(F32)
    small = n_causal < k_f
    rows16 = 2 * SUBLANES

    def count_ge_bf16(thr):
        one, zero = jnp.ones((), BF16), jnp.zeros((), BF16)

        def body(c, acc):
            hits = jnp.where(sc16_ref[c] >= thr, one, zero)
            return acc + _fold_rows(hits, jnp.add, rows16).astype(F32)

        return jnp.sum(lax.fori_loop(0, n_chunks, body, jnp.zeros((rows16, tq), F32)), axis=0, keepdims=True)

    def coarse(_, carry):
        lo, hi = carry
        mid = (0.5 * (lo + hi)).astype(BF16)
        ok = count_ge_bf16(mid) >= k_f
        mid = mid.astype(F32)
        return jnp.where(ok, mid, lo), jnp.where(ok, hi, mid)

    top = col_max.astype(BF16).astype(F32)
    lo, hi = lax.fori_loop(0, COARSE_STEPS, coarse,
                           (col_min.astype(BF16).astype(F32),
                            (top + jnp.abs(top) * 2.0 * BF16_SPACING + TINY).astype(BF16).astype(F32)))
    lo = jnp.where(small, NEG, lo - jnp.abs(lo) * BF16_SPACING - TINY)
    c_lo = jnp.full((1, tq), 2.0 * k_f, F32)
    c_hi = jnp.full((1, tq), -1.0, F32)

    def bisect(_, carry):
        lo, hi, c_lo, c_hi = carry
        mid = 0.5 * (lo + hi)
        cnt = count_ge(mid)
        ok = cnt >= k_f
        return (jnp.where(ok, mid, lo), jnp.where(ok, hi, mid), jnp.where(ok, cnt, c_lo), jnp.where(ok, c_hi, cnt))

    def bracket_ends(lo, hi):
        def body(c, carry):
            below, above = carry
            x = sc_ref[c]
            return (jnp.maximum(below, _fold_rows(jnp.where(x < hi, x, -big), jnp.maximum)),
                    jnp.minimum(above, _fold_rows(jnp.where(x >= lo, x, big), jnp.minimum)))

        below, above = lax.fori_loop(0, n_chunks, body, (jnp.full((SUBLANES, tq), -big, F32),
                                                         jnp.full((SUBLANES, tq), big, F32)))
        return jnp.max(below, axis=0, keepdims=True), jnp.min(above, axis=0, keepdims=True)

    def settled(c_lo, c_hi, below, above):
        return small | (c_hi == k_f - 1.0) | (c_lo == k_f) | (below == above)

    def narrow_cond(carry):
        return (carry[6] < BISECT_ROUNDS) & (carry[7] > 0.5)

    def narrow(carry):
        state = lax.fori_loop(0, BISECT_PER_ROUND, bisect, carry[:4])
        below, above = bracket_ends(state[0], state[1])
        return (*state, below, above, carry[6] + 1, pending(settled(state[2], state[3], below, above)))

    zeros = jnp.zeros((1, tq), F32)
    lo, hi, c_lo, c_hi, below, above, _, _ = lax.while_loop(
        narrow_cond, narrow, (lo, hi, c_lo, c_hi, zeros, zeros, jnp.int32(0), jnp.float32(1.0)))
    thr = jnp.where(small, NEG, jnp.where(c_lo == k_f, above, below))
    done = jnp.where(settled(c_lo, c_hi, below, above), 1.0, 0.0)

    def snap_cond(carry):
        return carry[3] > 0.5

    def snap(carry):
        hi, thr, done, _ = carry
        cand = jnp.max(reduce_chunks(lambda x: jnp.where(x < hi, x, -big), jnp.maximum, -big),
                       axis=0, keepdims=True)
        ok = count_ge(cand) >= k_f
        thr = jnp.where(done > 0.5, thr, cand)
        done = jnp.where(ok, 1.0, done)
        hi = jnp.where(done > 0.5, hi, cand)
        return hi, thr, done, jnp.max(1.0 - done)

    _, thr, _, _ = lax.while_loop(snap_cond, snap, (hi, thr, done, jnp.max(1.0 - done)))

    n_ge = count_ge(thr)
    tied = jnp.max(jnp.where((n_ge > k_f) & jnp.logical_not(small), 1.0, 0.0))

    @pl.when(tied < 0.5)
    def _():
        def body(c, carry):
            x = sc_ref[c]
            b_ref[0, 0, c] = jnp.where((x >= thr) & causal(c), 0.0, NEG)
            return carry
        lax.fori_loop(0, n_chunks, body, 0)

    @pl.when(tied > 0.5)
    def _():
        n_gt = reduce_chunks(lambda x: jnp.where(x > thr, 1.0, 0.0), jnp.add, 0.0)
        need = k_f - jnp.sum(n_gt, axis=0, keepdims=True)

        def body(c, seen):
            x = sc_ref[c]
            eq = x == thr
            total = seen + jnp.sum(_fold_rows(jnp.where(eq, 1.0, 0.0), jnp.add), axis=0, keepdims=True)
            keep_all = total <= need
            crossing = jnp.max(jnp.where(keep_all | (seen >= need), 0.0, 1.0))

            @pl.when(crossing < 0.5)
            def _():
                b_ref[0, 0, c] = jnp.where(((x > thr) | (eq & keep_all)) & causal(c), 0.0, NEG)

            @pl.when(crossing > 0.5)
            def _():
                r = lax.broadcasted_iota(jnp.int32, (kc_n, kc_n), 0)
                cc = lax.broadcasted_iota(jnp.int32, (kc_n, kc_n), 1)
                prefix = jnp.where(cc <= r, 1.0, 0.0).astype(BF16)
                cnt = seen + _dot(prefix, jnp.where(eq, 1.0, 0.0).astype(BF16))
                b_ref[0, 0, c] = jnp.where(((x > thr) | (eq & (cnt <= need))) & causal(c), 0.0, NEG)

            return total
        lax.fori_loop(0, n_chunks, body, jnp.zeros((1, tq), F32))

    def fill(c, carry):
        b_ref[0, 0, c] = jnp.full((kc_n, tq), NEG, F32)
        return carry

    lax.fori_loop(n_chunks, n_chunks_total, fill, 0)


def _dsa_index(qit, wit, ki, batch, seq):
    tq = Q_TILE
    nq = seq // tq
    nch = seq // DSA_K_CHUNK
    n_sel = min(DSA_TOPK, seq // 4)
    return pl.pallas_call(
        functools.partial(_dsa_index_kernel, n_sel=n_sel),
        grid=(batch, nq),
        in_specs=[pl.BlockSpec((1, qit.shape[1], tq), lambda b, i: (b, 0, i)),
                  pl.BlockSpec((1, IDX_HEADS, tq), lambda b, i: (b, 0, i)),
                  pl.BlockSpec((seq, LANES), lambda b, i: (b, 0))],
        out_specs=pl.BlockSpec((1, 1, nch, DSA_K_CHUNK, tq), lambda b, i: (b, i, 0, 0, 0)),
        out_shape=jax.ShapeDtypeStruct((batch, nq, nch, DSA_K_CHUNK, tq), F32),
        scratch_shapes=[pltpu.VMEM((nch, DSA_K_CHUNK, tq), F32), pltpu.VMEM((nch, DSA_K_CHUNK, tq), BF16)],
        compiler_params=_params(2),
        name="dsa_index",
    )(qit, wit, ki)


def _dsa_attn_kernel(qt_ref, k_ref, vt_ref, b_ref, o_ref, qm_ref, s_ref, cm_ref, m_ref, acc_ref):
    i = pl.program_id(1)
    tq = qt_ref.shape[2]
    kc_n = DSA_K_CHUNK
    n_heads = acc_ref.shape[0]
    n_chunks = (i * tq + tq + kc_n - 1) // kc_n
    _softmax_init(m_ref, acc_ref)
    for h in range(n_heads):
        qm_ref[h] = _head_rows(qt_ref[0, (h // 2) * LANES:(h // 2 + 1) * LANES, :], h % 2)

    def chunk_scores(c):
        st = pl.multiple_of(c * kc_n, kc_n)

        def scores(h):
            return _dot(k_ref[pl.ds(st, kc_n), (h // 2) * LANES:(h // 2 + 1) * LANES], qm_ref[h]) + b_ref[0, 0, c]

        return scores

    _stage_scores(chunk_scores(0), range(s_ref.shape[0] - 1), s_ref, cm_ref)

    def body(c, carry):
        def values(h):
            return vt_ref[0, c, h * HEAD_DIM:(h + 1) * HEAD_DIM, :]

        _attend_chunk(n_heads, chunk_scores(c), chunk_scores(jnp.minimum(c + 1, n_chunks - 1)), values,
                      s_ref, cm_ref, m_ref, acc_ref)
        return carry

    lax.fori_loop(0, n_chunks, body, 0)
    _write_heads(o_ref, acc_ref)


def _dsa_attn(qt, k, vt, bias, batch, seq):
    t, width = k.shape
    tq = Q_TILE
    nq = seq // tq
    n_heads = width // HEAD_DIM
    one = pl.Buffered(1)
    return pl.pallas_call(
        _dsa_attn_kernel,
        grid=(batch, nq),
        in_specs=[pl.BlockSpec((1, width, tq), lambda b, i: (b, 0, i)),
                  pl.BlockSpec((seq, width), lambda b, i: (b, 0), pipeline_mode=one),
                  pl.BlockSpec((1,) + vt.shape[1:], lambda b, i: (b, 0, 0, 0), pipeline_mode=one),
                  pl.BlockSpec((1, 1) + bias.shape[2:], lambda b, i: (b, i, 0, 0, 0))],
        out_specs=pl.BlockSpec((tq, width), lambda b, i: (b * nq + i, 0)),
        out_shape=jax.ShapeDtypeStruct((t, width), BF16),
        scratch_shapes=[pltpu.VMEM((n_heads, LANES, tq), BF16), pltpu.VMEM((SCORE_SLOTS, DSA_K_CHUNK, tq), F32),
                        pltpu.VMEM((SCORE_SLOTS, SUBLANES, tq), F32),
                        pltpu.VMEM((n_heads, 1, tq), F32), pltpu.VMEM((n_heads, HEAD_DIM + ONES_ROWS, tq), F32)],
        compiler_params=_params(2),
        name="dsa_attn",
    )(qt, k, vt, bias)


def _rope_tables(positions):
    inv_freq = ROPE_THETA ** (-jnp.arange(0, ROPE_DIM, 2, dtype=F32) / ROPE_DIM)
    m = jnp.arange(LANES) % HEAD_DIM
    ang = positions.astype(F32).reshape(-1, 1) * inv_freq[m % ROPE_HALF][None, :]
    cos, sin = jnp.cos(ang), jnp.sin(ang)
    c = jnp.where(m < ROPE_DIM, cos, 1.0)
    sa = jnp.where(m < ROPE_HALF, -sin, 0.0)
    sb = jnp.where((m >= ROPE_HALF) & (m < ROPE_DIM), sin, 0.0)
    return c, sa, sb


def kernel(x, p, positions, g_mix, w_in_even, a_ln_g, a_ln_b, a_w_s, a_b_s, w_out_even, w_in_odd, c_kidx_ln_g, c_kidx_ln_b, w_out_odd, g_ffn, w_ffn_gate, w_ffn_up, w_ffn_down, g_ple, w_ple_proj, w_ple_gate, g_final):
    batch, seq, d = x.shape
    depth = p.shape[0]
    t = batch * seq
    assert seq % ROW_TILE == 0 and seq % DSA_K_CHUNK == 0 and min(DSA_TOPK, seq // 4) <= DSA_K_CHUNK
    c, sa, sb = _rope_tables(positions)
    h = x.reshape(t, d)
    p2 = p.reshape(depth, t, p.shape[-1])
    row = lambda a: a.reshape(1, -1)
    onehot = (jnp.arange(seq)[:, None] // MOBA_BLOCK == jnp.arange(LANES)[None, :]).astype(BF16)
    bf = lambda w: w.astype(BF16)
    wo_even, wo_odd, wg, wu, wd, wpg, wpp = map(bf, (w_out_even, w_out_odd, w_ffn_gate, w_ffn_up, w_ffn_down,
                                                     w_ple_gate, w_ple_proj))
    for i in range(depth):
        j = i // 2
        if i % 2 == 0:
            bs = jnp.repeat(a_b_s[j].T, HEAD_DIM, axis=1)
            ya, qt, k, vt, km = _front0(h, row(g_mix[i]), bf(w_in_even[j]), c, sa, sb,
                                        row(a_ln_g[j]), row(a_ln_b[j]), a_w_s[j], bs, batch, seq)
            ys = [ya, _moba(qt, k, vt, km.reshape(t // MOBA_BLOCK, WIDTH_B), onehot, batch, seq)]
            wo = wo_even
        else:
            w = w_in_odd[j]
            o = 3 * WIDTH_C + IDX_HEADS * IDX_DIM
            w1 = bf(jnp.concatenate([w[:, :o], w[:, o:o + IDX_DIM], w[:, o:o + IDX_DIM], w[:, o + IDX_DIM:],
                                     jnp.zeros((d, LANES - IDX_HEADS), w.dtype)], axis=1))
            qt, k, vt, qit, ki, wit = _front1(h, row(g_mix[i]), w1, c, sa, sb, row(jnp.tile(c_kidx_ln_g[j], 2)),
                                              row(jnp.tile(c_kidx_ln_b[j], 2)), batch, seq)
            ys = [_dsa_attn(qt, k, vt, _dsa_index(qit, wit, ki, batch, seq), batch, seq)]
            wo = wo_odd
        h = _post(h, ys, wo, j, p2, i, row(g_ffn[i]), wg, wu, wd, row(g_ple[i]), wpg, wpp, row(g_final),
                  final=(i == depth - 1))
    return h.reshape(batch, seq, d)
```

```python
import functools

import jax
import jax.numpy as jnp
from jax import lax
from jax.experimental import pallas as pl
from jax.experimental.pallas import tpu as pltpu

HEAD_DIM = 64
ROPE_DIM = HEAD_DIM // 4
ROPE_HALF = ROPE_DIM // 2
ROPE_THETA = 500000.0
N_GROUPS_A = 8
WIDTH_A = N_GROUPS_A * HEAD_DIM
N_HEADS_B = 8
WIDTH_B = N_HEADS_B * HEAD_DIM
N_HEADS_C = 16
WIDTH_C = N_HEADS_C * HEAD_DIM
SGU_CHUNK = 128
MOBA_BLOCK = 256
MOBA_TOPK = 3
IDX_HEADS = 8
IDX_DIM = 64
DSA_TOPK = 256
EPS = 1e-6
NEG = -1e30
LOG2E = 1.4426950408889634
BF16_SPACING = 2.0 ** -7
TINY = 1e-30

LANES = 128
SUBLANES = 8
ROW_TILE = 512
Q_TILE = 256
DSA_K_CHUNK = 512
FOLD_CHAINS = 8
ONES_ROWS = 16
SCORE_SLOTS = 4
MOBA_SLOTS = 4
MOBA_K_CHUNK = 2 * MOBA_BLOCK
COARSE_STEPS = 11
BISECT_PER_ROUND = 8
BISECT_ROUNDS = 12
VMEM_LIMIT = 56 * 1024 * 1024

F32 = jnp.float32
BF16 = jnp.bfloat16


def _dot(a, b):
    return jnp.dot(a, b, preferred_element_type=F32)


def _rms(x, g):
    return x * lax.rsqrt(jnp.mean(x * x, axis=-1, keepdims=True) + EPS) * g


def _rope(z, c, sa, sb):
    outs = []
    for j in range(z.shape[1] // LANES):
        zz = z[:, j * LANES:(j + 1) * LANES]
        outs.append(zz * c + pltpu.roll(zz, LANES - ROPE_HALF, 1) * sa + pltpu.roll(zz, ROPE_HALF, 1) * sb)
    return outs[0] if len(outs) == 1 else jnp.concatenate(outs, axis=1)


def _gelu(x):
    return 0.5 * x * (1.0 + lax.erf(x * (2.0 ** -0.5)))


def _fold_rows(x, op, group=SUBLANES):
    parts = [x[a:a + group] for a in range(0, x.shape[0], group)]
    chains = parts[:FOLD_CHAINS]
    for a, part in enumerate(parts[FOLD_CHAINS:]):
        chains[a % FOLD_CHAINS] = op(chains[a % FOLD_CHAINS], part)
    while len(chains) > 1:
        chains = [op(chains[a], chains[a + 1]) if a + 1 < len(chains) else chains[a] for a in range(0, len(chains), 2)]
    return chains[0]


def _head_rows(x, hh):
    r = lax.broadcasted_iota(jnp.int32, (LANES, 1), 0)
    return jnp.where((r >= hh * HEAD_DIM) & (r < (hh + 1) * HEAD_DIM), x, jnp.zeros_like(x))


def _const_spec(shape):
    nd = len(shape)
    return pl.BlockSpec(shape, lambda *_: (0,) * nd, pipeline_mode=pl.Buffered(1))


def _params(n_axes):
    return pltpu.CompilerParams(dimension_semantics=("arbitrary",) * n_axes, vmem_limit_bytes=VMEM_LIMIT)


def _stage_scores(score_fn, heads, s_ref, cm_ref):
    n_slots = s_ref.shape[0]
    for h in heads:
        s = score_fn(h)
        s_ref[h % n_slots] = s
        cm_ref[h % n_slots] = _fold_rows(s, jnp.maximum)


def _attend_chunk(n_heads, score_fn, next_score_fn, vt_fn, s_ref, cm_ref, m_ref, acc_ref):
    n_slots = s_ref.shape[0]
    ahead = n_slots - 1
    assert n_heads % n_slots == 0
    for h in range(n_heads):
        if h + ahead < n_heads:
            _stage_scores(score_fn, [h + ahead], s_ref, cm_ref)
        else:
            _stage_scores(next_score_fn, [h + ahead - n_heads], s_ref, cm_ref)
        slot = h % n_slots
        m_old = m_ref[h]
        m_new = jnp.maximum(m_old, jnp.max(cm_ref[slot], axis=0, keepdims=True))
        p = jnp.exp2(s_ref[slot] - m_new).astype(BF16)
        m_ref[h] = m_new
        vt = vt_fn(h)
        vt = jnp.concatenate([vt, jnp.ones((acc_ref.shape[1] - vt.shape[0], vt.shape[1]), BF16)], axis=0)
        acc_ref[h] = jnp.exp2(m_old - m_new) * acc_ref[h] + _dot(vt, p)


def _softmax_init(m_ref, acc_ref):
    m_ref[...] = jnp.full(m_ref.shape, -jnp.inf, F32)
    acc_ref[...] = jnp.zeros(acc_ref.shape, F32)


def _write_heads(o_ref, acc_ref):
    def head(h):
        return acc_ref[h, :HEAD_DIM, :] / acc_ref[h, HEAD_DIM:HEAD_DIM + 1, :]

    for p in range(acc_ref.shape[0] // 2):
        o_ref[:, p * LANES:(p + 1) * LANES] = jnp.concatenate([head(2 * p), head(2 * p + 1)], axis=0).T.astype(BF16)


def _front0_kernel(h_ref, g_ref, w_ref, c_ref, sa_ref, sb_ref, lng_ref, lnb_ref, ws_ref, bs_ref,
                   ya_ref, qt_ref, k_ref, vt_ref, km_ref):
    tm = h_ref.shape[0]
    xn = _rms(h_ref[...], g_ref[...]).astype(BF16)

    def proj(lo, hi):
        return _dot(xn, w_ref[:, lo:hi])

    u = _gelu(proj(0, WIDTH_A))
    v = _gelu(proj(WIDTH_A, 2 * WIDTH_A))
    lane = lax.broadcasted_iota(jnp.int32, (1, LANES), 1)
    low = lane < HEAD_DIM
    row = lax.broadcasted_iota(jnp.int32, (SGU_CHUNK, SGU_CHUNK), 0)
    col = lax.broadcasted_iota(jnp.int32, (SGU_CHUNK, SGU_CHUNK), 1)
    tril = col <= row
    for j in range(WIDTH_A // LANES):
        sl = slice(j * LANES, (j + 1) * LANES)
        vv = v[:, sl]

        def half_mean(t):
            s_lo = jnp.sum(jnp.where(low, t, 0.0), axis=-1, keepdims=True)
            s_hi = jnp.sum(jnp.where(low, 0.0, t), axis=-1, keepdims=True)
            return jnp.where(low, s_lo, s_hi) * (1.0 / HEAD_DIM)

        xc = vv - half_mean(vv)
        var = half_mean(xc * xc)
        vn = (xc * lax.rsqrt(var + EPS) * lng_ref[:, sl] + lnb_ref[:, sl]).astype(BF16)
        w_lo = jnp.where(tril, ws_ref[2 * j], 0.0).astype(BF16)
        w_hi = jnp.where(tril, ws_ref[2 * j + 1], 0.0).astype(BF16)
        for c in range(tm // SGU_CHUNK):
            rs = slice(c * SGU_CHUNK, (c + 1) * SGU_CHUNK)
            vc = vn[rs]
            mixed = jnp.where(low, _dot(w_lo, vc), _dot(w_hi, vc)) + bs_ref[:, sl]
            ya_ref[rs, sl] = (u[rs, sl] * mixed).astype(BF16)

    o = 2 * WIDTH_A
    c, sa, sb = c_ref[...], sa_ref[...], sb_ref[...]
    q = _rope(proj(o, o + WIDTH_B), c, sa, sb) * (HEAD_DIM ** -0.5 * LOG2E)
    qt_ref[0] = q.T.astype(BF16)
    k = _rope(proj(o + WIDTH_B, o + 2 * WIDTH_B), c, sa, sb)
    k_ref[...] = k.astype(BF16)
    v_b = proj(o + 2 * WIDTH_B, o + 3 * WIDTH_B)
    for blk in range(tm // MOBA_BLOCK):
        rs = slice(blk * MOBA_BLOCK, (blk + 1) * MOBA_BLOCK)
        km_ref[0, blk:blk + 1, :] = jnp.mean(k[rs], axis=0, keepdims=True)
        vt_ref[0, blk] = v_b[rs].T.astype(BF16)


def _front0(h, g, w, c, sa, sb, lng, lnb, ws, bs, batch, seq):
    t, d = h.shape
    tm = ROW_TILE
    nt = seq // tm
    nblk = tm // MOBA_BLOCK
    row = lambda n: pl.BlockSpec((tm, n), lambda i: (i, 0))
    return pl.pallas_call(
        _front0_kernel,
        grid=(t // tm,),
        in_specs=[row(d), _const_spec((1, d)), _const_spec(w.shape), row(LANES), row(LANES), row(LANES),
                  _const_spec(lng.shape), _const_spec(lnb.shape), _const_spec(ws.shape), _const_spec(bs.shape)],
        out_specs=[row(WIDTH_A),
                   pl.BlockSpec((1, WIDTH_B, tm), lambda i: (i // nt, 0, i % nt)),
                   row(WIDTH_B),
                   pl.BlockSpec((1, nblk, WIDTH_B, MOBA_BLOCK), lambda i: (i // nt, i % nt, 0, 0)),
                   pl.BlockSpec((1, nblk, WIDTH_B), lambda i: (i, 0, 0))],
        out_shape=[jax.ShapeDtypeStruct((t, WIDTH_A), BF16),
                   jax.ShapeDtypeStruct((batch, WIDTH_B, seq), BF16),
                   jax.ShapeDtypeStruct((t, WIDTH_B), BF16),
                   jax.ShapeDtypeStruct((batch, seq // MOBA_BLOCK, WIDTH_B, MOBA_BLOCK), BF16),
                   jax.ShapeDtypeStruct((t // tm, nblk, WIDTH_B), F32)],
        compiler_params=_params(1),
        name="front0",
    )(h, g, w, c, sa, sb, lng, lnb, ws, bs)


def _moba_kernel(qt_ref, k_ref, vt_ref, km_ref, oh_ref, o_ref, qc_ref, s_ref, cm_ref, m_ref, acc_ref):
    i = pl.program_id(1)
    tq = qt_ref.shape[2]
    nb = km_ref.shape[0]
    n_heads = acc_ref.shape[0]
    kc_n = s_ref.shape[1]
    per_chunk = kc_n // MOBA_BLOCK
    blk = lax.broadcasted_iota(jnp.int32, (nb, 1), 0)
    blk_f = blk.astype(F32)
    q_blk = (i * tq + lax.broadcasted_iota(jnp.int32, (1, tq), 1)) // MOBA_BLOCK
    past = blk < q_blk
    _softmax_init(m_ref, acc_ref)
    for h in range(n_heads):
        sl = slice((h // 2) * LANES, (h // 2 + 1) * LANES)
        qm = _head_rows(qt_ref[0, sl, :], h % 2)
        g = jnp.where(past, _dot(km_ref[:, sl].astype(BF16), qm), NEG)
        sel = jnp.zeros(g.shape, jnp.bool_)
        for _ in range(min(MOBA_TOPK, nb)):
            top = jnp.max(g, axis=0, keepdims=True)
            idx = jnp.min(jnp.where(g == top, blk_f, 1e9), axis=0, keepdims=True)
            pick = blk_f == idx
            sel = sel | pick
            g = jnp.where(pick, -jnp.inf, g)
        selb = jnp.where((sel & past) | (blk == q_blk), 0.0, NEG).astype(BF16)
        qc_ref[h] = jnp.concatenate([qm, selb, jnp.zeros((LANES - nb, tq), BF16)], axis=0)

    def values(c):
        def vt(h):
            rows = slice(h * HEAD_DIM, (h + 1) * HEAD_DIM)
            return jnp.concatenate([vt_ref[0, per_chunk * c + a, rows, :] for a in range(per_chunk)], axis=1)
        return vt

    def chunk_scores(c, mask=None):
        st = pl.multiple_of(c * kc_n, kc_n)

        def scores(h):
            kc = jnp.concatenate([k_ref[pl.ds(st, kc_n), (h // 2) * LANES:(h // 2 + 1) * LANES],
                                  oh_ref[pl.ds(st, kc_n), :]], axis=1)
            s = _dot(kc, qc_ref[h])
            return s if mask is None else jnp.where(mask, s, NEG)

        return scores

    own = (i * tq) // kc_n
    kpos = own * kc_n + lax.broadcasted_iota(jnp.int32, (kc_n, 1), 0)
    qpos = i * tq + lax.broadcasted_iota(jnp.int32, (1, tq), 1)
    own_scores = chunk_scores(own, kpos <= qpos)
    refs = (s_ref, cm_ref, m_ref, acc_ref)
    last = jnp.maximum(own - 1, 0)
    _stage_scores(own_scores, range(s_ref.shape[0] - 1), s_ref, cm_ref)
    _attend_chunk(n_heads, own_scores, chunk_scores(0), values(own), *refs)

    def body(c, carry):
        _attend_chunk(n_heads, chunk_scores(c), chunk_scores(jnp.minimum(c + 1, last)), values(c), *refs)
        return carry

    lax.fori_loop(0, own, body, 0)
    _write_heads(o_ref, acc_ref)


def _moba(qt, k, vt, km, onehot, batch, seq):
    t, width = k.shape
    tq = MOBA_K_CHUNK
    nb = seq // MOBA_BLOCK
    nq = seq // tq
    n_heads = width // HEAD_DIM
    one = pl.Buffered(1)
    return pl.pallas_call(
        _moba_kernel,
        grid=(batch, nq),
        in_specs=[pl.BlockSpec((1, width, tq), lambda b, i: (b, 0, i)),
                  pl.BlockSpec((seq, width), lambda b, i: (b, 0), pipeline_mode=one),
                  pl.BlockSpec((1, nb, width, MOBA_BLOCK), lambda b, i: (b, 0, 0, 0), pipeline_mode=one),
                  pl.BlockSpec((nb, width), lambda b, i: (b, 0)),
                  _const_spec(onehot.shape)],
        out_specs=pl.BlockSpec((tq, width), lambda b, i: (b * nq + i, 0)),
        out_shape=jax.ShapeDtypeStruct((t, width), BF16),
        scratch_shapes=[pltpu.VMEM((n_heads, 2 * LANES, tq), BF16), pltpu.VMEM((MOBA_SLOTS, MOBA_K_CHUNK, tq), F32),
                        pltpu.VMEM((MOBA_SLOTS, SUBLANES, tq), F32),
                        pltpu.VMEM((n_heads, 1, tq), F32), pltpu.VMEM((n_heads, HEAD_DIM + ONES_ROWS, tq), F32)],
        compiler_params=_params(2),
        name="moba",
    )(qt, k, vt, km, onehot)


def _post_kernel(*refs, n_mix, final):
    h_ref = refs[0]
    y_refs = refs[1:1 + n_mix]
    (wo_ref, p_ref, gf_ref, wg_ref, wu_ref, wd_ref, gp_ref, wpg_ref, wpp_ref, gl_ref, o_ref, a_ref) = refs[1 + n_mix:]
    h = h_ref[...]
    lo = 0
    for y_ref in y_refs:
        h = h + _dot(y_ref[...], wo_ref[0, lo:lo + y_ref.shape[1], :])
        lo += y_ref.shape[1]
    xn = _rms(h, gf_ref[...]).astype(BF16)
    d_ff = wg_ref.shape[2]
    step = 2 * LANES
    for lo in range(0, d_ff, step):
        gate = _dot(xn, wg_ref[0, :, lo:lo + step])
        up = _dot(xn, wu_ref[0, :, lo:lo + step])
        a_ref[:, lo:lo + step] = (gate * jax.nn.sigmoid(gate) * up).astype(BF16)
    h = h + _dot(a_ref[...], wd_ref[0])
    pg = jax.nn.sigmoid(_dot(_rms(h, gp_ref[...]).astype(BF16), wpg_ref[0]))
    h = h + _dot(p_ref[0].astype(BF16), wpp_ref[0]) * pg
    o_ref[...] = _rms(h, gl_ref[...]) if final else h


def _layer_spec(shape, layer):
    rest = (0,) * (len(shape) - 1)
    return pl.BlockSpec((1,) + tuple(shape[1:]), lambda *_: (layer,) + rest, pipeline_mode=pl.Buffered(1))


def _post(h, ys, wo, wo_layer, p, layer, gf, wg, wu, wd, gp, wpg, wpp, gl, final):
    t, d = h.shape
    tm = ROW_TILE
    row = lambda n: pl.BlockSpec((tm, n), lambda i: (i, 0))
    stack = lambda w: _layer_spec(w.shape, layer)
    return pl.pallas_call(
        functools.partial(_post_kernel, n_mix=len(ys), final=final),
        grid=(t // tm,),
        in_specs=[row(d)] + [row(y.shape[1]) for y in ys]
                 + [_layer_spec(wo.shape, wo_layer), pl.BlockSpec((1, tm, p.shape[2]), lambda i: (layer, i, 0)),
                    _const_spec(gf.shape), stack(wg), stack(wu), stack(wd), _const_spec(gp.shape), stack(wpg),
                    stack(wpp), _const_spec(gl.shape)],
        out_specs=row(d),
        out_shape=jax.ShapeDtypeStruct((t, d), F32),
        scratch_shapes=[pltpu.VMEM((tm, wg.shape[2]), BF16)],
        compiler_params=_params(1),
        name="post_final" if final else "post",
    )(h, *ys, wo, p, gf, wg, wu, wd, gp, wpg, wpp, gl)


def _front1_kernel(h_ref, g_ref, w_ref, c_ref, sa_ref, sb_ref, lng_ref, lnb_ref,
                   qt_ref, k_ref, vt_ref, qit_ref, ki_ref, wit_ref):
    xn = _rms(h_ref[...], g_ref[...]).astype(BF16)

    def proj(lo, hi):
        return _dot(xn, w_ref[:, lo:hi])

    c, sa, sb = c_ref[...], sa_ref[...], sb_ref[...]
    qt_ref[0] = (_rope(proj(0, WIDTH_C), c, sa, sb) * (HEAD_DIM ** -0.5 * LOG2E)).T.astype(BF16)
    k_ref[...] = _rope(proj(WIDTH_C, 2 * WIDTH_C), c, sa, sb).astype(BF16)
    vt_ref[0, 0] = proj(2 * WIDTH_C, 3 * WIDTH_C).T.astype(BF16)
    o = 3 * WIDTH_C
    qit_ref[0] = _rope(proj(o, o + IDX_HEADS * IDX_DIM), c, sa, sb).T.astype(BF16)
    o += IDX_HEADS * IDX_DIM
    ki = proj(o, o + LANES)
    xc = ki - jnp.mean(ki, axis=-1, keepdims=True)
    var = jnp.mean(xc * xc, axis=-1, keepdims=True)
    ki = xc * lax.rsqrt(var + EPS) * lng_ref[...] + lnb_ref[...]
    ki_ref[...] = _rope(ki, c, sa, sb).astype(BF16)
    wi = proj(o + LANES, o + 2 * LANES) * ((IDX_HEADS ** -0.5) * (IDX_DIM ** -0.5))
    wit_ref[0] = wi.T[:IDX_HEADS]


def _front1(h, g, w, c, sa, sb, lng, lnb, batch, seq):
    t, d = h.shape
    tm = ROW_TILE
    assert tm == DSA_K_CHUNK
    nt = seq // tm
    row = lambda n: pl.BlockSpec((tm, n), lambda i: (i, 0))
    tr = lambda n: pl.BlockSpec((1, n, tm), lambda i: (i // nt, 0, i % nt))
    return pl.pallas_call(
        _front1_kernel,
        grid=(t // tm,),
        in_specs=[row(d), _const_spec((1, d)), _const_spec(w.shape), row(LANES), row(LANES), row(LANES),
                  _const_spec(lng.shape), _const_spec(lnb.shape)],
        out_specs=[tr(WIDTH_C), row(WIDTH_C),
                   pl.BlockSpec((1, 1, WIDTH_C, tm), lambda i: (i // nt, i % nt, 0, 0)),
                   tr(IDX_HEADS * IDX_DIM), row(LANES), tr(IDX_HEADS)],
        out_shape=[jax.ShapeDtypeStruct((batch, WIDTH_C, seq), BF16),
                   jax.ShapeDtypeStruct((t, WIDTH_C), BF16),
                   jax.ShapeDtypeStruct((batch, nt, WIDTH_C, tm), BF16),
                   jax.ShapeDtypeStruct((batch, IDX_HEADS * IDX_DIM, seq), BF16),
                   jax.ShapeDtypeStruct((t, LANES), BF16),
                   jax.ShapeDtypeStruct((batch, IDX_HEADS, seq), F32)],
        compiler_params=_params(1),
        name="front1",
    )(h, g, w, c, sa, sb, lng, lnb)


def _dsa_index_kernel(qit_ref, wit_ref, ki_ref, b_ref, sc_ref, sc16_ref, *, n_sel):
    i = pl.program_id(1)
    tq = qit_ref.shape[2]
    kc_n = DSA_K_CHUNK
    n_chunks_total = sc_ref.shape[0]
    n_chunks = (i * tq + tq + kc_n - 1) // kc_n
    qpos = i * tq + lax.broadcasted_iota(jnp.int32, (1, tq), 1)
    kofs = lax.broadcasted_iota(jnp.int32, (kc_n, 1), 0)
    wv = wit_ref[0]

    def causal(c):
        return (c * kc_n + kofs) <= qpos

    big = 3e38

    def score_chunk(c, carry):
        mx, mn = carry
        kc = ki_ref[pl.ds(pl.multiple_of(c * kc_n, kc_n), kc_n), :]
        acc = jnp.zeros((kc_n, tq), F32)
        for h in range(IDX_HEADS):
            qm = _head_rows(qit_ref[0, (h // 2) * LANES:(h // 2 + 1) * LANES, :], h % 2)
            acc = acc + jnp.maximum(_dot(kc, qm), 0.0) * wv[h:h + 1, :]
        ok = causal(c)
        sc_ref[c] = jnp.where(ok, acc, NEG)
        sc16_ref[c] = jnp.where(ok, acc, NEG).astype(BF16)
        return (jnp.maximum(mx, _fold_rows(jnp.where(ok, acc, -big), jnp.maximum)),
                jnp.minimum(mn, _fold_rows(jnp.where(ok, acc, big), jnp.minimum)))

    mx, mn = lax.fori_loop(0, n_chunks, score_chunk,
                           (jnp.full((SUBLANES, tq), -big, F32), jnp.full((SUBLANES, tq), big, F32)))
    col_max = jnp.max(mx, axis=0, keepdims=True)
    col_min = jnp.min(mn, axis=0, keepdims=True)

    def reduce_chunks(fn, op, init):
        def body(c, acc):
            return op(acc, _fold_rows(fn(sc_ref[c]), op))
        return lax.fori_loop(0, n_chunks, body, jnp.full((SUBLANES, tq), init, F32))

    def count_ge(thr):
        acc = reduce_chunks(lambda x: jnp.where(x >= thr, 1.0, 0.0), jnp.add, 0.0)
        return jnp.sum(acc, axis=0, keepdims=True)

    def pending(flags):
        return jnp.max(jnp.where(flags, 0.0, 1.0))

    k_f = float(n_sel)
    n_causal = (qpos + 1).astype(F32)
    small = n_causal < k_f
    rows16 = 2 * SUBLANES

    def count_ge_bf16(thr):
        one, zero = jnp.ones((), BF16), jnp.zeros((), BF16)

        def body(c, acc):
            hits = jnp.where(sc16_ref[c] >= thr, one, zero)
            return acc + _fold_rows(hits, jnp.add, rows16).astype(F32)

        return jnp.sum(lax.fori_loop(0, n_chunks, body, jnp.zeros((rows16, tq), F32)), axis=0, keepdims=True)

    def coarse(_, carry):
        lo, hi = carry
        mid = (0.5 * (lo + hi)).astype(BF16)
        ok = count_ge_bf16(mid) >= k_f
        mid = mid.astype(F32)
        return jnp.where(ok, mid, lo), jnp.where(ok, hi, mid)

    top = col_max.astype(BF16).astype(F32)
    lo, hi = lax.fori_loop(0, COARSE_STEPS, coarse,
                           (col_min.astype(BF16).astype(F32),
                            (top + jnp.abs(top) * 2.0 * BF16_SPACING + TINY).astype(BF16).astype(F32)))
    lo = jnp.where(small, NEG, lo - jnp.abs(lo) * BF16_SPACING - TINY)
    c_lo = jnp.full((1, tq), 2.0 * k_f, F32)
    c_hi = jnp.full((1, tq), -1.0, F32)

    def bisect(_, carry):
        lo, hi, c_lo, c_hi = carry
        mid = 0.5 * (lo + hi)
        cnt = count_ge(mid)
        ok = cnt >= k_f
        return (jnp.where(ok, mid, lo), jnp.where(ok, hi, mid), jnp.where(ok, cnt, c_lo), jnp.where(ok, c_hi, cnt))

    def bracket_ends(lo, hi):
        def body(c, carry):
            below, above = carry
            x = sc_ref[c]
            return (jnp.maximum(below, _fold_rows(jnp.where(x < hi, x, -big), jnp.maximum)),
                    jnp.minimum(above, _fold_rows(jnp.where(x >= lo, x, big), jnp.minimum)))

        below, above = lax.fori_loop(0, n_chunks, body, (jnp.full((SUBLANES, tq), -big, F32),
                                                         jnp.full((SUBLANES, tq), big, F32)))
        return jnp.max(below, axis=0, keepdims=True), jnp.min(above, axis=0, keepdims=True)

    def settled(c_lo, c_hi, below, above):
        return small | (c_hi == k_f - 1.0) | (c_lo == k_f) | (below == above)

    def narrow_cond(carry):
        return (carry[6] < BISECT_ROUNDS) & (carry[7] > 0.5)

    def narrow(carry):
        state = lax.fori_loop(0, BISECT_PER_ROUND, bisect, carry[:4])
        below, above = bracket_ends(state[0], state[1])
        return (*state, below, above, carry[6] + 1, pending(settled(state[2], state[3], below, above)))

    zeros = jnp.zeros((1, tq), F32)
    lo, hi, c_lo, c_hi, below, above, _, _ = lax.while_loop(
        narrow_cond, narrow, (lo, hi, c_lo, c_hi, zeros, zeros, jnp.int32(0), jnp.float32(1.0)))
    thr = jnp.where(small, NEG, jnp.where(c_lo == k_f, above, below))
    done = jnp.where(settled(c_lo, c_hi, below, above), 1.0, 0.0)

    def snap_cond(carry):
        return carry[3] > 0.5

    def snap(carry):
        hi, thr, done, _ = carry
        cand = jnp.max(reduce_chunks(lambda x: jnp.where(x < hi, x, -big), jnp.maximum, -big),
                       axis=0, keepdims=True)
        ok = count_ge(cand) >= k_f
        thr = jnp.where(done > 0.5, thr, cand)
        done = jnp.where(ok, 1.0, done)
        hi = jnp.where(done > 0.5, hi, cand)
        return hi, thr, done, jnp.max(1.0 - done)

    _, thr, _, _ = lax.while_loop(snap_cond, snap, (hi, thr, done, jnp.max(1.0 - done)))

    n_ge = count_ge(thr)
    tied = jnp.max(jnp.where((n_ge > k_f) & jnp.logical_not(small), 1.0, 0.0))

    @pl.when(tied < 0.5)
    def _():
        def body(c, carry):
            x = sc_ref[c]
            b_ref[0, 0, c] = jnp.where((x >= thr) & causal(c), 0.0, NEG)
            return carry
        lax.fori_loop(0, n_chunks, body, 0)

    @pl.when(tied > 0.5)
    def _():
        n_gt = reduce_chunks(lambda x: jnp.where(x > thr, 1.0, 0.0), jnp.add, 0.0)
        need = k_f - jnp.sum(n_gt, axis=0, keepdims=True)

        def body(c, seen):
            x = sc_ref[c]
            eq = x == thr
            total = seen + jnp.sum(_fold_rows(jnp.where(eq, 1.0, 0.0), jnp.add), axis=0, keepdims=True)
            keep_all = total <= need
            crossing = jnp.max(jnp.where(keep_all | (seen >= need), 0.0, 1.0))

            @pl.when(crossing < 0.5)
            def _():
                b_ref[0, 0, c] = jnp.where(((x > thr) | (eq & keep_all)) & causal(c), 0.0, NEG)

            @pl.when(crossing > 0.5)
            def _():
                r = lax.broadcasted_iota(jnp.int32, (kc_n, kc_n), 0)
                cc = lax.broadcasted_iota(jnp.int32, (kc_n, kc_n), 1)
                prefix = jnp.where(cc <= r, 1.0, 0.0).astype(BF16)
                cnt = seen + _dot(prefix, jnp.where(eq, 1.0, 0.0).astype(BF16))
                b_ref[0, 0, c] = jnp.where(((x > thr) | (eq & (cnt <= need))) & causal(c), 0.0, NEG)

            return total
        lax.fori_loop(0, n_chunks, body, jnp.zeros((1, tq), F32))

    def fill(c, carry):
        b_ref[0, 0, c] = jnp.full((kc_n, tq), NEG, F32)
        return carry

    lax.fori_loop(n_chunks, n_chunks_total, fill, 0)


def _dsa_index(qit, wit, ki, batch, seq):
    tq = Q_TILE
    nq = seq // tq
    nch = seq // DSA_K_CHUNK
    n_sel = min(DSA_TOPK, seq // 4)
    return pl.pallas_call(
        functools.partial(_dsa_index_kernel, n_sel=n_sel),
        grid=(batch, nq),
        in_specs=[pl.BlockSpec((1, qit.shape[1], tq), lambda b, i: (b, 0, i)),
                  pl.BlockSpec((1, IDX_HEADS, tq), lambda b, i: (b, 0, i)),
                  pl.BlockSpec((seq, LANES), lambda b, i: (b, 0))],
        out_specs=pl.BlockSpec((1, 1, nch, DSA_K_CHUNK, tq), lambda b, i: (b, i, 0, 0, 0)),
        out_shape=jax.ShapeDtypeStruct((batch, nq, nch, DSA_K_CHUNK, tq), F32),
        scratch_shapes=[pltpu.VMEM((nch, DSA_K_CHUNK, tq), F32), pltpu.VMEM((nch, DSA_K_CHUNK, tq), BF16)],
        compiler_params=_params(2),
        name="dsa_index",
    )(qit, wit, ki)


def _dsa_attn_kernel(qt_ref, k_ref, vt_ref, b_ref, o_ref, qm_ref, s_ref, cm_ref, m_ref, acc_ref):
    i = pl.program_id(1)
    tq = qt_ref.shape[2]
    kc_n = DSA_K_CHUNK
    n_heads = acc_ref.shape[0]
    n_chunks = (i * tq + tq + kc_n - 1) // kc_n
    _softmax_init(m_ref, acc_ref)
    for h in range(n_heads):
        qm_ref[h] = _head_rows(qt_ref[0, (h // 2) * LANES:(h // 2 + 1) * LANES, :], h % 2)

    def chunk_scores(c):
        st = pl.multiple_of(c * kc_n, kc_n)

        def scores(h):
            return _dot(k_ref[pl.ds(st, kc_n), (h // 2) * LANES:(h // 2 + 1) * LANES], qm_ref[h]) + b_ref[0, 0, c]

        return scores

    _stage_scores(chunk_scores(0), range(s_ref.shape[0] - 1), s_ref, cm_ref)

    def body(c, carry):
        def values(h):
            return vt_ref[0, c, h * HEAD_DIM:(h + 1) * HEAD_DIM, :]

        _attend_chunk(n_heads, chunk_scores(c), chunk_scores(jnp.minimum(c + 1, n_chunks - 1)), values,
                      s_ref, cm_ref, m_ref, acc_ref)
        return carry

    lax.fori_loop(0, n_chunks, body, 0)
    _write_heads(o_ref, acc_ref)


def _dsa_attn(qt, k, vt, bias, batch, seq):
    t, width = k.shape
    tq = Q_TILE
    nq = seq // tq
    n_heads = width // HEAD_DIM
    one = pl.Buffered(1)
    return pl.pallas_call(
        _dsa_attn_kernel,
        grid=(batch, nq),
        in_specs=[pl.BlockSpec((1, width, tq), lambda b, i: (b, 0, i)),
                  pl.BlockSpec((seq, width), lambda b, i: (b, 0), pipeline_mode=one),
                  pl.BlockSpec((1,) + vt.shape[1:], lambda b, i: (b, 0, 0, 0), pipeline_mode=one),
                  pl.BlockSpec((1, 1) + bias.shape[2:], lambda b, i: (b, i, 0, 0, 0))],
        out_specs=pl.BlockSpec((tq, width), lambda b, i: (b * nq + i, 0)),
        out_shape=jax.ShapeDtypeStruct((t, width), BF16),
        scratch_shapes=[pltpu.VMEM((n_heads, LANES, tq), BF16), pltpu.VMEM((SCORE_SLOTS, DSA_K_CHUNK, tq), F32),
                        pltpu.VMEM((SCORE_SLOTS, SUBLANES, tq), F32),
                        pltpu.VMEM((n_heads, 1, tq), F32), pltpu.VMEM((n_heads, HEAD_DIM + ONES_ROWS, tq), F32)],
        compiler_params=_params(2),
        name="dsa_attn",
    )(qt, k, vt, bias)


def _rope_tables(positions):
    inv_freq = ROPE_THETA ** (-jnp.arange(0, ROPE_DIM, 2, dtype=F32) / ROPE_DIM)
    m = jnp.arange(LANES) % HEAD_DIM
    ang = positions.astype(F32).reshape(-1, 1) * inv_freq[m % ROPE_HALF][None, :]
    cos, sin = jnp.cos(ang), jnp.sin(ang)
    c = jnp.where(m < ROPE_DIM, cos, 1.0)
    sa = jnp.where(m < ROPE_HALF, -sin, 0.0)
    sb = jnp.where((m >= ROPE_HALF) & (m < ROPE_DIM), sin, 0.0)
    return c, sa, sb


def kernel(x, p, positions, g_mix, w_in_even, a_ln_g, a_ln_b, a_w_s, a_b_s, w_out_even, w_in_odd, c_kidx_ln_g, c_kidx_ln_b, w_out_odd, g_ffn, w_ffn_gate, w_ffn_up, w_ffn_down, g_ple, w_ple_proj, w_ple_gate, g_final):
    batch, seq, d = x.shape
    depth = p.shape[0]
    t = batch * seq
    assert seq % ROW_TILE == 0 and seq % DSA_K_CHUNK == 0 and min(DSA_TOPK, seq // 4) <= DSA_K_CHUNK
    c, sa, sb = _rope_tables(positions)
    h = x.reshape(t, d)
    p2 = p.reshape(depth, t, p.shape[-1])
    row = lambda a: a.reshape(1, -1)
    onehot = (jnp.arange(seq)[:, None] // MOBA_BLOCK == jnp.arange(LANES)[None, :]).astype(BF16)
    bf = lambda w: w.astype(BF16)
    wo_even, wo_odd, wg, wu, wd, wpg, wpp = map(bf, (w_out_even, w_out_odd, w_ffn_gate, w_ffn_up, w_ffn_down,
                                                     w_ple_gate, w_ple_proj))
    for i in range(depth):
        j = i // 2
        if i % 2 == 0:
            bs = jnp.repeat(a_b_s[j].T, HEAD_DIM, axis=1)
            ya, qt, k, vt, km = _front0(h, row(g_mix[i]), bf(w_in_even[j]), c, sa, sb,
                                        row(a_ln_g[j]), row(a_ln_b[j]), a_w_s[j], bs, batch, seq)
            ys = [ya, _moba(qt, k, vt, km.reshape(t // MOBA_BLOCK, WIDTH_B), onehot, batch, seq)]
            wo = wo_even
        else:
            w = w_in_odd[j]
            o = 3 * WIDTH_C + IDX_HEADS * IDX_DIM
            w1 = bf(jnp.concatenate([w[:, :o], w[:, o:o + IDX_DIM], w[:, o:o + IDX_DIM], w[:, o + IDX_DIM:],
                                     jnp.zeros((d, LANES - IDX_HEADS), w.dtype)], axis=1))
            qt, k, vt, qit, ki, wit = _front1(h, row(g_mix[i]), w1, c, sa, sb, row(jnp.tile(c_kidx_ln_g[j], 2)),
                                              row(jnp.tile(c_kidx_ln_b[j], 2)), batch, seq)
            ys = [_dsa_attn(qt, k, vt, _dsa_index(qit, wit, ki, batch, seq), batch, seq)]
            wo = wo_odd
        h = _post(h, ys, wo, j, p2, i, row(g_ffn[i]), wg, wu, wd, row(g_ple[i]), wpg, wpp, row(g_final),
                  final=(i == depth - 1))
    return h.reshape(batch, seq, d)
```

```python
import functools

import jax
import jax.numpy as jnp
from jax import lax
from jax.experimental import pallas as pl
from jax.experimental.pallas import tpu as pltpu

HEAD_DIM = 64
ROPE_DIM = HEAD_DIM // 4
ROPE_HALF = ROPE_DIM // 2
ROPE_THETA = 500000.0
N_GROUPS_A = 8
WIDTH_A = N_GROUPS_A * HEAD_DIM
N_HEADS_B = 8
WIDTH_B = N_HEADS_B * HEAD_DIM
N_HEADS_C = 16
WIDTH_C = N_HEADS_C * HEAD_DIM
SGU_CHUNK = 128
MOBA_BLOCK = 256
MOBA_TOPK = 3
IDX_HEADS = 8
IDX_DIM = 64
DSA_TOPK = 256
EPS = 1e-6
NEG = -1e30
LOG2E = 1.4426950408889634
BF16_SPACING = 2.0 ** -7
TINY = 1e-30

LANES = 128
SUBLANES = 8
ROW_TILE = 512
Q_TILE = 256
DSA_K_CHUNK = 512
FOLD_CHAINS = 8
ONES_ROWS = 16
SCORE_SLOTS = 4
MOBA_SLOTS = 4
MOBA_K_CHUNK = 2 * MOBA_BLOCK
COARSE_STEPS = 11
BISECT_PER_ROUND = 7
BISECT_ROUNDS = 12
VMEM_LIMIT = 56 * 1024 * 1024

F32 = jnp.float32
BF16 = jnp.bfloat16


def _dot(a, b):
    return jnp.dot(a, b, preferred_element_type=F32)


def _rms(x, g):
    return x * lax.rsqrt(jnp.mean(x * x, axis=-1, keepdims=True) + EPS) * g


def _rope(z, c, sa, sb):
    outs = []
    for j in range(z.shape[1] // LANES):
        zz = z[:, j * LANES:(j + 1) * LANES]
        outs.append(zz * c + pltpu.roll(zz, LANES - ROPE_HALF, 1) * sa + pltpu.roll(zz, ROPE_HALF, 1) * sb)
    return outs[0] if len(outs) == 1 else jnp.concatenate(outs, axis=1)


def _gelu(x):
    return 0.5 * x * (1.0 + lax.erf(x * (2.0 ** -0.5)))


def _fold_rows(x, op, group=SUBLANES):
    parts = [x[a:a + group] for a in range(0, x.shape[0], group)]
    chains = parts[:FOLD_CHAINS]
    for a, part in enumerate(parts[FOLD_CHAINS:]):
        chains[a % FOLD_CHAINS] = op(chains[a % FOLD_CHAINS], part)
    while len(chains) > 1:
        chains = [op(chains[a], chains[a + 1]) if a + 1 < len(chains) else chains[a] for a in range(0, len(chains), 2)]
    return chains[0]


def _head_rows(x, hh):
    r = lax.broadcasted_iota(jnp.int32, (LANES, 1), 0)
    return jnp.where((r >= hh * HEAD_DIM) & (r < (hh + 1) * HEAD_DIM), x, jnp.zeros_like(x))


def _const_spec(shape):
    nd = len(shape)
    return pl.BlockSpec(shape, lambda *_: (0,) * nd, pipeline_mode=pl.Buffered(1))


def _params(n_axes):
    return pltpu.CompilerParams(dimension_semantics=("arbitrary",) * n_axes, vmem_limit_bytes=VMEM_LIMIT)


def _stage_scores(score_fn, heads, s_ref, cm_ref):
    n_slots = s_ref.shape[0]
    for h in heads:
        s = score_fn(h)
        s_ref[h % n_slots] = s
        cm_ref[h % n_slots] = _fold_rows(s, jnp.maximum)


def _attend_chunk(n_heads, score_fn, next_score_fn, vt_fn, s_ref, cm_ref, m_ref, acc_ref):
    n_slots = s_ref.shape[0]
    ahead = n_slots - 1
    assert n_heads % n_slots == 0
    for h in range(n_heads):
        if h + ahead < n_heads:
            _stage_scores(score_fn, [h + ahead], s_ref, cm_ref)
        else:
            _stage_scores(next_score_fn, [h + ahead - n_heads], s_ref, cm_ref)
        slot = h % n_slots
        m_old = m_ref[h]
        m_new = jnp.maximum(m_old, jnp.max(cm_ref[slot], axis=0, keepdims=True))
        p = jnp.exp2(s_ref[slot] - m_new).astype(BF16)
        m_ref[h] = m_new
        vt = vt_fn(h)
        vt = jnp.concatenate([vt, jnp.ones((acc_ref.shape[1] - vt.shape[0], vt.shape[1]), BF16)], axis=0)
        acc_ref[h] = jnp.exp2(m_old - m_new) * acc_ref[h] + _dot(vt, p)


def _softmax_init(m_ref, acc_ref):
    m_ref[...] = jnp.full(m_ref.shape, -jnp.inf, F32)
    acc_ref[...] = jnp.zeros(acc_ref.shape, F32)


def _write_heads(o_ref, acc_ref):
    def head(h):
        return acc_ref[h, :HEAD_DIM, :] / acc_ref[h, HEAD_DIM:HEAD_DIM + 1, :]

    for p in range(acc_ref.shape[0] // 2):
        o_ref[:, p * LANES:(p + 1) * LANES] = jnp.concatenate([head(2 * p), head(2 * p + 1)], axis=0).T.astype(BF16)


def _front0_kernel(h_ref, g_ref, w_ref, c_ref, sa_ref, sb_ref, lng_ref, lnb_ref, ws_ref, bs_ref,
                   ya_ref, qt_ref, k_ref, vt_ref, km_ref):
    tm = h_ref.shape[0]
    xn = _rms(h_ref[...], g_ref[...]).astype(BF16)

    def proj(lo, hi):
        return _dot(xn, w_ref[:, lo:hi])

    u = _gelu(proj(0, WIDTH_A))
    v = _gelu(proj(WIDTH_A, 2 * WIDTH_A))
    lane = lax.broadcasted_iota(jnp.int32, (1, LANES), 1)
    low = lane < HEAD_DIM
    row = lax.broadcasted_iota(jnp.int32, (SGU_CHUNK, SGU_CHUNK), 0)
    col = lax.broadcasted_iota(jnp.int32, (SGU_CHUNK, SGU_CHUNK), 1)
    tril = col <= row
    for j in range(WIDTH_A // LANES):
        sl = slice(j * LANES, (j + 1) * LANES)
        vv = v[:, sl]

        def half_mean(t):
            s_lo = jnp.sum(jnp.where(low, t, 0.0), axis=-1, keepdims=True)
            s_hi = jnp.sum(jnp.where(low, 0.0, t), axis=-1, keepdims=True)
            return jnp.where(low, s_lo, s_hi) * (1.0 / HEAD_DIM)

        xc = vv - half_mean(vv)
        var = half_mean(xc * xc)
        vn = (xc * lax.rsqrt(var + EPS) * lng_ref[:, sl] + lnb_ref[:, sl]).astype(BF16)
        w_lo = jnp.where(tril, ws_ref[2 * j], 0.0).astype(BF16)
        w_hi = jnp.where(tril, ws_ref[2 * j + 1], 0.0).astype(BF16)
        for c in range(tm // SGU_CHUNK):
            rs = slice(c * SGU_CHUNK, (c + 1) * SGU_CHUNK)
            vc = vn[rs]
            mixed = jnp.where(low, _dot(w_lo, vc), _dot(w_hi, vc)) + bs_ref[:, sl]
            ya_ref[rs, sl] = (u[rs, sl] * mixed).astype(BF16)

    o = 2 * WIDTH_A
    c, sa, sb = c_ref[...], sa_ref[...], sb_ref[...]
    q = _rope(proj(o, o + WIDTH_B), c, sa, sb) * (HEAD_DIM ** -0.5 * LOG2E)
    qt_ref[0] = q.T.astype(BF16)
    k = _rope(proj(o + WIDTH_B, o + 2 * WIDTH_B), c, sa, sb)
    k_ref[...] = k.astype(BF16)
    v_b = proj(o + 2 * WIDTH_B, o + 3 * WIDTH_B)
    for blk in range(tm // MOBA_BLOCK):
        rs = slice(blk * MOBA_BLOCK, (blk + 1) * MOBA_BLOCK)
        km_ref[0, blk:blk + 1, :] = jnp.mean(k[rs], axis=0, keepdims=True)
        vt_ref[0, blk] = v_b[rs].T.astype(BF16)


def _front0(h, g, w, c, sa, sb, lng, lnb, ws, bs, batch, seq):
    t, d = h.shape
    tm = ROW_TILE
    nt = seq // tm
    nblk = tm // MOBA_BLOCK
    row = lambda n: pl.BlockSpec((tm, n), lambda i: (i, 0))
    return pl.pallas_call(
        _front0_kernel,
        grid=(t // tm,),
        in_specs=[row(d), _const_spec((1, d)), _const_spec(w.shape), row(LANES), row(LANES), row(LANES),
                  _const_spec(lng.shape), _const_spec(lnb.shape), _const_spec(ws.shape), _const_spec(bs.shape)],
        out_specs=[row(WIDTH_A),
                   pl.BlockSpec((1, WIDTH_B, tm), lambda i: (i // nt, 0, i % nt)),
                   row(WIDTH_B),
                   pl.BlockSpec((1, nblk, WIDTH_B, MOBA_BLOCK), lambda i: (i // nt, i % nt, 0, 0)),
                   pl.BlockSpec((1, nblk, WIDTH_B), lambda i: (i, 0, 0))],
        out_shape=[jax.ShapeDtypeStruct((t, WIDTH_A), BF16),
                   jax.ShapeDtypeStruct((batch, WIDTH_B, seq), BF16),
                   jax.ShapeDtypeStruct((t, WIDTH_B), BF16),
                   jax.ShapeDtypeStruct((batch, seq // MOBA_BLOCK, WIDTH_B, MOBA_BLOCK), BF16),
                   jax.ShapeDtypeStruct((t // tm, nblk, WIDTH_B), F32)],
        compiler_params=_params(1),
        name="front0",
    )(h, g, w, c, sa, sb, lng, lnb, ws, bs)


def _moba_kernel(qt_ref, k_ref, vt_ref, km_ref, oh_ref, o_ref, qc_ref, s_ref, cm_ref, m_ref, acc_ref):
    i = pl.program_id(1)
    tq = qt_ref.shape[2]
    nb = km_ref.shape[0]
    n_heads = acc_ref.shape[0]
    kc_n = s_ref.shape[1]
    per_chunk = kc_n // MOBA_BLOCK
    blk = lax.broadcasted_iota(jnp.int32, (nb, 1), 0)
    blk_f = blk.astype(F32)
    q_blk = (i * tq + lax.broadcasted_iota(jnp.int32, (1, tq), 1)) // MOBA_BLOCK
    past = blk < q_blk
    _softmax_init(m_ref, acc_ref)
    for h in range(n_heads):
        sl = slice((h // 2) * LANES, (h // 2 + 1) * LANES)
        qm = _head_rows(qt_ref[0, sl, :], h % 2)
        g = jnp.where(past, _dot(km_ref[:, sl].astype(BF16), qm), NEG)
        sel = jnp.zeros(g.shape, jnp.bool_)
        for _ in range(min(MOBA_TOPK, nb)):
            top = jnp.max(g, axis=0, keepdims=True)
            idx = jnp.min(jnp.where(g == top, blk_f, 1e9), axis=0, keepdims=True)
            pick = blk_f == idx
            sel = sel | pick
            g = jnp.where(pick, -jnp.inf, g)
        selb = jnp.where((sel & past) | (blk == q_blk), 0.0, NEG).astype(BF16)
        qc_ref[h] = jnp.concatenate([qm, selb, jnp.zeros((LANES - nb, tq), BF16)], axis=0)

    def values(c):
        def vt(h):
            rows = slice(h * HEAD_DIM, (h + 1) * HEAD_DIM)
            return jnp.concatenate([vt_ref[0, per_chunk * c + a, rows, :] for a in range(per_chunk)], axis=1)
        return vt

    def chunk_scores(c, mask=None):
        st = pl.multiple_of(c * kc_n, kc_n)

        def scores(h):
            kc = jnp.concatenate([k_ref[pl.ds(st, kc_n), (h // 2) * LANES:(h // 2 + 1) * LANES],
                                  oh_ref[pl.ds(st, kc_n), :]], axis=1)
            s = _dot(kc, qc_ref[h])
            return s if mask is None else jnp.where(mask, s, NEG)

        return scores

    own = (i * tq) // kc_n
    kpos = own * kc_n + lax.broadcasted_iota(jnp.int32, (kc_n, 1), 0)
    qpos = i * tq + lax.broadcasted_iota(jnp.int32, (1, tq), 1)
    own_scores = chunk_scores(own, kpos <= qpos)
    refs = (s_ref, cm_ref, m_ref, acc_ref)
    last = jnp.maximum(own - 1, 0)
    _stage_scores(own_scores, range(s_ref.shape[0] - 1), s_ref, cm_ref)
    _attend_chunk(n_heads, own_scores, chunk_scores(0), values(own), *refs)

    def body(c, carry):
        _attend_chunk(n_heads, chunk_scores(c), chunk_scores(jnp.minimum(c + 1, last)), values(c), *refs)
        return carry

    lax.fori_loop(0, own, body, 0)
    _write_heads(o_ref, acc_ref)


def _moba(qt, k, vt, km, onehot, batch, seq):
    t, width = k.shape
    tq = MOBA_K_CHUNK
    nb = seq // MOBA_BLOCK
    nq = seq // tq
    n_heads = width // HEAD_DIM
    one = pl.Buffered(1)
    return pl.pallas_call(
        _moba_kernel,
        grid=(batch, nq),
        in_specs=[pl.BlockSpec((1, width, tq), lambda b, i: (b, 0, i)),
                  pl.BlockSpec((seq, width), lambda b, i: (b, 0), pipeline_mode=one),
                  pl.BlockSpec((1, nb, width, MOBA_BLOCK), lambda b, i: (b, 0, 0, 0), pipeline_mode=one),
                  pl.BlockSpec((nb, width), lambda b, i: (b, 0)),
                  _const_spec(onehot.shape)],
        out_specs=pl.BlockSpec((tq, width), lambda b, i: (b * nq + i, 0)),
        out_shape=jax.ShapeDtypeStruct((t, width), BF16),
        scratch_shapes=[pltpu.VMEM((n_heads, 2 * LANES, tq), BF16), pltpu.VMEM((MOBA_SLOTS, MOBA_K_CHUNK, tq), F32),
                        pltpu.VMEM((MOBA_SLOTS, SUBLANES, tq), F32),
                        pltpu.VMEM((n_heads, 1, tq), F32), pltpu.VMEM((n_heads, HEAD_DIM + ONES_ROWS, tq), F32)],
        compiler_params=_params(2),
        name="moba",
    )(qt, k, vt, km, onehot)


def _post_kernel(*refs, n_mix, final):
    h_ref = refs[0]
    y_refs = refs[1:1 + n_mix]
    (wo_ref, p_ref, gf_ref, wg_ref, wu_ref, wd_ref, gp_ref, wpg_ref, wpp_ref, gl_ref, o_ref, a_ref) = refs[1 + n_mix:]
    h = h_ref[...]
    lo = 0
    for y_ref in y_refs:
        h = h + _dot(y_ref[...], wo_ref[0, lo:lo + y_ref.shape[1], :])
        lo += y_ref.shape[1]
    xn = _rms(h, gf_ref[...]).astype(BF16)
    d_ff = wg_ref.shape[2]
    step = 2 * LANES
    for lo in range(0, d_ff, step):
        gate = _dot(xn, wg_ref[0, :, lo:lo + step])
        up = _dot(xn, wu_ref[0, :, lo:lo + step])
        a_ref[:, lo:lo + step] = (gate * jax.nn.sigmoid(gate) * up).astype(BF16)
    h = h + _dot(a_ref[...], wd_ref[0])
    pg = jax.nn.sigmoid(_dot(_rms(h, gp_ref[...]).astype(BF16), wpg_ref[0]))
    h = h + _dot(p_ref[0].astype(BF16), wpp_ref[0]) * pg
    o_ref[...] = _rms(h, gl_ref[...]) if final else h


def _layer_spec(shape, layer):
    rest = (0,) * (len(shape) - 1)
    return pl.BlockSpec((1,) + tuple(shape[1:]), lambda *_: (layer,) + rest, pipeline_mode=pl.Buffered(1))


def _post(h, ys, wo, wo_layer, p, layer, gf, wg, wu, wd, gp, wpg, wpp, gl, final):
    t, d = h.shape
    tm = ROW_TILE
    row = lambda n: pl.BlockSpec((tm, n), lambda i: (i, 0))
    stack = lambda w: _layer_spec(w.shape, layer)
    return pl.pallas_call(
        functools.partial(_post_kernel, n_mix=len(ys), final=final),
        grid=(t // tm,),
        in_specs=[row(d)] + [row(y.shape[1]) for y in ys]
                 + [_layer_spec(wo.shape, wo_layer), pl.BlockSpec((1, tm, p.shape[2]), lambda i: (layer, i, 0)),
                    _const_spec(gf.shape), stack(wg), stack(wu), stack(wd), _const_spec(gp.shape), stack(wpg),
                    stack(wpp), _const_spec(gl.shape)],
        out_specs=row(d),
        out_shape=jax.ShapeDtypeStruct((t, d), F32),
        scratch_shapes=[pltpu.VMEM((tm, wg.shape[2]), BF16)],
        compiler_params=_params(1),
        name="post_final" if final else "post",
    )(h, *ys, wo, p, gf, wg, wu, wd, gp, wpg, wpp, gl)


def _front1_kernel(h_ref, g_ref, w_ref, c_ref, sa_ref, sb_ref, lng_ref, lnb_ref,
                   qt_ref, k_ref, vt_ref, qit_ref, ki_ref, wit_ref):
    xn = _rms(h_ref[...], g_ref[...]).astype(BF16)

    def proj(lo, hi):
        return _dot(xn, w_ref[:, lo:hi])

    c, sa, sb = c_ref[...], sa_ref[...], sb_ref[...]
    qt_ref[0] = (_rope(proj(0, WIDTH_C), c, sa, sb) * (HEAD_DIM ** -0.5 * LOG2E)).T.astype(BF16)
    k_ref[...] = _rope(proj(WIDTH_C, 2 * WIDTH_C), c, sa, sb).astype(BF16)
    vt_ref[0, 0] = proj(2 * WIDTH_C, 3 * WIDTH_C).T.astype(BF16)
    o = 3 * WIDTH_C
    qit_ref[0] = _rope(proj(o, o + IDX_HEADS * IDX_DIM), c, sa, sb).T.astype(BF16)
    o += IDX_HEADS * IDX_DIM
    ki = proj(o, o + LANES)
    xc = ki - jnp.mean(ki, axis=-1, keepdims=True)
    var = jnp.mean(xc * xc, axis=-1, keepdims=True)
    ki = xc * lax.rsqrt(var + EPS) * lng_ref[...] + lnb_ref[...]
    ki_ref[...] = _rope(ki, c, sa, sb).astype(BF16)
    wi = proj(o + LANES, o + 2 * LANES) * ((IDX_HEADS ** -0.5) * (IDX_DIM ** -0.5))
    wit_ref[0] = wi.T[:IDX_HEADS]


def _front1(h, g, w, c, sa, sb, lng, lnb, batch, seq):
    t, d = h.shape
    tm = ROW_TILE
    assert tm == DSA_K_CHUNK
    nt = seq // tm
    row = lambda n: pl.BlockSpec((tm, n), lambda i: (i, 0))
    tr = lambda n: pl.BlockSpec((1, n, tm), lambda i: (i // nt, 0, i % nt))
    return pl.pallas_call(
        _front1_kernel,
        grid=(t // tm,),
        in_specs=[row(d), _const_spec((1, d)), _const_spec(w.shape), row(LANES), row(LANES), row(LANES),
                  _const_spec(lng.shape), _const_spec(lnb.shape)],
        out_specs=[tr(WIDTH_C), row(WIDTH_C),
                   pl.BlockSpec((1, 1, WIDTH_C, tm), lambda i: (i // nt, i % nt, 0, 0)),
                   tr(IDX_HEADS * IDX_DIM), row(LANES), tr(IDX_HEADS)],
        out_shape=[jax.ShapeDtypeStruct((batch, WIDTH_C, seq), BF16),
                   jax.ShapeDtypeStruct((t, WIDTH_C), BF16),
                   jax.ShapeDtypeStruct((batch, nt, WIDTH_C, tm), BF16),
                   jax.ShapeDtypeStruct((batch, IDX_HEADS * IDX_DIM, seq), BF16),
                   jax.ShapeDtypeStruct((t, LANES), BF16),
                   jax.ShapeDtypeStruct((batch, IDX_HEADS, seq), F32)],
        compiler_params=_params(1),
        name="front1",
    )(h, g, w, c, sa, sb, lng, lnb)


def _dsa_index_kernel(qit_ref, wit_ref, ki_ref, b_ref, sc_ref, sc16_ref, *, n_sel):
    i = pl.program_id(1)
    tq = qit_ref.shape[2]
    kc_n = DSA_K_CHUNK
    n_chunks_total = sc_ref.shape[0]
    n_chunks = (i * tq + tq + kc_n - 1) // kc_n
    qpos = i * tq + lax.broadcasted_iota(jnp.int32, (1, tq), 1)
    kofs = lax.broadcasted_iota(jnp.int32, (kc_n, 1), 0)
    wv = wit_ref[0]

    def causal(c):
        return (c * kc_n + kofs) <= qpos

    big = 3e38

    def score_chunk(c, carry):
        mx, mn = carry
        kc = ki_ref[pl.ds(pl.multiple_of(c * kc_n, kc_n), kc_n), :]
        acc = jnp.zeros((kc_n, tq), F32)
        for h in range(IDX_HEADS):
            qm = _head_rows(qit_ref[0, (h // 2) * LANES:(h // 2 + 1) * LANES, :], h % 2)
            acc = acc + jnp.maximum(_dot(kc, qm), 0.0) * wv[h:h + 1, :]
        ok = causal(c)
        sc_ref[c] = jnp.where(ok, acc, NEG)
        sc16_ref[c] = jnp.where(ok, acc, NEG).astype(BF16)
        return (jnp.maximum(mx, _fold_rows(jnp.where(ok, acc, -big), jnp.maximum)),
                jnp.minimum(mn, _fold_rows(jnp.where(ok, acc, big), jnp.minimum)))

    mx, mn = lax.fori_loop(0, n_chunks, score_chunk,
                           (jnp.full((SUBLANES, tq), -big, F32), jnp.full((SUBLANES, tq), big, F32)))
    col_max = jnp.max(mx, axis=0, keepdims=True)
    col_min = jnp.min(mn, axis=0, keepdims=True)

    def reduce_chunks(fn, op, init):
        def body(c, acc):
            return op(acc, _fold_rows(fn(sc_ref[c]), op))
        return lax.fori_loop(0, n_chunks, body, jnp.full((SUBLANES, tq), init, F32))

    def count_ge(thr):
        acc = reduce_chunks(lambda x: jnp.where(x >= thr, 1.0, 0.0), jnp.add, 0.0)
        return jnp.sum(acc, axis=0, keepdims=True)

    def pending(flags):
        return jnp.max(jnp.where(flags, 0.0, 1.0))

    k_f = float(n_sel)
    n_causal = (qpos + 1).astype(F32)
    small = n_causal < k_f
    rows16 = 2 * SUBLANES

    def count_ge_bf16(thr):
        one, zero = jnp.ones((), BF16), jnp.zeros((), BF16)

        def body(c, acc):
            hits = jnp.where(sc16_ref[c] >= thr, one, zero)
            return acc + _fold_rows(hits, jnp.add, rows16).astype(F32)

        return jnp.sum(lax.fori_loop(0, n_chunks, body, jnp.zeros((rows16, tq), F32)), axis=0, keepdims=True)

    def coarse(_, carry):
        lo, hi = carry
        mid = (0.5 * (lo + hi)).astype(BF16)
        ok = count_ge_bf16(mid) >= k_f
        mid = mid.astype(F32)
        return jnp.where(ok, mid, lo), jnp.where(ok, hi, mid)

    top = col_max.astype(BF16).astype(F32)
    lo, hi = lax.fori_loop(0, COARSE_STEPS, coarse,
                           (col_min.astype(BF16).astype(F32),
                            (top + jnp.abs(top) * 2.0 * BF16_SPACING + TINY).astype(BF16).astype(F32)))
    lo = jnp.where(small, NEG, lo - jnp.abs(lo) * BF16_SPACING - TINY)
    c_lo = jnp.full((1, tq), 2.0 * k_f, F32)
    c_hi = jnp.full((1, tq), -1.0, F32)

    def bisect(_, carry):
        lo, hi, c_lo, c_hi = carry
        mid = 0.5 * (lo + hi)
        cnt = count_ge(mid)
        ok = cnt >= k_f
        return (jnp.where(ok, mid, lo), jnp.where(ok, hi, mid), jnp.where(ok, cnt, c_lo), jnp.where(ok, c_hi, cnt))

    def bracket_ends(lo, hi):
        def body(c, carry):
            below, above = carry
            x = sc_ref[c]
            return (jnp.maximum(below, _fold_rows(jnp.where(x < hi, x, -big), jnp.maximum)),
                    jnp.minimum(above, _fold_rows(jnp.where(x >= lo, x, big), jnp.minimum)))

        below, above = lax.fori_loop(0, n_chunks, body, (jnp.full((SUBLANES, tq), -big, F32),
                                                         jnp.full((SUBLANES, tq), big, F32)))
        return jnp.max(below, axis=0, keepdims=True), jnp.min(above, axis=0, keepdims=True)

    def settled(c_lo, c_hi, below, above):
        return small | (c_hi == k_f - 1.0) | (c_lo == k_f) | (below == above)

    def narrow_cond(carry):
        return (carry[6] < BISECT_ROUNDS) & (carry[7] > 0.5)

    def narrow(carry):
        state = lax.fori_loop(0, BISECT_PER_ROUND, bisect, carry[:4])
        below, above = bracket_ends(state[0], state[1])
        return (*state, below, above, carry[6] + 1, pending(settled(state[2], state[3], below, above)))

    zeros = jnp.zeros((1, tq), F32)
    lo, hi, c_lo, c_hi, below, above, _, _ = lax.while_loop(
        narrow_cond, narrow, (lo, hi, c_lo, c_hi, zeros, zeros, jnp.int32(0), jnp.float32(1.0)))
    thr = jnp.where(small, NEG, jnp.where(c_lo == k_f, above, below))
    done = jnp.where(settled(c_lo, c_hi, below, above), 1.0, 0.0)

    def snap_cond(carry):
        return carry[3] > 0.5

    def snap(carry):
        hi, thr, done, _ = carry
        cand = jnp.max(reduce_chunks(lambda x: jnp.where(x < hi, x, -big), jnp.maximum, -big),
                       axis=0, keepdims=True)
        ok = count_ge(cand) >= k_f
        thr = jnp.where(done > 0.5, thr, cand)
        done = jnp.where(ok, 1.0, done)
        hi = jnp.where(done > 0.5, hi, cand)
        return hi, thr, done, jnp.max(1.0 - done)

    _, thr, _, _ = lax.while_loop(snap_cond, snap, (hi, thr, done, jnp.max(1.0 - done)))

    n_ge = count_ge(thr)
    tied = jnp.max(jnp.where((n_ge > k_f) & jnp.logical_not(small), 1.0, 0.0))

    @pl.when(tied < 0.5)
    def _():
        def body(c, carry):
            x = sc_ref[c]
            b_ref[0, 0, c] = jnp.where((x >= thr) & causal(c), 0.0, NEG)
            return carry
        lax.fori_loop(0, n_chunks, body, 0)

    @pl.when(tied > 0.5)
    def _():
        n_gt = reduce_chunks(lambda x: jnp.where(x > thr, 1.0, 0.0), jnp.add, 0.0)
        need = k_f - jnp.sum(n_gt, axis=0, keepdims=True)

        def body(c, seen):
            x = sc_ref[c]
            eq = x == thr
            total = seen + jnp.sum(_fold_rows(jnp.where(eq, 1.0, 0.0), jnp.add), axis=0, keepdims=True)
            keep_all = total <= need
            crossing = jnp.max(jnp.where(keep_all | (seen >= need), 0.0, 1.0))

            @pl.when(crossing < 0.5)
            def _():
                b_ref[0, 0, c] = jnp.where(((x > thr) | (eq & keep_all)) & causal(c), 0.0, NEG)

            @pl.when(crossing > 0.5)
            def _():
                r = lax.broadcasted_iota(jnp.int32, (kc_n, kc_n), 0)
                cc = lax.broadcasted_iota(jnp.int32, (kc_n, kc_n), 1)
                prefix = jnp.where(cc <= r, 1.0, 0.0).astype(BF16)
                cnt = seen + _dot(prefix, jnp.where(eq, 1.0, 0.0).astype(BF16))
                b_ref[0, 0, c] = jnp.where(((x > thr) | (eq & (cnt <= need))) & causal(c), 0.0, NEG)

            return total
        lax.fori_loop(0, n_chunks, body, jnp.zeros((1, tq), F32))

    def fill(c, carry):
        b_ref[0, 0, c] = jnp.full((kc_n, tq), NEG, F32)
        return carry

    lax.fori_loop(n_chunks, n_chunks_total, fill, 0)


def _dsa_index(qit, wit, ki, batch, seq):
    tq = Q_TILE
    nq = seq // tq
    nch = seq // DSA_K_CHUNK
    n_sel = min(DSA_TOPK, seq // 4)
    return pl.pallas_call(
        functools.partial(_dsa_index_kernel, n_sel=n_sel),
        grid=(batch, nq),
        in_specs=[pl.BlockSpec((1, qit.shape[1], tq), lambda b, i: (b, 0, i)),
                  pl.BlockSpec((1, IDX_HEADS, tq), lambda b, i: (b, 0, i)),
                  pl.BlockSpec((seq, LANES), lambda b, i: (b, 0))],
        out_specs=pl.BlockSpec((1, 1, nch, DSA_K_CHUNK, tq), lambda b, i: (b, i, 0, 0, 0)),
        out_shape=jax.ShapeDtypeStruct((batch, nq, nch, DSA_K_CHUNK, tq), F32),
        scratch_shapes=[pltpu.VMEM((nch, DSA_K_CHUNK, tq), F32), pltpu.VMEM((nch, DSA_K_CHUNK, tq), BF16)],
        compiler_params=_params(2),
        name="dsa_index",
    )(qit, wit, ki)


def _dsa_attn_kernel(qt_ref, k_ref, vt_ref, b_ref, o_ref, qm_ref, s_ref, cm_ref, m_ref, acc_ref):
    i = pl.program_id(1)
    tq = qt_ref.shape[2]
    kc_n = DSA_K_CHUNK
    n_heads = acc_ref.shape[0]
    n_chunks = (i * tq + tq + kc_n - 1) // kc_n
    _softmax_init(m_ref, acc_ref)
    for h in range(n_heads):
        qm_ref[h] = _head_rows(qt_ref[0, (h // 2) * LANES:(h // 2 + 1) * LANES, :], h % 2)

    def chunk_scores(c):
        st = pl.multiple_of(c * kc_n, kc_n)

        def scores(h):
            return _dot(k_ref[pl.ds(st, kc_n), (h // 2) * LANES:(h // 2 + 1) * LANES], qm_ref[h]) + b_ref[0, 0, c]

        return scores

    _stage_scores(chunk_scores(0), range(s_ref.shape[0] - 1), s_ref, cm_ref)

    def body(c, carry):
        def values(h):
            return vt_ref[0, c, h * HEAD_DIM:(h + 1) * HEAD_DIM, :]

        _attend_chunk(n_heads, chunk_scores(c), chunk_scores(jnp.minimum(c + 1, n_chunks - 1)), values,
                      s_ref, cm_ref, m_ref, acc_ref)
        return carry

    lax.fori_loop(0, n_chunks, body, 0)
    _write_heads(o_ref, acc_ref)


def _dsa_attn(qt, k, vt, bias, batch, seq):
    t, width = k.shape
    tq = Q_TILE
    nq = seq // tq
    n_heads = width // HEAD_DIM
    one = pl.Buffered(1)
    return pl.pallas_call(
        _dsa_attn_kernel,
        grid=(batch, nq),
        in_specs=[pl.BlockSpec((1, width, tq), lambda b, i: (b, 0, i)),
                  pl.BlockSpec((seq, width), lambda b, i: (b, 0), pipeline_mode=one),
                  pl.BlockSpec((1,) + vt.shape[1:], lambda b, i: (b, 0, 0, 0), pipeline_mode=one),
                  pl.BlockSpec((1, 1) + bias.shape[2:], lambda b, i: (b, i, 0, 0, 0))],
        out_specs=pl.BlockSpec((tq, width), lambda b, i: (b * nq + i, 0)),
        out_shape=jax.ShapeDtypeStruct((t, width), BF16),
        scratch_shapes=[pltpu.VMEM((n_heads, LANES, tq), BF16), pltpu.VMEM((SCORE_SLOTS, DSA_K_CHUNK, tq), F32),
                        pltpu.VMEM((SCORE_SLOTS, SUBLANES, tq), F32),
                        pltpu.VMEM((n_heads, 1, tq), F32), pltpu.VMEM((n_heads, HEAD_DIM + ONES_ROWS, tq), F32)],
        compiler_params=_params(2),
        name="dsa_attn",
    )(qt, k, vt, bias)


def _rope_tables(positions):
    inv_freq = ROPE_THETA ** (-jnp.arange(0, ROPE_DIM, 2, dtype=F32) / ROPE_DIM)
    m = jnp.arange(LANES) % HEAD_DIM
    ang = positions.astype(F32).reshape(-1, 1) * inv_freq[m % ROPE_HALF][None, :]
    cos, sin = jnp.cos(ang), jnp.sin(ang)
    c = jnp.where(m < ROPE_DIM, cos, 1.0)
    sa = jnp.where(m < ROPE_HALF, -sin, 0.0)
    sb = jnp.where((m >= ROPE_HALF) & (m < ROPE_DIM), sin, 0.0)
    return c, sa, sb


def kernel(x, p, positions, g_mix, w_in_even, a_ln_g, a_ln_b, a_w_s, a_b_s, w_out_even, w_in_odd, c_kidx_ln_g, c_kidx_ln_b, w_out_odd, g_ffn, w_ffn_gate, w_ffn_up, w_ffn_down, g_ple, w_ple_proj, w_ple_gate, g_final):
    batch, seq, d = x.shape
    depth = p.shape[0]
    t = batch * seq
    assert seq % ROW_TILE == 0 and seq % DSA_K_CHUNK == 0 and min(DSA_TOPK, seq // 4) <= DSA_K_CHUNK
    c, sa, sb = _rope_tables(positions)
    h = x.reshape(t, d)
    p2 = p.reshape(depth, t, p.shape[-1])
    row = lambda a: a.reshape(1, -1)
    onehot = (jnp.arange(seq)[:, None] // MOBA_BLOCK == jnp.arange(LANES)[None, :]).astype(BF16)
    bf = lambda w: w.astype(BF16)
    wo_even, wo_odd, wg, wu, wd, wpg, wpp = map(bf, (w_out_even, w_out_odd, w_ffn_gate, w_ffn_up, w_ffn_down,
                                                     w_ple_gate, w_ple_proj))
    for i in range(depth):
        j = i // 2
        if i % 2 == 0:
            bs = jnp.repeat(a_b_s[j].T, HEAD_DIM, axis=1)
            ya, qt, k, vt, km = _front0(h, row(g_mix[i]), bf(w_in_even[j]), c, sa, sb,
                                        row(a_ln_g[j]), row(a_ln_b[j]), a_w_s[j], bs, batch, seq)
            ys = [ya, _moba(qt, k, vt, km.reshape(t // MOBA_BLOCK, WIDTH_B), onehot, batch, seq)]
            wo = wo_even
        else:
            w = w_in_odd[j]
            o = 3 * WIDTH_C + IDX_HEADS * IDX_DIM
            w1 = bf(jnp.concatenate([w[:, :o], w[:, o:o + IDX_DIM], w[:, o:o + IDX_DIM], w[:, o + IDX_DIM:],
                                     jnp.zeros((d, LANES - IDX_HEADS), w.dtype)], axis=1))
            qt, k, vt, qit, ki, wit = _front1(h, row(g_mix[i]), w1, c, sa, sb, row(jnp.tile(c_kidx_ln_g[j], 2)),
                                              row(jnp.tile(c_kidx_ln_b[j], 2)), batch, seq)
            ys = [_dsa_attn(qt, k, vt, _dsa_index(qit, wit, ki, batch, seq), batch, seq)]
            wo = wo_odd
        h = _post(h, ys, wo, j, p2, i, row(g_ffn[i]), wg, wu, wd, row(g_ple[i]), wpg, wpp, row(g_final),
                  final=(i == depth - 1))
    return h.reshape(batch, seq, d)
```

```python
import functools

import jax
import jax.numpy as jnp
from jax import lax
from jax.experimental import pallas as pl
from jax.experimental.pallas import tpu as pltpu

HEAD_DIM = 64
ROPE_DIM = HEAD_DIM // 4
ROPE_HALF = ROPE_DIM // 2
ROPE_THETA = 500000.0
N_GROUPS_A = 8
WIDTH_A = N_GROUPS_A * HEAD_DIM
N_HEADS_B = 8
WIDTH_B = N_HEADS_B * HEAD_DIM
N_HEADS_C = 16
WIDTH_C = N_HEADS_C * HEAD_DIM
SGU_CHUNK = 128
MOBA_BLOCK = 256
MOBA_TOPK = 3
IDX_HEADS = 8
IDX_DIM = 64
DSA_TOPK = 256
EPS = 1e-6
NEG = -1e30
LOG2E = 1.4426950408889634
BF16_SPACING = 2.0 ** -7
TINY = 1e-30

LANES = 128
SUBLANES = 8
ROW_TILE = 512
Q_TILE = 256
DSA_K_CHUNK = 512
FOLD_CHAINS = 8
ONES_ROWS = 16
SCORE_SLOTS = 4
MOBA_SLOTS = 4
MOBA_K_CHUNK = 2 * MOBA_BLOCK
COARSE_STEPS = 10
BISECT_PER_ROUND = 8
BISECT_ROUNDS = 12
VMEM_LIMIT = 56 * 1024 * 1024

F32 = jnp.float32
BF16 = jnp.bfloat16


def _dot(a, b):
    return jnp.dot(a, b, preferred_element_type=F32)


def _rms(x, g):
    return x * lax.rsqrt(jnp.mean(x * x, axis=-1, keepdims=True) + EPS) * g


def _rope(z, c, sa, sb):
    outs = []
    for j in range(z.shape[1] // LANES):
        zz = z[:, j * LANES:(j + 1) * LANES]
        outs.append(zz * c + pltpu.roll(zz, LANES - ROPE_HALF, 1) * sa + pltpu.roll(zz, ROPE_HALF, 1) * sb)
    return outs[0] if len(outs) == 1 else jnp.concatenate(outs, axis=1)


def _gelu(x):
    return 0.5 * x * (1.0 + lax.erf(x * (2.0 ** -0.5)))


def _fold_rows(x, op, group=SUBLANES):
    parts = [x[a:a + group] for a in range(0, x.shape[0], group)]
    chains = parts[:FOLD_CHAINS]
    for a, part in enumerate(parts[FOLD_CHAINS:]):
        chains[a % FOLD_CHAINS] = op(chains[a % FOLD_CHAINS], part)
    while len(chains) > 1:
        chains = [op(chains[a], chains[a + 1]) if a + 1 < len(chains) else chains[a] for a in range(0, len(chains), 2)]
    return chains[0]


def _head_rows(x, hh):
    r = lax.broadcasted_iota(jnp.int32, (LANES, 1), 0)
    return jnp.where((r >= hh * HEAD_DIM) & (r < (hh + 1) * HEAD_DIM), x, jnp.zeros_like(x))


def _const_spec(shape):
    nd = len(shape)
    return pl.BlockSpec(shape, lambda *_: (0,) * nd, pipeline_mode=pl.Buffered(1))


def _params(n_axes):
    return pltpu.CompilerParams(dimension_semantics=("arbitrary",) * n_axes, vmem_limit_bytes=VMEM_LIMIT)


def _stage_scores(score_fn, heads, s_ref, cm_ref):
    n_slots = s_ref.shape[0]
    for h in heads:
        s = score_fn(h)
        s_ref[h % n_slots] = s
        cm_ref[h % n_slots] = _fold_rows(s, jnp.maximum)


def _attend_chunk(n_heads, score_fn, next_score_fn, vt_fn, s_ref, cm_ref, m_ref, acc_ref):
    n_slots = s_ref.shape[0]
    ahead = n_slots - 1
    assert n_heads % n_slots == 0
    for h in range(n_heads):
        if h + ahead < n_heads:
            _stage_scores(score_fn, [h + ahead], s_ref, cm_ref)
        else:
            _stage_scores(next_score_fn, [h + ahead - n_heads], s_ref, cm_ref)
        slot = h % n_slots
        m_old = m_ref[h]
        m_new = jnp.maximum(m_old, jnp.max(cm_ref[slot], axis=0, keepdims=True))
        p = jnp.exp2(s_ref[slot] - m_new).astype(BF16)
        m_ref[h] = m_new
        vt = vt_fn(h)
        vt = jnp.concatenate([vt, jnp.ones((acc_ref.shape[1] - vt.shape[0], vt.shape[1]), BF16)], axis=0)
        acc_ref[h] = jnp.exp2(m_old - m_new) * acc_ref[h] + _dot(vt, p)


def _softmax_init(m_ref, acc_ref):
    m_ref[...] = jnp.full(m_ref.shape, -jnp.inf, F32)
    acc_ref[...] = jnp.zeros(acc_ref.shape, F32)


def _write_heads(o_ref, acc_ref):
    def head(h):
        return acc_ref[h, :HEAD_DIM, :] / acc_ref[h, HEAD_DIM:HEAD_DIM + 1, :]

    for p in range(acc_ref.shape[0] // 2):
        o_ref[:, p * LANES:(p + 1) * LANES] = jnp.concatenate([head(2 * p), head(2 * p + 1)], axis=0).T.astype(BF16)


def _front0_kernel(h_ref, g_ref, w_ref, c_ref, sa_ref, sb_ref, lng_ref, lnb_ref, ws_ref, bs_ref,
                   ya_ref, qt_ref, k_ref, vt_ref, km_ref):
    tm = h_ref.shape[0]
    xn = _rms(h_ref[...], g_ref[...]).astype(BF16)

    def proj(lo, hi):
        return _dot(xn, w_ref[:, lo:hi])

    u = _gelu(proj(0, WIDTH_A))
    v = _gelu(proj(WIDTH_A, 2 * WIDTH_A))
    lane = lax.broadcasted_iota(jnp.int32, (1, LANES), 1)
    low = lane < HEAD_DIM
    row = lax.broadcasted_iota(jnp.int32, (SGU_CHUNK, SGU_CHUNK), 0)
    col = lax.broadcasted_iota(jnp.int32, (SGU_CHUNK, SGU_CHUNK), 1)
    tril = col <= row
    for j in range(WIDTH_A // LANES):
        sl = slice(j * LANES, (j + 1) * LANES)
        vv = v[:, sl]

        def half_mean(t):
            s_lo = jnp.sum(jnp.where(low, t, 0.0), axis=-1, keepdims=True)
            s_hi = jnp.sum(jnp.where(low, 0.0, t), axis=-1, keepdims=True)
            return jnp.where(low, s_lo, s_hi) * (1.0 / HEAD_DIM)

        xc = vv - half_mean(vv)
        var = half_mean(xc * xc)
        vn = (xc * lax.rsqrt(var + EPS) * lng_ref[:, sl] + lnb_ref[:, sl]).astype(BF16)
        w_lo = jnp.where(tril, ws_ref[2 * j], 0.0).astype(BF16)
        w_hi = jnp.where(tril, ws_ref[2 * j + 1], 0.0).astype(BF16)
        for c in range(tm // SGU_CHUNK):
            rs = slice(c * SGU_CHUNK, (c + 1) * SGU_CHUNK)
            vc = vn[rs]
            mixed = jnp.where(low, _dot(w_lo, vc), _dot(w_hi, vc)) + bs_ref[:, sl]
            ya_ref[rs, sl] = (u[rs, sl] * mixed).astype(BF16)

    o = 2 * WIDTH_A
    c, sa, sb = c_ref[...], sa_ref[...], sb_ref[...]
    q = _rope(proj(o, o + WIDTH_B), c, sa, sb) * (HEAD_DIM ** -0.5 * LOG2E)
    qt_ref[0] = q.T.astype(BF16)
    k = _rope(proj(o + WIDTH_B, o + 2 * WIDTH_B), c, sa, sb)
    k_ref[...] = k.astype(BF16)
    v_b = proj(o + 2 * WIDTH_B, o + 3 * WIDTH_B)
    for blk in range(tm // MOBA_BLOCK):
        rs = slice(blk * MOBA_BLOCK, (blk + 1) * MOBA_BLOCK)
        km_ref[0, blk:blk + 1, :] = jnp.mean(k[rs], axis=0, keepdims=True)
        vt_ref[0, blk] = v_b[rs].T.astype(BF16)


def _front0(h, g, w, c, sa, sb, lng, lnb, ws, bs, batch, seq):
    t, d = h.shape
    tm = ROW_TILE
    nt = seq // tm
    nblk = tm // MOBA_BLOCK
    row = lambda n: pl.BlockSpec((tm, n), lambda i: (i, 0))
    return pl.pallas_call(
        _front0_kernel,
        grid=(t // tm,),
        in_specs=[row(d), _const_spec((1, d)), _const_spec(w.shape), row(LANES), row(LANES), row(LANES),
                  _const_spec(lng.shape), _const_spec(lnb.shape), _const_spec(ws.shape), _const_spec(bs.shape)],
        out_specs=[row(WIDTH_A),
                   pl.BlockSpec((1, WIDTH_B, tm), lambda i: (i // nt, 0, i % nt)),
                   row(WIDTH_B),
                   pl.BlockSpec((1, nblk, WIDTH_B, MOBA_BLOCK), lambda i: (i // nt, i % nt, 0, 0)),
                   pl.BlockSpec((1, nblk, WIDTH_B), lambda i: (i, 0, 0))],
        out_shape=[jax.ShapeDtypeStruct((t, WIDTH_A), BF16),
                   jax.ShapeDtypeStruct((batch, WIDTH_B, seq), BF16),
                   jax.ShapeDtypeStruct((t, WIDTH_B), BF16),
                   jax.ShapeDtypeStruct((batch, seq // MOBA_BLOCK, WIDTH_B, MOBA_BLOCK), BF16),
                   jax.ShapeDtypeStruct((t // tm, nblk, WIDTH_B), F32)],
        compiler_params=_params(1),
        name="front0",
    )(h, g, w, c, sa, sb, lng, lnb, ws, bs)


def _moba_kernel(qt_ref, k_ref, vt_ref, km_ref, oh_ref, o_ref, qc_ref, s_ref, cm_ref, m_ref, acc_ref):
    i = pl.program_id(1)
    tq = qt_ref.shape[2]
    nb = km_ref.shape[0]
    n_heads = acc_ref.shape[0]
    kc_n = s_ref.shape[1]
    per_chunk = kc_n // MOBA_BLOCK
    blk = lax.broadcasted_iota(jnp.int32, (nb, 1), 0)
    blk_f = blk.astype(F32)
    q_blk = (i * tq + lax.broadcasted_iota(jnp.int32, (1, tq), 1)) // MOBA_BLOCK
    past = blk < q_blk
    _softmax_init(m_ref, acc_ref)
    for h in range(n_heads):
        sl = slice((h // 2) * LANES, (h // 2 + 1) * LANES)
        qm = _head_rows(qt_ref[0, sl, :], h % 2)
        g = jnp.where(past, _dot(km_ref[:, sl].astype(BF16), qm), NEG)
        sel = jnp.zeros(g.shape, jnp.bool_)
        for _ in range(min(MOBA_TOPK, nb)):
            top = jnp.max(g, axis=0, keepdims=True)
            idx = jnp.min(jnp.where(g == top, blk_f, 1e9), axis=0, keepdims=True)
            pick = blk_f == idx
            sel = sel | pick
            g = jnp.where(pick, -jnp.inf, g)
        selb = jnp.where((sel & past) | (blk == q_blk), 0.0, NEG).astype(BF16)
        qc_ref[h] = jnp.concatenate([qm, selb, jnp.zeros((LANES - nb, tq), BF16)], axis=0)

    def values(c):
        def vt(h):
            rows = slice(h * HEAD_DIM, (h + 1) * HEAD_DIM)
            return jnp.concatenate([vt_ref[0, per_chunk * c + a, rows, :] for a in range(per_chunk)], axis=1)
        return vt

    def chunk_scores(c, mask=None):
        st = pl.multiple_of(c * kc_n, kc_n)

        def scores(h):
            kc = jnp.concatenate([k_ref[pl.ds(st, kc_n), (h // 2) * LANES:(h // 2 + 1) * LANES],
                                  oh_ref[pl.ds(st, kc_n), :]], axis=1)
            s = _dot(kc, qc_ref[h])
            return s if mask is None else jnp.where(mask, s, NEG)

        return scores

    own = (i * tq) // kc_n
    kpos = own * kc_n + lax.broadcasted_iota(jnp.int32, (kc_n, 1), 0)
    qpos = i * tq + lax.broadcasted_iota(jnp.int32, (1, tq), 1)
    own_scores = chunk_scores(own, kpos <= qpos)
    refs = (s_ref, cm_ref, m_ref, acc_ref)
    last = jnp.maximum(own - 1, 0)
    _stage_scores(own_scores, range(s_ref.shape[0] - 1), s_ref, cm_ref)
    _attend_chunk(n_heads, own_scores, chunk_scores(0), values(own), *refs)

    def body(c, carry):
        _attend_chunk(n_heads, chunk_scores(c), chunk_scores(jnp.minimum(c + 1, last)), values(c), *refs)
        return carry

    lax.fori_loop(0, own, body, 0)
    _write_heads(o_ref, acc_ref)


def _moba(qt, k, vt, km, onehot, batch, seq):
    t, width = k.shape
    tq = MOBA_K_CHUNK
    nb = seq // MOBA_BLOCK
    nq = seq // tq
    n_heads = width // HEAD_DIM
    one = pl.Buffered(1)
    return pl.pallas_call(
        _moba_kernel,
        grid=(batch, nq),
        in_specs=[pl.BlockSpec((1, width, tq), lambda b, i: (b, 0, i)),
                  pl.BlockSpec((seq, width), lambda b, i: (b, 0), pipeline_mode=one),
                  pl.BlockSpec((1, nb, width, MOBA_BLOCK), lambda b, i: (b, 0, 0, 0), pipeline_mode=one),
                  pl.BlockSpec((nb, width), lambda b, i: (b, 0)),
                  _const_spec(onehot.shape)],
        out_specs=pl.BlockSpec((tq, width), lambda b, i: (b * nq + i, 0)),
        out_shape=jax.ShapeDtypeStruct((t, width), BF16),
        scratch_shapes=[pltpu.VMEM((n_heads, 2 * LANES, tq), BF16), pltpu.VMEM((MOBA_SLOTS, MOBA_K_CHUNK, tq), F32),
                        pltpu.VMEM((MOBA_SLOTS, SUBLANES, tq), F32),
                        pltpu.VMEM((n_heads, 1, tq), F32), pltpu.VMEM((n_heads, HEAD_DIM + ONES_ROWS, tq), F32)],
        compiler_params=_params(2),
        name="moba",
    )(qt, k, vt, km, onehot)


def _post_kernel(*refs, n_mix, final):
    h_ref = refs[0]
    y_refs = refs[1:1 + n_mix]
    (wo_ref, p_ref, gf_ref, wg_ref, wu_ref, wd_ref, gp_ref, wpg_ref, wpp_ref, gl_ref, o_ref, a_ref) = refs[1 + n_mix:]
    h = h_ref[...]
    lo = 0
    for y_ref in y_refs:
        h = h + _dot(y_ref[...], wo_ref[0, lo:lo + y_ref.shape[1], :])
        lo += y_ref.shape[1]
    xn = _rms(h, gf_ref[...]).astype(BF16)
    d_ff = wg_ref.shape[2]
    step = 2 * LANES
    for lo in range(0, d_ff, step):
        gate = _dot(xn, wg_ref[0, :, lo:lo + step])
        up = _dot(xn, wu_ref[0, :, lo:lo + step])
        a_ref[:, lo:lo + step] = (gate * jax.nn.sigmoid(gate) * up).astype(BF16)
    h = h + _dot(a_ref[...], wd_ref[0])
    pg = jax.nn.sigmoid(_dot(_rms(h, gp_ref[...]).astype(BF16), wpg_ref[0]))
    h = h + _dot(p_ref[0].astype(BF16), wpp_ref[0]) * pg
    o_ref[...] = _rms(h, gl_ref[...]) if final else h


def _layer_spec(shape, layer):
    rest = (0,) * (len(shape) - 1)
    return pl.BlockSpec((1,) + tuple(shape[1:]), lambda *_: (layer,) + rest, pipeline_mode=pl.Buffered(1))


def _post(h, ys, wo, wo_layer, p, layer, gf, wg, wu, wd, gp, wpg, wpp, gl, final):
    t, d = h.shape
    tm = ROW_TILE
    row = lambda n: pl.BlockSpec((tm, n), lambda i: (i, 0))
    stack = lambda w: _layer_spec(w.shape, layer)
    return pl.pallas_call(
        functools.partial(_post_kernel, n_mix=len(ys), final=final),
        grid=(t // tm,),
        in_specs=[row(d)] + [row(y.shape[1]) for y in ys]
                 + [_layer_spec(wo.shape, wo_layer), pl.BlockSpec((1, tm, p.shape[2]), lambda i: (layer, i, 0)),
                    _const_spec(gf.shape), stack(wg), stack(wu), stack(wd), _const_spec(gp.shape), stack(wpg),
                    stack(wpp), _const_spec(gl.shape)],
        out_specs=row(d),
        out_shape=jax.ShapeDtypeStruct((t, d), F32),
        scratch_shapes=[pltpu.VMEM((tm, wg.shape[2]), BF16)],
        compiler_params=_params(1),
        name="post_final" if final else "post",
    )(h, *ys, wo, p, gf, wg, wu, wd, gp, wpg, wpp, gl)


def _front1_kernel(h_ref, g_ref, w_ref, c_ref, sa_ref, sb_ref, lng_ref, lnb_ref,
                   qt_ref, k_ref, vt_ref, qit_ref, ki_ref, wit_ref):
    xn = _rms(h_ref[...], g_ref[...]).astype(BF16)

    def proj(lo, hi):
        return _dot(xn, w_ref[:, lo:hi])

    c, sa, sb = c_ref[...], sa_ref[...], sb_ref[...]
    qt_ref[0] = (_rope(proj(0, WIDTH_C), c, sa, sb) * (HEAD_DIM ** -0.5 * LOG2E)).T.astype(BF16)
    k_ref[...] = _rope(proj(WIDTH_C, 2 * WIDTH_C), c, sa, sb).astype(BF16)
    vt_ref[0, 0] = proj(2 * WIDTH_C, 3 * WIDTH_C).T.astype(BF16)
    o = 3 * WIDTH_C
    qit_ref[0] = _rope(proj(o, o + IDX_HEADS * IDX_DIM), c, sa, sb).T.astype(BF16)
    o += IDX_HEADS * IDX_DIM
    ki = proj(o, o + LANES)
    xc = ki - jnp.mean(ki, axis=-1, keepdims=True)
    var = jnp.mean(xc * xc, axis=-1, keepdims=True)
    ki = xc * lax.rsqrt(var + EPS) * lng_ref[...] + lnb_ref[...]
    ki_ref[...] = _rope(ki, c, sa, sb).astype(BF16)
    wi = proj(o + LANES, o + 2 * LANES) * ((IDX_HEADS ** -0.5) * (IDX_DIM ** -0.5))
    wit_ref[0] = wi.T[:IDX_HEADS]


def _front1(h, g, w, c, sa, sb, lng, lnb, batch, seq):
    t, d = h.shape
    tm = ROW_TILE
    assert tm == DSA_K_CHUNK
    nt = seq // tm
    row = lambda n: pl.BlockSpec((tm, n), lambda i: (i, 0))
    tr = lambda n: pl.BlockSpec((1, n, tm), lambda i: (i // nt, 0, i % nt))
    return pl.pallas_call(
        _front1_kernel,
        grid=(t // tm,),
        in_specs=[row(d), _const_spec((1, d)), _const_spec(w.shape), row(LANES), row(LANES), row(LANES),
                  _const_spec(lng.shape), _const_spec(lnb.shape)],
        out_specs=[tr(WIDTH_C), row(WIDTH_C),
                   pl.BlockSpec((1, 1, WIDTH_C, tm), lambda i: (i // nt, i % nt, 0, 0)),
                   tr(IDX_HEADS * IDX_DIM), row(LANES), tr(IDX_HEADS)],
        out_shape=[jax.ShapeDtypeStruct((batch, WIDTH_C, seq), BF16),
                   jax.ShapeDtypeStruct((t, WIDTH_C), BF16),
                   jax.ShapeDtypeStruct((batch, nt, WIDTH_C, tm), BF16),
                   jax.ShapeDtypeStruct((batch, IDX_HEADS * IDX_DIM, seq), BF16),
                   jax.ShapeDtypeStruct((t, LANES), BF16),
                   jax.ShapeDtypeStruct((batch, IDX_HEADS, seq), F32)],
        compiler_params=_params(1),
        name="front1",
    )(h, g, w, c, sa, sb, lng, lnb)


def _dsa_select(qit_ref, wit_ref, ki_ref, sc_ref, sc16_ref, n_sel):
    i = pl.program_id(1)
    tq = qit_ref.shape[2]
    kc_n = DSA_K_CHUNK
    n_chunks = (i * tq + tq + kc_n - 1) // kc_n
    qpos = i * tq + lax.broadcasted_iota(jnp.int32, (1, tq), 1)
    kofs = lax.broadcasted_iota(jnp.int32, (kc_n, 1), 0)
    wv = wit_ref[0]

    def causal(c):
        return (c * kc_n + kofs) <= qpos

    big = 3e38

    def score_chunk(c, carry):
        mx, mn = carry
        kc = ki_ref[pl.ds(pl.multiple_of(c * kc_n, kc_n), kc_n), :]
        acc = jnp.zeros((kc_n, tq), F32)
        for h in range(IDX_HEADS):
            qm = _head_rows(qit_ref[0, (h // 2) * LANES:(h // 2 + 1) * LANES, :], h % 2)
            acc = acc + jnp.maximum(_dot(kc, qm), 0.0) * wv[h:h + 1, :]
        ok = causal(c)
        sc_ref[c] = jnp.where(ok, acc, NEG)
        sc16_ref[c] = jnp.where(ok, acc, NEG).astype(BF16)
        return (jnp.maximum(mx, _fold_rows(jnp.where(ok, acc, -big), jnp.maximum)),
                jnp.minimum(mn, _fold_rows(jnp.where(ok, acc, big), jnp.minimum)))

    mx, mn = lax.fori_loop(0, n_chunks, score_chunk,
                           (jnp.full((SUBLANES, tq), -big, F32), jnp.full((SUBLANES, tq), big, F32)))
    col_max = jnp.max(mx, axis=0, keepdims=True)
    col_min = jnp.min(mn, axis=0, keepdims=True)

    def reduce_chunks(fn, op, init):
        def body(c, acc):
            return op(acc, _fold_rows(fn(sc_ref[c]), op))
        return lax.fori_loop(0, n_chunks, body, jnp.full((SUBLANES, tq), init, F32))

    def count_ge(thr):
        acc = reduce_chunks(lambda x: jnp.where(x >= thr, 1.0, 0.0), jnp.add, 0.0)
        return jnp.sum(acc, axis=0, keepdims=True)

    def pending(flags):
        return jnp.max(jnp.where(flags, 0.0, 1.0))

    k_f = float(n_sel)
    n_causal = (qpos + 1).astype(F32)
    small = n_causal < k_f
    rows16 = 2 * SUBLANES

    def count_ge_bf16(thr):
        one, zero = jnp.ones((), BF16), jnp.zeros((), BF16)

        def body(c, acc):
            hits = jnp.where(sc16_ref[c] >= thr, one, zero)
            return acc + _fold_rows(hits, jnp.add, rows16).astype(F32)

        return jnp.sum(lax.fori_loop(0, n_chunks, body, jnp.zeros((rows16, tq), F32)), axis=0, keepdims=True)

    def coarse(_, carry):
        lo, hi = carry
        mid = (0.5 * (lo + hi)).astype(BF16)
        ok = count_ge_bf16(mid) >= k_f
        mid = mid.astype(F32)
        return jnp.where(ok, mid, lo), jnp.where(ok, hi, mid)

    top = col_max.astype(BF16).astype(F32)
    lo, hi = lax.fori_loop(0, COARSE_STEPS, coarse,
                           (col_min.astype(BF16).astype(F32),
                            (top + jnp.abs(top) * 2.0 * BF16_SPACING + TINY).astype(BF16).astype(F32)))
    lo = jnp.where(small, NEG, lo - jnp.abs(lo) * BF16_SPACING - TINY)
    c_lo = jnp.full((1, tq), 2.0 * k_f, F32)
    c_hi = jnp.full((1, tq), -1.0, F32)

    def bisect(_, carry):
        lo, hi, c_lo, c_hi = carry
        mid = 0.5 * (lo + hi)
        cnt = count_ge(mid)
        ok = cnt >= k_f
        return (jnp.where(ok, mid, lo), jnp.where(ok, hi, mid), jnp.where(ok, cnt, c_lo), jnp.where(ok, c_hi, cnt))

    def bracket_ends(lo, hi):
        def body(c, carry):
            below, above = carry
            x = sc_ref[c]
            return (jnp.maximum(below, _fold_rows(jnp.where(x < hi, x, -big), jnp.maximum)),
                    jnp.minimum(above, _fold_rows(jnp.where(x >= lo, x, big), jnp.minimum)))

        below, above = lax.fori_loop(0, n_chunks, body, (jnp.full((SUBLANES, tq), -big, F32),
                                                         jnp.full((SUBLANES, tq), big, F32)))
        return jnp.max(below, axis=0, keepdims=True), jnp.min(above, axis=0, keepdims=True)

    def settled(c_lo, c_hi, below, above):
        return small | (c_hi == k_f - 1.0) | (c_lo == k_f) | (below == above)

    def narrow_cond(carry):
        return (carry[6] < BISECT_ROUNDS) & (carry[7] > 0.5)

    def narrow(carry):
        state = lax.fori_loop(0, BISECT_PER_ROUND, bisect, carry[:4])
        below, above = bracket_ends(state[0], state[1])
        return (*state, below, above, carry[6] + 1, pending(settled(state[2], state[3], below, above)))

    zeros = jnp.zeros((1, tq), F32)
    lo, hi, c_lo, c_hi, below, above, _, _ = lax.while_loop(
        narrow_cond, narrow, (lo, hi, c_lo, c_hi, zeros, zeros, jnp.int32(0), jnp.float32(1.0)))
    thr = jnp.where(small, NEG, jnp.where(c_lo == k_f, above, below))
    done = jnp.where(settled(c_lo, c_hi, below, above), 1.0, 0.0)

    def snap_cond(carry):
        return carry[3] > 0.5

    def snap(carry):
        hi, thr, done, _ = carry
        cand = jnp.max(reduce_chunks(lambda x: jnp.where(x < hi, x, -big), jnp.maximum, -big),
                       axis=0, keepdims=True)
        ok = count_ge(cand) >= k_f
        thr = jnp.where(done > 0.5, thr, cand)
        done = jnp.where(ok, 1.0, done)
        hi = jnp.where(done > 0.5, hi, cand)
        return hi, thr, done, jnp.max(1.0 - done)

    _, thr, _, _ = lax.while_loop(snap_cond, snap, (hi, thr, done, jnp.max(1.0 - done)))

    n_ge = count_ge(thr)
    tied = jnp.max(jnp.where((n_ge > k_f) & jnp.logical_not(small), 1.0, 0.0))

    @pl.when(tied < 0.5)
    def _():
        def body(c, carry):
            x = sc_ref[c]
            sc_ref[c] = jnp.where((x >= thr) & causal(c), 0.0, NEG)
            return carry
        lax.fori_loop(0, n_chunks, body, 0)

    @pl.when(tied > 0.5)
    def _():
        n_gt = reduce_chunks(lambda x: jnp.where(x > thr, 1.0, 0.0), jnp.add, 0.0)
        need = k_f - jnp.sum(n_gt, axis=0, keepdims=True)

        def body(c, seen):
            x = sc_ref[c]
            eq = x == thr
            total = seen + jnp.sum(_fold_rows(jnp.where(eq, 1.0, 0.0), jnp.add), axis=0, keepdims=True)
            keep_all = total <= need
            crossing = jnp.max(jnp.where(keep_all | (seen >= need), 0.0, 1.0))

            @pl.when(crossing < 0.5)
            def _():
                sc_ref[c] = jnp.where(((x > thr) | (eq & keep_all)) & causal(c), 0.0, NEG)

            @pl.when(crossing > 0.5)
            def _():
                r = lax.broadcasted_iota(jnp.int32, (kc_n, kc_n), 0)
                cc = lax.broadcasted_iota(jnp.int32, (kc_n, kc_n), 1)
                prefix = jnp.where(cc <= r, 1.0, 0.0).astype(BF16)
                cnt = seen + _dot(prefix, jnp.where(eq, 1.0, 0.0).astype(BF16))
                sc_ref[c] = jnp.where(((x > thr) | (eq & (cnt <= need))) & causal(c), 0.0, NEG)

            return total
        lax.fori_loop(0, n_chunks, body, jnp.zeros((1, tq), F32))


def _dsa_kernel(qit_ref, wit_ref, ki_ref, qt_ref, k_ref, vt_ref, o_ref,
                sc_ref, sc16_ref, qm_ref, s_ref, cm_ref, m_ref, acc_ref, *, n_sel):
    i = pl.program_id(1)
    tq = qt_ref.shape[2]
    kc_n = DSA_K_CHUNK
    n_heads = acc_ref.shape[0]
    n_chunks = (i * tq + tq + kc_n - 1) // kc_n
    _dsa_select(qit_ref, wit_ref, ki_ref, sc_ref, sc16_ref, n_sel)
    _softmax_init(m_ref, acc_ref)
    for h in range(n_heads):
        qm_ref[h] = _head_rows(qt_ref[0, (h // 2) * LANES:(h // 2 + 1) * LANES, :], h % 2)

    def chunk_scores(c):
        st = pl.multiple_of(c * kc_n, kc_n)

        def scores(h):
            return _dot(k_ref[pl.ds(st, kc_n), (h // 2) * LANES:(h // 2 + 1) * LANES], qm_ref[h]) + sc_ref[c]

        return scores

    _stage_scores(chunk_scores(0), range(s_ref.shape[0] - 1), s_ref, cm_ref)

    def body(c, carry):
        def values(h):
            return vt_ref[0, c, h * HEAD_DIM:(h + 1) * HEAD_DIM, :]

        _attend_chunk(n_heads, chunk_scores(c), chunk_scores(jnp.minimum(c + 1, n_chunks - 1)), values,
                      s_ref, cm_ref, m_ref, acc_ref)
        return carry

    lax.fori_loop(0, n_chunks, body, 0)
    _write_heads(o_ref, acc_ref)


def _dsa(qit, wit, ki, qt, k, vt, batch, seq):
    t, width = k.shape
    tq = Q_TILE
    nq = seq // tq
    nch = seq // DSA_K_CHUNK
    n_heads = width // HEAD_DIM
    one = pl.Buffered(1)
    return pl.pallas_call(
        functools.partial(_dsa_kernel, n_sel=min(DSA_TOPK, seq // 4)),
        grid=(batch, nq),
        in_specs=[pl.BlockSpec((1, qit.shape[1], tq), lambda b, i: (b, 0, i)),
                  pl.BlockSpec((1, IDX_HEADS, tq), lambda b, i: (b, 0, i)),
                  pl.BlockSpec((seq, LANES), lambda b, i: (b, 0), pipeline_mode=one),
                  pl.BlockSpec((1, width, tq), lambda b, i: (b, 0, i)),
                  pl.BlockSpec((seq, width), lambda b, i: (b, 0), pipeline_mode=one),
                  pl.BlockSpec((1,) + vt.shape[1:], lambda b, i: (b, 0, 0, 0), pipeline_mode=one)],
        out_specs=pl.BlockSpec((tq, width), lambda b, i: (b * nq + i, 0)),
        out_shape=jax.ShapeDtypeStruct((t, width), BF16),
        scratch_shapes=[pltpu.VMEM((nch, DSA_K_CHUNK, tq), F32), pltpu.VMEM((nch, DSA_K_CHUNK, tq), BF16),
                        pltpu.VMEM((n_heads, LANES, tq), BF16), pltpu.VMEM((SCORE_SLOTS, DSA_K_CHUNK, tq), F32),
                        pltpu.VMEM((SCORE_SLOTS, SUBLANES, tq), F32),
                        pltpu.VMEM((n_heads, 1, tq), F32), pltpu.VMEM((n_heads, HEAD_DIM + ONES_ROWS, tq), F32)],
        compiler_params=_params(2),
        name="dsa",
    )(qit, wit, ki, qt, k, vt)


def _rope_tables(positions):
    inv_freq = ROPE_THETA ** (-jnp.arange(0, ROPE_DIM, 2, dtype=F32) / ROPE_DIM)
    m = jnp.arange(LANES) % HEAD_DIM
    ang = positions.astype(F32).reshape(-1, 1) * inv_freq[m % ROPE_HALF][None, :]
    cos, sin = jnp.cos(ang), jnp.sin(ang)
    c = jnp.where(m < ROPE_DIM, cos, 1.0)
    sa = jnp.where(m < ROPE_HALF, -sin, 0.0)
    sb = jnp.where((m >= ROPE_HALF) & (m < ROPE_DIM), sin, 0.0)
    return c, sa, sb


def kernel(x, p, positions, g_mix, w_in_even, a_ln_g, a_ln_b, a_w_s, a_b_s, w_out_even, w_in_odd, c_kidx_ln_g, c_kidx_ln_b, w_out_odd, g_ffn, w_ffn_gate, w_ffn_up, w_ffn_down, g_ple, w_ple_proj, w_ple_gate, g_final):
    batch, seq, d = x.shape
    depth = p.shape[0]
    t = batch * seq
    assert seq % ROW_TILE == 0 and seq % DSA_K_CHUNK == 0 and min(DSA_TOPK, seq // 4) <= DSA_K_CHUNK
    c, sa, sb = _rope_tables(positions)
    h = x.reshape(t, d)
    p2 = p.reshape(depth, t, p.shape[-1])
    row = lambda a: a.reshape(1, -1)
    onehot = (jnp.arange(seq)[:, None] // MOBA_BLOCK == jnp.arange(LANES)[None, :]).astype(BF16)
    bf = lambda w: w.astype(BF16)
    wo_even, wo_odd, wg, wu, wd, wpg, wpp = map(bf, (w_out_even, w_out_odd, w_ffn_gate, w_ffn_up, w_ffn_down,
                                                     w_ple_gate, w_ple_proj))
    for i in range(depth):
        j = i // 2
        if i % 2 == 0:
            bs = jnp.repeat(a_b_s[j].T, HEAD_DIM, axis=1)
            ya, qt, k, vt, km = _front0(h, row(g_mix[i]), bf(w_in_even[j]), c, sa, sb,
                                        row(a_ln_g[j]), row(a_ln_b[j]), a_w_s[j], bs, batch, seq)
            ys = [ya, _moba(qt, k, vt, km.reshape(t // MOBA_BLOCK, WIDTH_B), onehot, batch, seq)]
            wo = wo_even
        else:
            w = w_in_odd[j]
            o = 3 * WIDTH_C + IDX_HEADS * IDX_DIM
            w1 = bf(jnp.concatenate([w[:, :o], w[:, o:o + IDX_DIM], w[:, o:o + IDX_DIM], w[:, o + IDX_DIM:],
                                     jnp.zeros((d, LANES - IDX_HEADS), w.dtype)], axis=1))
            qt, k, vt, qit, ki, wit = _front1(h, row(g_mix[i]), w1, c, sa, sb, row(jnp.tile(c_kidx_ln_g[j], 2)),
                                              row(jnp.tile(c_kidx_ln_b[j], 2)), batch, seq)
            ys = [_dsa(qit, wit, ki, qt, k, vt, batch, seq)]
            wo = wo_odd
        h = _post(h, ys, wo, j, p2, i, row(g_ffn[i]), wg, wu, wd, row(g_ple[i]), wpg, wpp, row(g_final),
                  final=(i == depth - 1))
    return h.reshape(batch, seq, d)
```

```python
import functools

import jax
import jax.numpy as jnp
from jax import lax
from jax.experimental import pallas as pl
from jax.experimental.pallas import tpu as pltpu

HEAD_DIM = 64
ROPE_DIM = HEAD_DIM // 4
ROPE_HALF = ROPE_DIM // 2
ROPE_THETA = 500000.0
N_GROUPS_A = 8
WIDTH_A = N_GROUPS_A * HEAD_DIM
N_HEADS_B = 8
WIDTH_B = N_HEADS_B * HEAD_DIM
N_HEADS_C = 16
WIDTH_C = N_HEADS_C * HEAD_DIM
SGU_CHUNK = 128
MOBA_BLOCK = 256
MOBA_TOPK = 3
IDX_HEADS = 8
IDX_DIM = 64
DSA_TOPK = 256
EPS = 1e-6
NEG = -1e30
LOG2E = 1.4426950408889634
BF16_SPACING = 2.0 ** -7
TINY = 1e-30

LANES = 128
SUBLANES = 8
ROW_TILE = 512
Q_TILE = 256
DSA_K_CHUNK = 512
FOLD_CHAINS = 8
ONES_ROWS = 16
SCORE_SLOTS = 4
MOBA_SLOTS = 4
MOBA_K_CHUNK = 2 * MOBA_BLOCK
COARSE_STEPS = 10
BISECT_PER_ROUND = 8
BISECT_ROUNDS = 12
VMEM_LIMIT = 56 * 1024 * 1024

F32 = jnp.float32
BF16 = jnp.bfloat16


def _dot(a, b):
    return jnp.dot(a, b, preferred_element_type=F32)


def _rms(x, g):
    return x * lax.rsqrt(jnp.mean(x * x, axis=-1, keepdims=True) + EPS) * g


def _rope(z, c, sa, sb):
    outs = []
    for j in range(z.shape[1] // LANES):
        zz = z[:, j * LANES:(j + 1) * LANES]
        outs.append(zz * c + pltpu.roll(zz, LANES - ROPE_HALF, 1) * sa + pltpu.roll(zz, ROPE_HALF, 1) * sb)
    return outs[0] if len(outs) == 1 else jnp.concatenate(outs, axis=1)


def _gelu(x):
    return 0.5 * x * (1.0 + lax.erf(x * (2.0 ** -0.5)))


def _fold_rows(x, op, group=SUBLANES):
    parts = [x[a:a + group] for a in range(0, x.shape[0], group)]
    chains = parts[:FOLD_CHAINS]
    for a, part in enumerate(parts[FOLD_CHAINS:]):
        chains[a % FOLD_CHAINS] = op(chains[a % FOLD_CHAINS], part)
    while len(chains) > 1:
        chains = [op(chains[a], chains[a + 1]) if a + 1 < len(chains) else chains[a] for a in range(0, len(chains), 2)]
    return chains[0]


def _head_rows(x, hh):
    r = lax.broadcasted_iota(jnp.int32, (LANES, 1), 0)
    return jnp.where((r >= hh * HEAD_DIM) & (r < (hh + 1) * HEAD_DIM), x, jnp.zeros_like(x))


def _const_spec(shape):
    nd = len(shape)
    return pl.BlockSpec(shape, lambda *_: (0,) * nd, pipeline_mode=pl.Buffered(1))


def _params(n_axes):
    return pltpu.CompilerParams(dimension_semantics=("parallel",) + ("arbitrary",) * (n_axes - 1),
                                vmem_limit_bytes=VMEM_LIMIT)


def _stage_scores(score_fn, heads, s_ref, cm_ref):
    n_slots = s_ref.shape[0]
    for h in heads:
        s = score_fn(h)
        s_ref[h % n_slots] = s
        cm_ref[h % n_slots] = _fold_rows(s, jnp.maximum)


def _attend_chunk(n_heads, score_fn, next_score_fn, vt_fn, s_ref, cm_ref, m_ref, acc_ref):
    n_slots = s_ref.shape[0]
    ahead = n_slots - 1
    assert n_heads % n_slots == 0
    for h in range(n_heads):
        if h + ahead < n_heads:
            _stage_scores(score_fn, [h + ahead], s_ref, cm_ref)
        else:
            _stage_scores(next_score_fn, [h + ahead - n_heads], s_ref, cm_ref)
        slot = h % n_slots
        m_old = m_ref[h]
        m_new = jnp.maximum(m_old, jnp.max(cm_ref[slot], axis=0, keepdims=True))
        p = jnp.exp2(s_ref[slot] - m_new).astype(BF16)
        m_ref[h] = m_new
        vt = vt_fn(h)
        vt = jnp.concatenate([vt, jnp.ones((acc_ref.shape[1] - vt.shape[0], vt.shape[1]), BF16)], axis=0)
        acc_ref[h] = jnp.exp2(m_old - m_new) * acc_ref[h] + _dot(vt, p)


def _softmax_init(m_ref, acc_ref):
    m_ref[...] = jnp.full(m_ref.shape, -jnp.inf, F32)
    acc_ref[...] = jnp.zeros(acc_ref.shape, F32)


def _write_heads(o_ref, acc_ref):
    def head(h):
        return acc_ref[h, :HEAD_DIM, :] / acc_ref[h, HEAD_DIM:HEAD_DIM + 1, :]

    for p in range(acc_ref.shape[0] // 2):
        o_ref[:, p * LANES:(p + 1) * LANES] = jnp.concatenate([head(2 * p), head(2 * p + 1)], axis=0).T.astype(BF16)


def _front0_kernel(h_ref, g_ref, w_ref, c_ref, sa_ref, sb_ref, lng_ref, lnb_ref, ws_ref, bs_ref,
                   ya_ref, qt_ref, k_ref, vt_ref, km_ref):
    tm = h_ref.shape[0]
    xn = _rms(h_ref[...], g_ref[...]).astype(BF16)

    def proj(lo, hi):
        return _dot(xn, w_ref[:, lo:hi])

    u = _gelu(proj(0, WIDTH_A))
    v = _gelu(proj(WIDTH_A, 2 * WIDTH_A))
    lane = lax.broadcasted_iota(jnp.int32, (1, LANES), 1)
    low = lane < HEAD_DIM
    row = lax.broadcasted_iota(jnp.int32, (SGU_CHUNK, SGU_CHUNK), 0)
    col = lax.broadcasted_iota(jnp.int32, (SGU_CHUNK, SGU_CHUNK), 1)
    tril = col <= row
    for j in range(WIDTH_A // LANES):
        sl = slice(j * LANES, (j + 1) * LANES)
        vv = v[:, sl]

        def half_mean(t):
            s_lo = jnp.sum(jnp.where(low, t, 0.0), axis=-1, keepdims=True)
            s_hi = jnp.sum(jnp.where(low, 0.0, t), axis=-1, keepdims=True)
            return jnp.where(low, s_lo, s_hi) * (1.0 / HEAD_DIM)

        xc = vv - half_mean(vv)
        var = half_mean(xc * xc)
        vn = (xc * lax.rsqrt(var + EPS) * lng_ref[:, sl] + lnb_ref[:, sl]).astype(BF16)
        w_lo = jnp.where(tril, ws_ref[2 * j], 0.0).astype(BF16)
        w_hi = jnp.where(tril, ws_ref[2 * j + 1], 0.0).astype(BF16)
        for c in range(tm // SGU_CHUNK):
            rs = slice(c * SGU_CHUNK, (c + 1) * SGU_CHUNK)
            vc = vn[rs]
            mixed = jnp.where(low, _dot(w_lo, vc), _dot(w_hi, vc)) + bs_ref[:, sl]
            ya_ref[rs, sl] = (u[rs, sl] * mixed).astype(BF16)

    o = 2 * WIDTH_A
    c, sa, sb = c_ref[...], sa_ref[...], sb_ref[...]
    q = _rope(proj(o, o + WIDTH_B), c, sa, sb) * (HEAD_DIM ** -0.5 * LOG2E)
    qt_ref[0] = q.T.astype(BF16)
    k = _rope(proj(o + WIDTH_B, o + 2 * WIDTH_B), c, sa, sb)
    k_ref[...] = k.astype(BF16)
    v_b = proj(o + 2 * WIDTH_B, o + 3 * WIDTH_B)
    for blk in range(tm // MOBA_BLOCK):
        rs = slice(blk * MOBA_BLOCK, (blk + 1) * MOBA_BLOCK)
        km_ref[0, blk:blk + 1, :] = jnp.mean(k[rs], axis=0, keepdims=True)
        vt_ref[0, blk] = v_b[rs].T.astype(BF16)


def _front0(h, g, w, c, sa, sb, lng, lnb, ws, bs, batch, seq):
    t, d = h.shape
    tm = ROW_TILE
    nt = seq // tm
    nblk = tm // MOBA_BLOCK
    row = lambda n: pl.BlockSpec((tm, n), lambda i: (i, 0))
    return pl.pallas_call(
        _front0_kernel,
        grid=(t // tm,),
        in_specs=[row(d), _const_spec((1, d)), _const_spec(w.shape), row(LANES), row(LANES), row(LANES),
                  _const_spec(lng.shape), _const_spec(lnb.shape), _const_spec(ws.shape), _const_spec(bs.shape)],
        out_specs=[row(WIDTH_A),
                   pl.BlockSpec((1, WIDTH_B, tm), lambda i: (i // nt, 0, i % nt)),
                   row(WIDTH_B),
                   pl.BlockSpec((1, nblk, WIDTH_B, MOBA_BLOCK), lambda i: (i // nt, i % nt, 0, 0)),
                   pl.BlockSpec((1, nblk, WIDTH_B), lambda i: (i, 0, 0))],
        out_shape=[jax.ShapeDtypeStruct((t, WIDTH_A), BF16),
                   jax.ShapeDtypeStruct((batch, WIDTH_B, seq), BF16),
                   jax.ShapeDtypeStruct((t, WIDTH_B), BF16),
                   jax.ShapeDtypeStruct((batch, seq // MOBA_BLOCK, WIDTH_B, MOBA_BLOCK), BF16),
                   jax.ShapeDtypeStruct((t // tm, nblk, WIDTH_B), F32)],
        compiler_params=_params(1),
        name="front0",
    )(h, g, w, c, sa, sb, lng, lnb, ws, bs)


def _moba_kernel(qt_ref, k_ref, vt_ref, km_ref, oh_ref, o_ref, qc_ref, s_ref, cm_ref, m_ref, acc_ref):
    i = pl.program_id(1)
    tq = qt_ref.shape[2]
    nb = km_ref.shape[0]
    n_heads = acc_ref.shape[0]
    kc_n = s_ref.shape[1]
    per_chunk = kc_n // MOBA_BLOCK
    blk = lax.broadcasted_iota(jnp.int32, (nb, 1), 0)
    blk_f = blk.astype(F32)
    q_blk = (i * tq + lax.broadcasted_iota(jnp.int32, (1, tq), 1)) // MOBA_BLOCK
    past = blk < q_blk
    _softmax_init(m_ref, acc_ref)
    for h in range(n_heads):
        sl = slice((h // 2) * LANES, (h // 2 + 1) * LANES)
        qm = _head_rows(qt_ref[0, sl, :], h % 2)
        g = jnp.where(past, _dot(km_ref[:, sl].astype(BF16), qm), NEG)
        sel = jnp.zeros(g.shape, jnp.bool_)
        for _ in range(min(MOBA_TOPK, nb)):
            top = jnp.max(g, axis=0, keepdims=True)
            idx = jnp.min(jnp.where(g == top, blk_f, 1e9), axis=0, keepdims=True)
            pick = blk_f == idx
            sel = sel | pick
            g = jnp.where(pick, -jnp.inf, g)
        selb = jnp.where((sel & past) | (blk == q_blk), 0.0, NEG).astype(BF16)
        qc_ref[h] = jnp.concatenate([qm, selb, jnp.zeros((LANES - nb, tq), BF16)], axis=0)

    def values(c):
        def vt(h):
            rows = slice(h * HEAD_DIM, (h + 1) * HEAD_DIM)
            return jnp.concatenate([vt_ref[0, per_chunk * c + a, rows, :] for a in range(per_chunk)], axis=1)
        return vt

    def chunk_scores(c, mask=None):
        st = pl.multiple_of(c * kc_n, kc_n)

        def scores(h):
            kc = jnp.concatenate([k_ref[pl.ds(st, kc_n), (h // 2) * LANES:(h // 2 + 1) * LANES],
                                  oh_ref[pl.ds(st, kc_n), :]], axis=1)
            s = _dot(kc, qc_ref[h])
            return s if mask is None else jnp.where(mask, s, NEG)

        return scores

    own = (i * tq) // kc_n
    kpos = own * kc_n + lax.broadcasted_iota(jnp.int32, (kc_n, 1), 0)
    qpos = i * tq + lax.broadcasted_iota(jnp.int32, (1, tq), 1)
    own_scores = chunk_scores(own, kpos <= qpos)
    refs = (s_ref, cm_ref, m_ref, acc_ref)
    last = jnp.maximum(own - 1, 0)
    _stage_scores(own_scores, range(s_ref.shape[0] - 1), s_ref, cm_ref)
    _attend_chunk(n_heads, own_scores, chunk_scores(0), values(own), *refs)

    def body(c, carry):
        _attend_chunk(n_heads, chunk_scores(c), chunk_scores(jnp.minimum(c + 1, last)), values(c), *refs)
        return carry

    lax.fori_loop(0, own, body, 0)
    _write_heads(o_ref, acc_ref)


def _moba(qt, k, vt, km, onehot, batch, seq):
    t, width = k.shape
    tq = MOBA_K_CHUNK
    nb = seq // MOBA_BLOCK
    nq = seq // tq
    n_heads = width // HEAD_DIM
    one = pl.Buffered(1)
    return pl.pallas_call(
        _moba_kernel,
        grid=(batch, nq),
        in_specs=[pl.BlockSpec((1, width, tq), lambda b, i: (b, 0, i)),
                  pl.BlockSpec((seq, width), lambda b, i: (b, 0), pipeline_mode=one),
                  pl.BlockSpec((1, nb, width, MOBA_BLOCK), lambda b, i: (b, 0, 0, 0), pipeline_mode=one),
                  pl.BlockSpec((nb, width), lambda b, i: (b, 0)),
                  _const_spec(onehot.shape)],
        out_specs=pl.BlockSpec((tq, width), lambda b, i: (b * nq + i, 0)),
        out_shape=jax.ShapeDtypeStruct((t, width), BF16),
        scratch_shapes=[pltpu.VMEM((n_heads, 2 * LANES, tq), BF16), pltpu.VMEM((MOBA_SLOTS, MOBA_K_CHUNK, tq), F32),
                        pltpu.VMEM((MOBA_SLOTS, SUBLANES, tq), F32),
                        pltpu.VMEM((n_heads, 1, tq), F32), pltpu.VMEM((n_heads, HEAD_DIM + ONES_ROWS, tq), F32)],
        compiler_params=_params(2),
        name="moba",
    )(qt, k, vt, km, onehot)


def _post_kernel(*refs, n_mix, final):
    h_ref = refs[0]
    y_refs = refs[1:1 + n_mix]
    (wo_ref, p_ref, gf_ref, wg_ref, wu_ref, wd_ref, gp_ref, wpg_ref, wpp_ref, gl_ref, o_ref, a_ref) = refs[1 + n_mix:]
    h = h_ref[...]
    lo = 0
    for y_ref in y_refs:
        h = h + _dot(y_ref[...], wo_ref[0, lo:lo + y_ref.shape[1], :])
        lo += y_ref.shape[1]
    xn = _rms(h, gf_ref[...]).astype(BF16)
    d_ff = wg_ref.shape[2]
    step = 2 * LANES
    for lo in range(0, d_ff, step):
        gate = _dot(xn, wg_ref[0, :, lo:lo + step])
        up = _dot(xn, wu_ref[0, :, lo:lo + step])
        a_ref[:, lo:lo + step] = (gate * jax.nn.sigmoid(gate) * up).astype(BF16)
    h = h + _dot(a_ref[...], wd_ref[0])
    pg = jax.nn.sigmoid(_dot(_rms(h, gp_ref[...]).astype(BF16), wpg_ref[0]))
    h = h + _dot(p_ref[0].astype(BF16), wpp_ref[0]) * pg
    o_ref[...] = _rms(h, gl_ref[...]) if final else h


def _layer_spec(shape, layer):
    rest = (0,) * (len(shape) - 1)
    return pl.BlockSpec((1,) + tuple(shape[1:]), lambda *_: (layer,) + rest, pipeline_mode=pl.Buffered(1))


def _post(h, ys, wo, wo_layer, p, layer, gf, wg, wu, wd, gp, wpg, wpp, gl, final):
    t, d = h.shape
    tm = ROW_TILE
    row = lambda n: pl.BlockSpec((tm, n), lambda i: (i, 0))
    stack = lambda w: _layer_spec(w.shape, layer)
    return pl.pallas_call(
        functools.partial(_post_kernel, n_mix=len(ys), final=final),
        grid=(t // tm,),
        in_specs=[row(d)] + [row(y.shape[1]) for y in ys]
                 + [_layer_spec(wo.shape, wo_layer), pl.BlockSpec((1, tm, p.shape[2]), lambda i: (layer, i, 0)),
                    _const_spec(gf.shape), stack(wg), stack(wu), stack(wd), _const_spec(gp.shape), stack(wpg),
                    stack(wpp), _const_spec(gl.shape)],
        out_specs=row(d),
        out_shape=jax.ShapeDtypeStruct((t, d), F32),
        scratch_shapes=[pltpu.VMEM((tm, wg.shape[2]), BF16)],
        compiler_params=_params(1),
        name="post_final" if final else "post",
    )(h, *ys, wo, p, gf, wg, wu, wd, gp, wpg, wpp, gl)


def _front1_kernel(h_ref, g_ref, w_ref, c_ref, sa_ref, sb_ref, lng_ref, lnb_ref,
                   qt_ref, k_ref, vt_ref, qit_ref, ki_ref, wit_ref):
    xn = _rms(h_ref[...], g_ref[...]).astype(BF16)

    def proj(lo, hi):
        return _dot(xn, w_ref[:, lo:hi])

    c, sa, sb = c_ref[...], sa_ref[...], sb_ref[...]
    qt_ref[0] = (_rope(proj(0, WIDTH_C), c, sa, sb) * (HEAD_DIM ** -0.5 * LOG2E)).T.astype(BF16)
    k_ref[...] = _rope(proj(WIDTH_C, 2 * WIDTH_C), c, sa, sb).astype(BF16)
    vt_ref[0, 0] = proj(2 * WIDTH_C, 3 * WIDTH_C).T.astype(BF16)
    o = 3 * WIDTH_C
    qit_ref[0] = _rope(proj(o, o + IDX_HEADS * IDX_DIM), c, sa, sb).T.astype(BF16)
    o += IDX_HEADS * IDX_DIM
    ki = proj(o, o + LANES)
    xc = ki - jnp.mean(ki, axis=-1, keepdims=True)
    var = jnp.mean(xc * xc, axis=-1, keepdims=True)
    ki = xc * lax.rsqrt(var + EPS) * lng_ref[...] + lnb_ref[...]
    ki_ref[...] = _rope(ki, c, sa, sb).astype(BF16)
    wi = proj(o + LANES, o + 2 * LANES) * ((IDX_HEADS ** -0.5) * (IDX_DIM ** -0.5))
    wit_ref[0] = wi.T[:IDX_HEADS]


def _front1(h, g, w, c, sa, sb, lng, lnb, batch, seq):
    t, d = h.shape
    tm = ROW_TILE
    assert tm == DSA_K_CHUNK
    nt = seq // tm
    row = lambda n: pl.BlockSpec((tm, n), lambda i: (i, 0))
    tr = lambda n: pl.BlockSpec((1, n, tm), lambda i: (i // nt, 0, i % nt))
    return pl.pallas_call(
        _front1_kernel,
        grid=(t // tm,),
        in_specs=[row(d), _const_spec((1, d)), _const_spec(w.shape), row(LANES), row(LANES), row(LANES),
                  _const_spec(lng.shape), _const_spec(lnb.shape)],
        out_specs=[tr(WIDTH_C), row(WIDTH_C),
                   pl.BlockSpec((1, 1, WIDTH_C, tm), lambda i: (i // nt, i % nt, 0, 0)),
                   tr(IDX_HEADS * IDX_DIM), row(LANES), tr(IDX_HEADS)],
        out_shape=[jax.ShapeDtypeStruct((batch, WIDTH_C, seq), BF16),
                   jax.ShapeDtypeStruct((t, WIDTH_C), BF16),
                   jax.ShapeDtypeStruct((batch, nt, WIDTH_C, tm), BF16),
                   jax.ShapeDtypeStruct((batch, IDX_HEADS * IDX_DIM, seq), BF16),
                   jax.ShapeDtypeStruct((t, LANES), BF16),
                   jax.ShapeDtypeStruct((batch, IDX_HEADS, seq), F32)],
        compiler_params=_params(1),
        name="front1",
    )(h, g, w, c, sa, sb, lng, lnb)


def _dsa_select(qit_ref, wit_ref, ki_ref, sc_ref, sc16_ref, n_sel):
    i = pl.program_id(1)
    tq = qit_ref.shape[2]
    kc_n = DSA_K_CHUNK
    n_chunks = (i * tq + tq + kc_n - 1) // kc_n
    qpos = i * tq + lax.broadcasted_iota(jnp.int32, (1, tq), 1)
    kofs = lax.broadcasted_iota(jnp.int32, (kc_n, 1), 0)
    wv = wit_ref[0]

    def causal(c):
        return (c * kc_n + kofs) <= qpos

    big = 3e38

    def score_chunk(c, carry):
        mx, mn = carry
        kc = ki_ref[pl.ds(pl.multiple_of(c * kc_n, kc_n), kc_n), :]
        acc = jnp.zeros((kc_n, tq), F32)
        for h in range(IDX_HEADS):
            qm = _head_rows(qit_ref[0, (h // 2) * LANES:(h // 2 + 1) * LANES, :], h % 2)
            acc = acc + jnp.maximum(_dot(kc, qm), 0.0) * wv[h:h + 1, :]
        ok = causal(c)
        sc_ref[c] = jnp.where(ok, acc, NEG)
        sc16_ref[c] = jnp.where(ok, acc, NEG).astype(BF16)
        return (jnp.maximum(mx, _fold_rows(jnp.where(ok, acc, -big), jnp.maximum)),
                jnp.minimum(mn, _fold_rows(jnp.where(ok, acc, big), jnp.minimum)))

    mx, mn = lax.fori_loop(0, n_chunks, score_chunk,
                           (jnp.full((SUBLANES, tq), -big, F32), jnp.full((SUBLANES, tq), big, F32)))
    col_max = jnp.max(mx, axis=0, keepdims=True)
    col_min = jnp.min(mn, axis=0, keepdims=True)

    def reduce_chunks(fn, op, init):
        def body(c, acc):
            return op(acc, _fold_rows(fn(sc_ref[c]), op))
        return lax.fori_loop(0, n_chunks, body, jnp.full((SUBLANES, tq), init, F32))

    def count_ge(thr):
        acc = reduce_chunks(lambda x: jnp.where(x >= thr, 1.0, 0.0), jnp.add, 0.0)
        return jnp.sum(acc, axis=0, keepdims=True)

    def pending(flags):
        return jnp.max(jnp.where(flags, 0.0, 1.0))

    k_f = float(n_sel)
    n_causal = (qpos + 1).astype(F32)
    small = n_causal < k_f
    rows16 = 2 * SUBLANES

    def count_ge_bf16(thr):
        one, zero = jnp.ones((), BF16), jnp.zeros((), BF16)

        def body(c, acc):
            hits = jnp.where(sc16_ref[c] >= thr, one, zero)
            return acc + _fold_rows(hits, jnp.add, rows16).astype(F32)

        return jnp.sum(lax.fori_loop(0, n_chunks, body, jnp.zeros((rows16, tq), F32)), axis=0, keepdims=True)

    def coarse(_, carry):
        lo, hi = carry
        mid = (0.5 * (lo + hi)).astype(BF16)
        ok = count_ge_bf16(mid) >= k_f
        mid = mid.astype(F32)
        return jnp.where(ok, mid, lo), jnp.where(ok, hi, mid)

    top = col_max.astype(BF16).astype(F32)
    lo, hi = lax.fori_loop(0, COARSE_STEPS, coarse,
                           (col_min.astype(BF16).astype(F32),
                            (top + jnp.abs(top) * 2.0 * BF16_SPACING + TINY).astype(BF16).astype(F32)))
    lo = jnp.where(small, NEG, lo - jnp.abs(lo) * BF16_SPACING - TINY)
    c_lo = jnp.full((1, tq), 2.0 * k_f, F32)
    c_hi = jnp.full((1, tq), -1.0, F32)

    def bisect(_, carry):
        lo, hi, c_lo, c_hi = carry
        mid = 0.5 * (lo + hi)
        cnt = count_ge(mid)
        ok = cnt >= k_f
        return (jnp.where(ok, mid, lo), jnp.where(ok, hi, mid), jnp.where(ok, cnt, c_lo), jnp.where(ok, c_hi, cnt))

    def bracket_ends(lo, hi):
        def body(c, carry):
            below, above = carry
            x = sc_ref[c]
            return (jnp.maximum(below, _fold_rows(jnp.where(x < hi, x, -big), jnp.maximum)),
                    jnp.minimum(above, _fold_rows(jnp.where(x >= lo, x, big), jnp.minimum)))

        below, above = lax.fori_loop(0, n_chunks, body, (jnp.full((SUBLANES, tq), -big, F32),
                                                         jnp.full((SUBLANES, tq), big, F32)))
        return jnp.max(below, axis=0, keepdims=True), jnp.min(above, axis=0, keepdims=True)

    def settled(c_lo, c_hi, below, above):
        return small | (c_hi == k_f - 1.0) | (c_lo == k_f) | (below == above)

    def narrow_cond(carry):
        return (carry[6] < BISECT_ROUNDS) & (carry[7] > 0.5)

    def narrow(carry):
        state = lax.fori_loop(0, BISECT_PER_ROUND, bisect, carry[:4])
        below, above = bracket_ends(state[0], state[1])
        return (*state, below, above, carry[6] + 1, pending(settled(state[2], state[3], below, above)))

    zeros = jnp.zeros((1, tq), F32)
    lo, hi, c_lo, c_hi, below, above, _, _ = lax.while_loop(
        narrow_cond, narrow, (lo, hi, c_lo, c_hi, zeros, zeros, jnp.int32(0), jnp.float32(1.0)))
    thr = jnp.where(small, NEG, jnp.where(c_lo == k_f, above, below))
    done = jnp.where(settled(c_lo, c_hi, below, above), 1.0, 0.0)

    def snap_cond(carry):
        return carry[3] > 0.5

    def snap(carry):
        hi, thr, done, _ = carry
        cand = jnp.max(reduce_chunks(lambda x: jnp.where(x < hi, x, -big), jnp.maximum, -big),
                       axis=0, keepdims=True)
        ok = count_ge(cand) >= k_f
        thr = jnp.where(done > 0.5, thr, cand)
        done = jnp.where(ok, 1.0, done)
        hi = jnp.where(done > 0.5, hi, cand)
        return hi, thr, done, jnp.max(1.0 - done)

    _, thr, _, _ = lax.while_loop(snap_cond, snap, (hi, thr, done, jnp.max(1.0 - done)))

    n_ge = count_ge(thr)
    tied = jnp.max(jnp.where((n_ge > k_f) & jnp.logical_not(small), 1.0, 0.0))

    @pl.when(tied < 0.5)
    def _():
        def body(c, carry):
            x = sc_ref[c]
            sc_ref[c] = jnp.where((x >= thr) & causal(c), 0.0, NEG)
            return carry
        lax.fori_loop(0, n_chunks, body, 0)

    @pl.when(tied > 0.5)
    def _():
        n_gt = reduce_chunks(lambda x: jnp.where(x > thr, 1.0, 0.0), jnp.add, 0.0)
        need = k_f - jnp.sum(n_gt, axis=0, keepdims=True)

        def body(c, seen):
            x = sc_ref[c]
            eq = x == thr
            total = seen + jnp.sum(_fold_rows(jnp.where(eq, 1.0, 0.0), jnp.add), axis=0, keepdims=True)
            keep_all = total <= need
            crossing = jnp.max(jnp.where(keep_all | (seen >= need), 0.0, 1.0))

            @pl.when(crossing < 0.5)
            def _():
                sc_ref[c] = jnp.where(((x > thr) | (eq & keep_all)) & causal(c), 0.0, NEG)

            @pl.when(crossing > 0.5)
            def _():
                r = lax.broadcasted_iota(jnp.int32, (kc_n, kc_n), 0)
                cc = lax.broadcasted_iota(jnp.int32, (kc_n, kc_n), 1)
                prefix = jnp.where(cc <= r, 1.0, 0.0).astype(BF16)
                cnt = seen + _dot(prefix, jnp.where(eq, 1.0, 0.0).astype(BF16))
                sc_ref[c] = jnp.where(((x > thr) | (eq & (cnt <= need))) & causal(c), 0.0, NEG)

            return total
        lax.fori_loop(0, n_chunks, body, jnp.zeros((1, tq), F32))


def _dsa_kernel(qit_ref, wit_ref, ki_ref, qt_ref, k_ref, vt_ref, o_ref,
                sc_ref, sc16_ref, qm_ref, s_ref, cm_ref, m_ref, acc_ref, *, n_sel):
    i = pl.program_id(1)
    tq = qt_ref.shape[2]
    kc_n = DSA_K_CHUNK
    n_heads = acc_ref.shape[0]
    n_chunks = (i * tq + tq + kc_n - 1) // kc_n
    _dsa_select(qit_ref, wit_ref, ki_ref, sc_ref, sc16_ref, n_sel)
    _softmax_init(m_ref, acc_ref)
    for h in range(n_heads):
        qm_ref[h] = _head_rows(qt_ref[0, (h // 2) * LANES:(h // 2 + 1) * LANES, :], h % 2)

    def chunk_scores(c):
        st = pl.multiple_of(c * kc_n, kc_n)

        def scores(h):
            return _dot(k_ref[pl.ds(st, kc_n), (h // 2) * LANES:(h // 2 + 1) * LANES], qm_ref[h]) + sc_ref[c]

        return scores

    _stage_scores(chunk_scores(0), range(s_ref.shape[0] - 1), s_ref, cm_ref)

    def body(c, carry):
        def values(h):
            return vt_ref[0, c, h * HEAD_DIM:(h + 1) * HEAD_DIM, :]

        _attend_chunk(n_heads, chunk_scores(c), chunk_scores(jnp.minimum(c + 1, n_chunks - 1)), values,
                      s_ref, cm_ref, m_ref, acc_ref)
        return carry

    lax.fori_loop(0, n_chunks, body, 0)
    _write_heads(o_ref, acc_ref)


def _dsa(qit, wit, ki, qt, k, vt, batch, seq):
    t, width = k.shape
    tq = Q_TILE
    nq = seq // tq
    nch = seq // DSA_K_CHUNK
    n_heads = width // HEAD_DIM
    one = pl.Buffered(1)
    return pl.pallas_call(
        functools.partial(_dsa_kernel, n_sel=min(DSA_TOPK, seq // 4)),
        grid=(batch, nq),
        in_specs=[pl.BlockSpec((1, qit.shape[1], tq), lambda b, i: (b, 0, i)),
                  pl.BlockSpec((1, IDX_HEADS, tq), lambda b, i: (b, 0, i)),
                  pl.BlockSpec((seq, LANES), lambda b, i: (b, 0), pipeline_mode=one),
                  pl.BlockSpec((1, width, tq), lambda b, i: (b, 0, i)),
                  pl.BlockSpec((seq, width), lambda b, i: (b, 0), pipeline_mode=one),
                  pl.BlockSpec((1,) + vt.shape[1:], lambda b, i: (b, 0, 0, 0), pipeline_mode=one)],
        out_specs=pl.BlockSpec((tq, width), lambda b, i: (b * nq + i, 0)),
        out_shape=jax.ShapeDtypeStruct((t, width), BF16),
        scratch_shapes=[pltpu.VMEM((nch, DSA_K_CHUNK, tq), F32), pltpu.VMEM((nch, DSA_K_CHUNK, tq), BF16),
                        pltpu.VMEM((n_heads, LANES, tq), BF16), pltpu.VMEM((SCORE_SLOTS, DSA_K_CHUNK, tq), F32),
                        pltpu.VMEM((SCORE_SLOTS, SUBLANES, tq), F32),
                        pltpu.VMEM((n_heads, 1, tq), F32), pltpu.VMEM((n_heads, HEAD_DIM + ONES_ROWS, tq), F32)],
        compiler_params=_params(2),
        name="dsa",
    )(qit, wit, ki, qt, k, vt)


def _rope_tables(positions):
    inv_freq = ROPE_THETA ** (-jnp.arange(0, ROPE_DIM, 2, dtype=F32) / ROPE_DIM)
    m = jnp.arange(LANES) % HEAD_DIM
    ang = positions.astype(F32).reshape(-1, 1) * inv_freq[m % ROPE_HALF][None, :]
    cos, sin = jnp.cos(ang), jnp.sin(ang)
    c = jnp.where(m < ROPE_DIM, cos, 1.0)
    sa = jnp.where(m < ROPE_HALF, -sin, 0.0)
    sb = jnp.where((m >= ROPE_HALF) & (m < ROPE_DIM), sin, 0.0)
    return c, sa, sb


def kernel(x, p, positions, g_mix, w_in_even, a_ln_g, a_ln_b, a_w_s, a_b_s, w_out_even, w_in_odd, c_kidx_ln_g, c_kidx_ln_b, w_out_odd, g_ffn, w_ffn_gate, w_ffn_up, w_ffn_down, g_ple, w_ple_proj, w_ple_gate, g_final):
    batch, seq, d = x.shape
    depth = p.shape[0]
    t = batch * seq
    assert seq % ROW_TILE == 0 and seq % DSA_K_CHUNK == 0 and min(DSA_TOPK, seq // 4) <= DSA_K_CHUNK
    c, sa, sb = _rope_tables(positions)
    h = x.reshape(t, d)
    p2 = p.reshape(depth, t, p.shape[-1])
    row = lambda a: a.reshape(1, -1)
    onehot = (jnp.arange(seq)[:, None] // MOBA_BLOCK == jnp.arange(LANES)[None, :]).astype(BF16)
    bf = lambda w: w.astype(BF16)
    wo_even, wo_odd, wg, wu, wd, wpg, wpp = map(bf, (w_out_even, w_out_odd, w_ffn_gate, w_ffn_up, w_ffn_down,
                                                     w_ple_gate, w_ple_proj))
    for i in range(depth):
        j = i // 2
        if i % 2 == 0:
            bs = jnp.repeat(a_b_s[j].T, HEAD_DIM, axis=1)
            ya, qt, k, vt, km = _front0(h, row(g_mix[i]), bf(w_in_even[j]), c, sa, sb,
                                        row(a_ln_g[j]), row(a_ln_b[j]), a_w_s[j], bs, batch, seq)
            ys = [ya, _moba(qt, k, vt, km.reshape(t // MOBA_BLOCK, WIDTH_B), onehot, batch, seq)]
            wo = wo_even
        else:
            w = w_in_odd[j]
            o = 3 * WIDTH_C + IDX_HEADS * IDX_DIM
            w1 = bf(jnp.concatenate([w[:, :o], w[:, o:o + IDX_DIM], w[:, o:o + IDX_DIM], w[:, o + IDX_DIM:],
                                     jnp.zeros((d, LANES - IDX_HEADS), w.dtype)], axis=1))
            qt, k, vt, qit, ki, wit = _front1(h, row(g_mix[i]), w1, c, sa, sb, row(jnp.tile(c_kidx_ln_g[j], 2)),
                                              row(jnp.tile(c_kidx_ln_b[j], 2)), batch, seq)
            ys = [_dsa(qit, wit, ki, qt, k, vt, batch, seq)]
            wo = wo_odd
        h = _post(h, ys, wo, j, p2, i, row(g_ffn[i]), wg, wu, wd, row(g_ple[i]), wpg, wpp, row(g_final),
                  final=(i == depth - 1))
    return h.reshape(batch, seq, d)
```
